```python
import math
import jax, jax.numpy as jnp
from jax import lax
import numpy as np


D_MODEL = 1024
BATCH = 32
SEQ = 2048
DEPTH = 4

CHUNK = 64
MIX_WIDTH = D_MODEL
POOL_WIDTH = MIX_WIDTH // 2
POOL_WINDOWS = (2, 4, 8, 16)
POOL_GROUPS = len(POOL_WINDOWS)
POOL_GROUP = POOL_WIDTH // POOL_GROUPS
DN_HEAD_DIM = 128
DN_HEADS = (MIX_WIDTH - POOL_WIDTH) // DN_HEAD_DIM
DN_WIDTH = DN_HEADS * DN_HEAD_DIM
DN_CONV = 4
SGU_WIDTH = MIX_WIDTH // 2
SGU_BLOCK = 128
SGU_HEADS = 4
SGU_HEAD_CH = SGU_WIDTH // SGU_HEADS
SC_WIDTH = MIX_WIDTH - SGU_WIDTH
SC_CONV = 3
FFN_DIM = ((8 * D_MODEL // 3 + 127) // 128) * 128
AB_IN = POOL_WIDTH + 4 * DN_WIDTH + 2 * DN_HEADS
CD_IN = 2 * SGU_WIDTH + 3 * SC_WIDTH
N_EVEN = (DEPTH + 1) // 2
N_ODD = DEPTH // 2
EPS = 1e-6

kernel_name = 'hybrid_chunk_causal_encoder'


def rmsnorm(x, g):
    xf = x.astype(jnp.float32)
    y = xf * lax.rsqrt(jnp.mean(xf * xf, axis=-1, keepdims=True) + EPS)
    return (y * g.astype(jnp.float32)).astype(x.dtype)


def layernorm(x, g, b):
    xf = x.astype(jnp.float32)
    mu = jnp.mean(xf, axis=-1, keepdims=True)
    xc = xf - mu
    y = xc * lax.rsqrt(jnp.mean(xc * xc, axis=-1, keepdims=True) + EPS)
    return (y * g.astype(jnp.float32) + b.astype(jnp.float32)).astype(x.dtype)


def l2norm(x):
    return x * lax.rsqrt(jnp.sum(x * x, axis=-1, keepdims=True) + EPS)


def causal_depthwise_conv(x, w):
    K, C = w.shape
    return lax.conv_general_dilated(
        x, w[:, None, :].astype(x.dtype), window_strides=(1,), padding=[(K - 1, 0)],
        dimension_numbers=('NWC', 'WIO', 'NWC'), feature_group_count=C)


def swiglu(x, w_gate, w_up, w_down):
    return (jax.nn.silu(x @ w_gate) * (x @ w_up)) @ w_down


def pool_mixer(a, w_pool, scale):
    B, S, _ = a.shape
    grp = a.astype(jnp.float32).reshape(B, S, POOL_GROUPS, POOL_GROUP)
    cs = jnp.cumsum(grp, axis=1)
    pos = jnp.arange(S)
    outs = []
    for gi, w in enumerate(POOL_WINDOWS):
        c = cs[:, :, gi]
        lagged = jnp.pad(c, ((0, 0), (w, 0), (0, 0)))[:, :S]
        cnt = jnp.minimum(pos + 1, w).astype(jnp.float32)[None, :, None]
        outs.append((c - lagged) / cnt - grp[:, :, gi])
    pooled = jnp.stack(outs, axis=2).astype(a.dtype)
    mixed = jnp.einsum('bsgc,gcd->bsgd', pooled, w_pool)
    return mixed.reshape(B, S, POOL_WIDTH) * scale


def gated_delta_rule(q, k, v, beta, g):
    B, S, H, dk = q.shape
    dv = v.shape[-1]
    N = S // CHUNK
    def chunks(t):
        t = jnp.swapaxes(t, 1, 2)
        return t.reshape((B, H, N, CHUNK) + t.shape[3:])
    q, k, v, beta, g = chunks(q), chunks(k), chunks(v), chunks(beta), chunks(g)
    gc = jnp.cumsum(g, axis=-1)
    tril = jnp.tril(jnp.ones((CHUNK, CHUNK), dtype=bool))
    strict = jnp.tril(jnp.ones((CHUNK, CHUNK), dtype=bool), k=-1)
    gamma = jnp.exp(jnp.where(tril, gc[..., :, None] - gc[..., None, :], -jnp.inf))
    kb = k * beta[..., None]
    lmat = jnp.where(strict, jnp.einsum('bhnid,bhnjd->bhnij', kb, k) * gamma, 0.0)
    eye = jnp.eye(CHUNK, dtype=q.dtype)
    rhs = jnp.concatenate([v * beta[..., None], kb * jnp.exp(gc)[..., None]], axis=-1)
    sol = lax.linalg.triangular_solve(eye + lmat, rhs, left_side=True, lower=True,
                                      unit_diagonal=True)
    u, w = sol[..., :dv], sol[..., dv:]
    aqk = jnp.einsum('bhnid,bhnjd->bhnij', q, k) * gamma
    q_dec = q * jnp.exp(gc)[..., None]
    k_dec = k * jnp.exp(gc[..., -1:] - gc)[..., None]
    last = jnp.exp(gc[..., -1])

    def step(state, xs):
        u_i, w_i, qd_i, a_i, kd_i, l_i = xs
        v_new = u_i - jnp.einsum('bhcd,bhde->bhce', w_i, state)
        o_i = (jnp.einsum('bhcd,bhde->bhce', qd_i, state)
               + jnp.einsum('bhij,bhje->bhie', a_i, v_new))
        state = state * l_i[..., None, None] + jnp.einsum('bhcd,bhce->bhde', kd_i, v_new)
        return state, o_i

    xs = tuple(jnp.moveaxis(t, 2, 0) for t in (u, w, q_dec, aqk, k_dec, last))
    s0 = jnp.zeros((B, H, dk, dv), q.dtype)
    _, o = lax.scan(step, s0, xs)
    return o.transpose(1, 0, 3, 2, 4).reshape(B, S, H, dv)


def mixer_ab(h, w_in, pool_w, pool_scale, conv_w, a_log, dt_bias, out_norm, w_out):
    B, S, _ = h.shape
    proj = h @ w_in
    o0 = POOL_WIDTH
    o1 = o0 + 3 * DN_WIDTH
    o2 = o1 + DN_WIDTH
    o3 = o2 + DN_HEADS
    a_in, qkv, z = proj[..., :o0], proj[..., o0:o1], proj[..., o1:o2]
    b_raw, g_raw = proj[..., o2:o3], proj[..., o3:]
    y_a = pool_mixer(a_in, pool_w, pool_scale)
    qkv = jax.nn.silu(causal_depthwise_conv(qkv, conv_w)).astype(jnp.float32)
    q, k, v = (t.reshape(B, S, DN_HEADS, DN_HEAD_DIM) for t in jnp.split(qkv, 3, axis=-1))
    q = l2norm(q) * (DN_HEAD_DIM ** -0.5)
    k = l2norm(k)
    beta = jax.nn.sigmoid(b_raw.astype(jnp.float32))
    g = -jnp.exp(a_log.astype(jnp.float32)) * jax.nn.softplus(
        g_raw.astype(jnp.float32) + dt_bias.astype(jnp.float32))
    o = gated_delta_rule(q, k, v, beta, g)
    o = rmsnorm(o, out_norm) * jax.nn.silu(z.reshape(B, S, DN_HEADS, DN_HEAD_DIM).astype(jnp.float32))
    y_b = o.reshape(B, S, DN_WIDTH).astype(h.dtype)
    return jnp.concatenate([y_a, y_b], axis=-1) @ w_out


def mixer_cd(h, w_in, sgu_norm_g, sgu_norm_b, sgu_w, sgu_bias, conv_w, w_out):
    B, S, _ = h.shape
    proj = h @ w_in
    uv = jax.nn.gelu(proj[..., :2 * SGU_WIDTH])
    u, v = uv[..., :SGU_WIDTH], uv[..., SGU_WIDTH:]
    v = layernorm(v, sgu_norm_g, sgu_norm_b)
    nb = S // SGU_BLOCK
    vb = v.reshape(B, nb, SGU_BLOCK, SGU_HEADS, SGU_HEAD_CH)
    mask = jnp.tril(jnp.ones((SGU_BLOCK, SGU_BLOCK), dtype=bool))
    ws = jnp.where(mask, sgu_w, 0.0).astype(v.dtype)
    mixed = jnp.einsum('hij,bnjhc->bnihc', ws, vb) + sgu_bias.T[None, None, :, :, None]
    y_c = u * mixed.reshape(B, S, SGU_WIDTH)
    sc = proj[..., 2 * SGU_WIDTH:]
    xd, bg, cg = sc[..., :SC_WIDTH], sc[..., SC_WIDTH:2 * SC_WIDTH], sc[..., 2 * SC_WIDTH:]
    y_d = bg * causal_depthwise_conv(cg * xd, conv_w)
    return jnp.concatenate([y_c, y_d], axis=-1) @ w_out


def _fwd_setup_inputs(seed: int = 0) -> dict:
    key = jax.random.key(seed)
    ks = jax.random.split(key, 32)
    f32 = jnp.float32
    def nrm(k, shape, scale):
        return jax.random.normal(k, shape, f32) * scale
    def gain(k, shape):
        return 1.0 + 0.02 * jax.random.normal(k, shape, f32)
    D, F = D_MODEL, FFN_DIM
    dt = jnp.exp(jax.random.uniform(ks[14], (N_EVEN, DN_HEADS), f32,
                                    minval=math.log(1e-3), maxval=math.log(1e-1)))
    return {
        'x': jax.random.normal(ks[0], (BATCH, SEQ, D), f32),
        'ffn1_norm': gain(ks[1], (DEPTH, D)),
        'ffn1_w_gate': nrm(ks[2], (DEPTH, D, F), D ** -0.5),
        'ffn1_w_up': nrm(ks[3], (DEPTH, D, F), D ** -0.5),
        'ffn1_w_down': nrm(ks[4], (DEPTH, F, D), F ** -0.5),
        'mix_norm': gain(ks[5], (DEPTH, D)),
        'ffn2_norm': gain(ks[6], (DEPTH, D)),
        'ffn2_w_gate': nrm(ks[7], (DEPTH, D, F), D ** -0.5),
        'ffn2_w_up': nrm(ks[8], (DEPTH, D, F), D ** -0.5),
        'ffn2_w_down': nrm(ks[9], (DEPTH, F, D), F ** -0.5),
        'ab_w_in': nrm(ks[10], (N_EVEN, D, AB_IN), D ** -0.5),
        'pool_w': nrm(ks[11], (N_EVEN, POOL_GROUPS, POOL_GROUP, POOL_GROUP), POOL_GROUP ** -0.5),
        'pool_scale': gain(ks[12], (N_EVEN, POOL_WIDTH)),
        'dn_conv_w': nrm(ks[13], (N_EVEN, DN_CONV, 3 * DN_WIDTH), DN_CONV ** -0.5),
        'dn_a_log': jnp.log(jax.random.uniform(ks[15], (N_EVEN, DN_HEADS), f32, minval=1.0, maxval=16.0)),
        'dn_dt_bias': dt + jnp.log(-jnp.expm1(-dt)),
        'dn_out_norm': gain(ks[16], (N_EVEN, DN_HEAD_DIM)),
        'ab_w_out': nrm(ks[17], (N_EVEN, MIX_WIDTH, D), MIX_WIDTH ** -0.5),
        'cd_w_in': nrm(ks[18], (N_ODD, D, CD_IN), D ** -0.5),
        'sgu_norm_g': gain(ks[19], (N_ODD, SGU_WIDTH)),
        'sgu_norm_b': nrm(ks[20], (N_ODD, SGU_WIDTH), 0.02),
        'sgu_w': nrm(ks[21], (N_ODD, SGU_HEADS, SGU_BLOCK, SGU_BLOCK), SGU_BLOCK ** -0.5),
        'sgu_bias': gain(ks[22], (N_ODD, SGU_HEADS, SGU_BLOCK)),
        'sc_conv_w': nrm(ks[23], (N_ODD, SC_CONV, SC_WIDTH), SC_CONV ** -0.5),
        'cd_w_out': nrm(ks[24], (N_ODD, MIX_WIDTH, D), MIX_WIDTH ** -0.5),
        'final_norm': gain(ks[25], (D,)),
    }


def _fwd_reference(x, ffn1_norm, ffn1_w_gate, ffn1_w_up, ffn1_w_down, mix_norm,
              ffn2_norm, ffn2_w_gate, ffn2_w_up, ffn2_w_down,
              ab_w_in, pool_w, pool_scale, dn_conv_w, dn_a_log, dn_dt_bias, dn_out_norm, ab_w_out,
              cd_w_in, sgu_norm_g, sgu_norm_b, sgu_w, sgu_bias, sc_conv_w, cd_w_out,
              final_norm):
    h = x
    for layer in range(DEPTH):
        h = h + 0.5 * swiglu(rmsnorm(h, ffn1_norm[layer]), ffn1_w_gate[layer],
                             ffn1_w_up[layer], ffn1_w_down[layer])
        hn = rmsnorm(h, mix_norm[layer])
        if layer % 2 == 0:
            e = layer // 2
            h = h + mixer_ab(hn, ab_w_in[e], pool_w[e], pool_scale[e], dn_conv_w[e],
                             dn_a_log[e], dn_dt_bias[e], dn_out_norm[e], ab_w_out[e])
        else:
            o = layer // 2
            h = h + mixer_cd(hn, cd_w_in[o], sgu_norm_g[o], sgu_norm_b[o], sgu_w[o],
                             sgu_bias[o], sc_conv_w[o], cd_w_out[o])
        h = h + 0.5 * swiglu(rmsnorm(h, ffn2_norm[layer]), ffn2_w_gate[layer],
                             ffn2_w_up[layer], ffn2_w_down[layer])
    return rmsnorm(h, final_norm)


import jax as _jax
import jax.numpy as _jnp

TWIN_FORMAT = 'train_step'
FWD_PARAMS = ['x', 'ffn1_norm', 'ffn1_w_gate', 'ffn1_w_up', 'ffn1_w_down', 'mix_norm', 'ffn2_norm', 'ffn2_w_gate', 'ffn2_w_up', 'ffn2_w_down', 'ab_w_in', 'pool_w', 'pool_scale', 'dn_conv_w', 'dn_a_log', 'dn_dt_bias', 'dn_out_norm', 'ab_w_out', 'cd_w_in', 'sgu_norm_g', 'sgu_norm_b', 'sgu_w', 'sgu_bias', 'sc_conv_w', 'cd_w_out', 'final_norm']
TWIN_WEIGHTS = ['ffn1_norm', 'ffn1_w_gate', 'ffn1_w_up', 'ffn1_w_down', 'mix_norm', 'ffn2_norm', 'ffn2_w_gate', 'ffn2_w_up', 'ffn2_w_down', 'ab_w_in', 'pool_w', 'pool_scale', 'dn_conv_w', 'dn_a_log', 'dn_dt_bias', 'dn_out_norm', 'ab_w_out', 'cd_w_in', 'sgu_norm_g', 'sgu_norm_b', 'sgu_w', 'sgu_bias', 'sc_conv_w', 'cd_w_out', 'final_norm']
TWIN_DIFF_INPUT = 'x'
TWIN_INPUTS = ['x', 'ffn1_norm', 'ffn1_w_gate', 'ffn1_w_up', 'ffn1_w_down', 'mix_norm', 'ffn2_norm', 'ffn2_w_gate', 'ffn2_w_up', 'ffn2_w_down', 'ab_w_in', 'pool_w', 'pool_scale', 'dn_conv_w', 'dn_a_log', 'dn_dt_bias', 'dn_out_norm', 'ab_w_out', 'cd_w_in', 'sgu_norm_g', 'sgu_norm_b', 'sgu_w', 'sgu_bias', 'sc_conv_w', 'cd_w_out', 'final_norm', 'loss_target', 'm_ffn1_norm', 'm_ffn1_w_gate', 'm_ffn1_w_up', 'm_ffn1_w_down', 'm_mix_norm', 'm_ffn2_norm', 'm_ffn2_w_gate', 'm_ffn2_w_up', 'm_ffn2_w_down', 'm_ab_w_in', 'm_pool_w', 'm_pool_scale', 'm_dn_conv_w', 'm_dn_a_log', 'm_dn_dt_bias', 'm_dn_out_norm', 'm_ab_w_out', 'm_cd_w_in', 'm_sgu_norm_g', 'm_sgu_norm_b', 'm_sgu_w', 'm_sgu_bias', 'm_sc_conv_w', 'm_cd_w_out', 'm_final_norm', 'v_ffn1_norm', 'v_ffn1_w_gate', 'v_ffn1_w_up', 'v_ffn1_w_down', 'v_mix_norm', 'v_ffn2_norm', 'v_ffn2_w_gate', 'v_ffn2_w_up', 'v_ffn2_w_down', 'v_ab_w_in', 'v_pool_w', 'v_pool_scale', 'v_dn_conv_w', 'v_dn_a_log', 'v_dn_dt_bias', 'v_dn_out_norm', 'v_ab_w_out', 'v_cd_w_in', 'v_sgu_norm_g', 'v_sgu_norm_b', 'v_sgu_w', 'v_sgu_bias', 'v_sc_conv_w', 'v_cd_w_out', 'v_final_norm']
TWIN_OUTPUTS = ['loss', 'grad_x', 'grad_ffn1_norm', 'grad_ffn1_w_gate', 'grad_ffn1_w_up', 'grad_ffn1_w_down', 'grad_mix_norm', 'grad_ffn2_norm', 'grad_ffn2_w_gate', 'grad_ffn2_w_up', 'grad_ffn2_w_down', 'grad_ab_w_in', 'grad_pool_w', 'grad_pool_scale', 'grad_dn_conv_w', 'grad_dn_a_log', 'grad_dn_dt_bias', 'grad_dn_out_norm', 'grad_ab_w_out', 'grad_cd_w_in', 'grad_sgu_norm_g', 'grad_sgu_norm_b', 'grad_sgu_w', 'grad_sgu_bias', 'grad_sc_conv_w', 'grad_cd_w_out', 'grad_final_norm', 'delta_ffn1_norm', 'delta_ffn1_w_gate', 'delta_ffn1_w_up', 'delta_ffn1_w_down', 'delta_mix_norm', 'delta_ffn2_norm', 'delta_ffn2_w_gate', 'delta_ffn2_w_up', 'delta_ffn2_w_down', 'delta_ab_w_in', 'delta_pool_w', 'delta_pool_scale', 'delta_dn_conv_w', 'delta_dn_a_log', 'delta_dn_dt_bias', 'delta_dn_out_norm', 'delta_ab_w_out', 'delta_cd_w_in', 'delta_sgu_norm_g', 'delta_sgu_norm_b', 'delta_sgu_w', 'delta_sgu_bias', 'delta_sc_conv_w', 'delta_cd_w_out', 'delta_final_norm', 'new_m_ffn1_norm', 'new_m_ffn1_w_gate', 'new_m_ffn1_w_up', 'new_m_ffn1_w_down', 'new_m_mix_norm', 'new_m_ffn2_norm', 'new_m_ffn2_w_gate', 'new_m_ffn2_w_up', 'new_m_ffn2_w_down', 'new_m_ab_w_in', 'new_m_pool_w', 'new_m_pool_scale', 'new_m_dn_conv_w', 'new_m_dn_a_log', 'new_m_dn_dt_bias', 'new_m_dn_out_norm', 'new_m_ab_w_out', 'new_m_cd_w_in', 'new_m_sgu_norm_g', 'new_m_sgu_norm_b', 'new_m_sgu_w', 'new_m_sgu_bias', 'new_m_sc_conv_w', 'new_m_cd_w_out', 'new_m_final_norm', 'new_v_ffn1_norm', 'new_v_ffn1_w_gate', 'new_v_ffn1_w_up', 'new_v_ffn1_w_down', 'new_v_mix_norm', 'new_v_ffn2_norm', 'new_v_ffn2_w_gate', 'new_v_ffn2_w_up', 'new_v_ffn2_w_down', 'new_v_ab_w_in', 'new_v_pool_w', 'new_v_pool_scale', 'new_v_dn_conv_w', 'new_v_dn_a_log', 'new_v_dn_dt_bias', 'new_v_dn_out_norm', 'new_v_ab_w_out', 'new_v_cd_w_in', 'new_v_sgu_norm_g', 'new_v_sgu_norm_b', 'new_v_sgu_w', 'new_v_sgu_bias', 'new_v_sc_conv_w', 'new_v_cd_w_out', 'new_v_final_norm']
TWIN_LEAF_KINDS = {'loss': 'loss', 'grad_x': 'grad_x', 'grad_ffn1_norm': 'grad_w', 'grad_ffn1_w_gate': 'grad_w', 'grad_ffn1_w_up': 'grad_w', 'grad_ffn1_w_down': 'grad_w', 'grad_mix_norm': 'grad_w', 'grad_ffn2_norm': 'grad_w', 'grad_ffn2_w_gate': 'grad_w', 'grad_ffn2_w_up': 'grad_w', 'grad_ffn2_w_down': 'grad_w', 'grad_ab_w_in': 'grad_w', 'grad_pool_w': 'grad_w', 'grad_pool_scale': 'grad_w', 'grad_dn_conv_w': 'grad_w', 'grad_dn_a_log': 'grad_w', 'grad_dn_dt_bias': 'grad_w', 'grad_dn_out_norm': 'grad_w', 'grad_ab_w_out': 'grad_w', 'grad_cd_w_in': 'grad_w', 'grad_sgu_norm_g': 'grad_w', 'grad_sgu_norm_b': 'grad_w', 'grad_sgu_w': 'grad_w', 'grad_sgu_bias': 'grad_w', 'grad_sc_conv_w': 'grad_w', 'grad_cd_w_out': 'grad_w', 'grad_final_norm': 'grad_w', 'delta_ffn1_norm': 'delta_w', 'delta_ffn1_w_gate': 'delta_w', 'delta_ffn1_w_up': 'delta_w', 'delta_ffn1_w_down': 'delta_w', 'delta_mix_norm': 'delta_w', 'delta_ffn2_norm': 'delta_w', 'delta_ffn2_w_gate': 'delta_w', 'delta_ffn2_w_up': 'delta_w', 'delta_ffn2_w_down': 'delta_w', 'delta_ab_w_in': 'delta_w', 'delta_pool_w': 'delta_w', 'delta_pool_scale': 'delta_w', 'delta_dn_conv_w': 'delta_w', 'delta_dn_a_log': 'delta_w', 'delta_dn_dt_bias': 'delta_w', 'delta_dn_out_norm': 'delta_w', 'delta_ab_w_out': 'delta_w', 'delta_cd_w_in': 'delta_w', 'delta_sgu_norm_g': 'delta_w', 'delta_sgu_norm_b': 'delta_w', 'delta_sgu_w': 'delta_w', 'delta_sgu_bias': 'delta_w', 'delta_sc_conv_w': 'delta_w', 'delta_cd_w_out': 'delta_w', 'delta_final_norm': 'delta_w', 'new_m_ffn1_norm': 'new_m', 'new_m_ffn1_w_gate': 'new_m', 'new_m_ffn1_w_up': 'new_m', 'new_m_ffn1_w_down': 'new_m', 'new_m_mix_norm': 'new_m', 'new_m_ffn2_norm': 'new_m', 'new_m_ffn2_w_gate': 'new_m', 'new_m_ffn2_w_up': 'new_m', 'new_m_ffn2_w_down': 'new_m', 'new_m_ab_w_in': 'new_m', 'new_m_pool_w': 'new_m', 'new_m_pool_scale': 'new_m', 'new_m_dn_conv_w': 'new_m', 'new_m_dn_a_log': 'new_m', 'new_m_dn_dt_bias': 'new_m', 'new_m_dn_out_norm': 'new_m', 'new_m_ab_w_out': 'new_m', 'new_m_cd_w_in': 'new_m', 'new_m_sgu_norm_g': 'new_m', 'new_m_sgu_norm_b': 'new_m', 'new_m_sgu_w': 'new_m', 'new_m_sgu_bias': 'new_m', 'new_m_sc_conv_w': 'new_m', 'new_m_cd_w_out': 'new_m', 'new_m_final_norm': 'new_m', 'new_v_ffn1_norm': 'new_v', 'new_v_ffn1_w_gate': 'new_v', 'new_v_ffn1_w_up': 'new_v', 'new_v_ffn1_w_down': 'new_v', 'new_v_mix_norm': 'new_v', 'new_v_ffn2_norm': 'new_v', 'new_v_ffn2_w_gate': 'new_v', 'new_v_ffn2_w_up': 'new_v', 'new_v_ffn2_w_down': 'new_v', 'new_v_ab_w_in': 'new_v', 'new_v_pool_w': 'new_v', 'new_v_pool_scale': 'new_v', 'new_v_dn_conv_w': 'new_v', 'new_v_dn_a_log': 'new_v', 'new_v_dn_dt_bias': 'new_v', 'new_v_dn_out_norm': 'new_v', 'new_v_ab_w_out': 'new_v', 'new_v_cd_w_in': 'new_v', 'new_v_sgu_norm_g': 'new_v', 'new_v_sgu_norm_b': 'new_v', 'new_v_sgu_w': 'new_v', 'new_v_sgu_bias': 'new_v', 'new_v_sc_conv_w': 'new_v', 'new_v_cd_w_out': 'new_v', 'new_v_final_norm': 'new_v'}


def _forward(args):
    return _fwd_reference(*[args[k] for k in FWD_PARAMS])


def _output_shape():
    out = _jax.eval_shape(lambda: _forward(_fwd_setup_inputs(0)))
    return out.shape, out.dtype

N_MICROBATCH = 1
ADAM_LR = 0.001
ADAM_B1 = 0.9
ADAM_B2 = 0.999
ADAM_EPS = 1e-08
ADAM_WD = 0.01
ADAM_STEP = 10
PER_EXAMPLE_BATCH_AXIS = {'x': 0, 'loss_target': 0}
SHARED_INPUTS = []
_WEIGHT_DTYPES = {'ffn1_norm': _jnp.float32, 'ffn1_w_gate': _jnp.float32, 'ffn1_w_up': _jnp.float32, 'ffn1_w_down': _jnp.float32, 'mix_norm': _jnp.float32, 'ffn2_norm': _jnp.float32, 'ffn2_w_gate': _jnp.float32, 'ffn2_w_up': _jnp.float32, 'ffn2_w_down': _jnp.float32, 'ab_w_in': _jnp.float32, 'pool_w': _jnp.float32, 'pool_scale': _jnp.float32, 'dn_conv_w': _jnp.float32, 'dn_a_log': _jnp.float32, 'dn_dt_bias': _jnp.float32, 'dn_out_norm': _jnp.float32, 'ab_w_out': _jnp.float32, 'cd_w_in': _jnp.float32, 'sgu_norm_g': _jnp.float32, 'sgu_norm_b': _jnp.float32, 'sgu_w': _jnp.float32, 'sgu_bias': _jnp.float32, 'sc_conv_w': _jnp.float32, 'cd_w_out': _jnp.float32, 'final_norm': _jnp.float32}
MOMENT_SCALE = {'ffn1_norm': 1.237171e-01, 'ffn1_w_gate': 5.184880e-02, 'ffn1_w_up': 5.008218e-02, 'ffn1_w_down': 8.316390e-02, 'mix_norm': 2.322801e-01, 'ffn2_norm': 9.277212e-02, 'ffn2_w_gate': 3.758697e-02, 'ffn2_w_up': 3.647018e-02, 'ffn2_w_down': 6.050876e-02, 'ab_w_in': 1.453123e-01, 'pool_w': 2.207235e-01, 'pool_scale': 2.170213e-01, 'dn_conv_w': 1.111129e-01, 'dn_a_log': 7.699603e-01, 'dn_dt_bias': 7.660338e-01, 'dn_out_norm': 3.179246e-01, 'ab_w_out': 1.871271e-01, 'cd_w_in': 1.433559e-01, 'sgu_norm_g': 7.132058e-02, 'sgu_norm_b': 7.780567e-02, 'sgu_w': 7.407639e-02, 'sgu_bias': 1.085554e-01, 'sc_conv_w': 1.681377e-01, 'cd_w_out': 1.469434e-01, 'final_norm': 6.396906e+01}


def _to_microbatches(a, axis):
    t = _jnp.moveaxis(a, axis, 0)
    t = t.reshape((N_MICROBATCH, t.shape[0] // N_MICROBATCH) + t.shape[1:])
    return _jnp.moveaxis(t, 1, axis + 1)


def setup_inputs(seed: int = 0) -> dict:
    inp = _fwd_setup_inputs(seed)
    key = _jax.random.fold_in(_jax.random.key(seed), 7919)
    shape, _ = _output_shape()
    out = dict(inp)
    out["loss_target"] = _jax.random.normal(_jax.random.fold_in(key, 0), shape, _jnp.float32)
    for i, name in enumerate(TWIN_WEIGHTS):
        w = inp[name].astype(_jnp.float32)
        if MOMENT_SCALE is None:
            s = _jnp.sqrt(_jnp.mean(_jnp.square(w)) + 1e-30)
        else:
            s = MOMENT_SCALE[name]
        km, kv = _jax.random.split(_jax.random.fold_in(key, i + 1))
        out[name] = w
        out["m_" + name] = s * _jax.random.normal(km, w.shape, _jnp.float32)
        out["v_" + name] = (s * s) * _jax.random.uniform(kv, w.shape, _jnp.float32, 0.5, 1.5)
    if N_MICROBATCH > 1:
        for name, axis in PER_EXAMPLE_BATCH_AXIS.items():
            out[name] = _to_microbatches(out[name], axis)
    return {'x': out['x'], 'ffn1_norm': out['ffn1_norm'], 'ffn1_w_gate': out['ffn1_w_gate'], 'ffn1_w_up': out['ffn1_w_up'], 'ffn1_w_down': out['ffn1_w_down'], 'mix_norm': out['mix_norm'], 'ffn2_norm': out['ffn2_norm'], 'ffn2_w_gate': out['ffn2_w_gate'], 'ffn2_w_up': out['ffn2_w_up'], 'ffn2_w_down': out['ffn2_w_down'], 'ab_w_in': out['ab_w_in'], 'pool_w': out['pool_w'], 'pool_scale': out['pool_scale'], 'dn_conv_w': out['dn_conv_w'], 'dn_a_log': out['dn_a_log'], 'dn_dt_bias': out['dn_dt_bias'], 'dn_out_norm': out['dn_out_norm'], 'ab_w_out': out['ab_w_out'], 'cd_w_in': out['cd_w_in'], 'sgu_norm_g': out['sgu_norm_g'], 'sgu_norm_b': out['sgu_norm_b'], 'sgu_w': out['sgu_w'], 'sgu_bias': out['sgu_bias'], 'sc_conv_w': out['sc_conv_w'], 'cd_w_out': out['cd_w_out'], 'final_norm': out['final_norm'], 'loss_target': out['loss_target'], 'm_ffn1_norm': out['m_ffn1_norm'], 'm_ffn1_w_gate': out['m_ffn1_w_gate'], 'm_ffn1_w_up': out['m_ffn1_w_up'], 'm_ffn1_w_down': out['m_ffn1_w_down'], 'm_mix_norm': out['m_mix_norm'], 'm_ffn2_norm': out['m_ffn2_norm'], 'm_ffn2_w_gate': out['m_ffn2_w_gate'], 'm_ffn2_w_up': out['m_ffn2_w_up'], 'm_ffn2_w_down': out['m_ffn2_w_down'], 'm_ab_w_in': out['m_ab_w_in'], 'm_pool_w': out['m_pool_w'], 'm_pool_scale': out['m_pool_scale'], 'm_dn_conv_w': out['m_dn_conv_w'], 'm_dn_a_log': out['m_dn_a_log'], 'm_dn_dt_bias': out['m_dn_dt_bias'], 'm_dn_out_norm': out['m_dn_out_norm'], 'm_ab_w_out': out['m_ab_w_out'], 'm_cd_w_in': out['m_cd_w_in'], 'm_sgu_norm_g': out['m_sgu_norm_g'], 'm_sgu_norm_b': out['m_sgu_norm_b'], 'm_sgu_w': out['m_sgu_w'], 'm_sgu_bias': out['m_sgu_bias'], 'm_sc_conv_w': out['m_sc_conv_w'], 'm_cd_w_out': out['m_cd_w_out'], 'm_final_norm': out['m_final_norm'], 'v_ffn1_norm': out['v_ffn1_norm'], 'v_ffn1_w_gate': out['v_ffn1_w_gate'], 'v_ffn1_w_up': out['v_ffn1_w_up'], 'v_ffn1_w_down': out['v_ffn1_w_down'], 'v_mix_norm': out['v_mix_norm'], 'v_ffn2_norm': out['v_ffn2_norm'], 'v_ffn2_w_gate': out['v_ffn2_w_gate'], 'v_ffn2_w_up': out['v_ffn2_w_up'], 'v_ffn2_w_down': out['v_ffn2_w_down'], 'v_ab_w_in': out['v_ab_w_in'], 'v_pool_w': out['v_pool_w'], 'v_pool_scale': out['v_pool_scale'], 'v_dn_conv_w': out['v_dn_conv_w'], 'v_dn_a_log': out['v_dn_a_log'], 'v_dn_dt_bias': out['v_dn_dt_bias'], 'v_dn_out_norm': out['v_dn_out_norm'], 'v_ab_w_out': out['v_ab_w_out'], 'v_cd_w_in': out['v_cd_w_in'], 'v_sgu_norm_g': out['v_sgu_norm_g'], 'v_sgu_norm_b': out['v_sgu_norm_b'], 'v_sgu_w': out['v_sgu_w'], 'v_sgu_bias': out['v_sgu_bias'], 'v_sc_conv_w': out['v_sc_conv_w'], 'v_cd_w_out': out['v_cd_w_out'], 'v_final_norm': out['v_final_norm']}


def _loss(weights, diff, rest, loss_target):
    with _jax.named_scope("forward"):
        args = {**rest, TWIN_DIFF_INPUT: diff, **{k: w.astype(_WEIGHT_DTYPES[k]) for k, w in weights.items()}}
        y = _forward(args)
    with _jax.named_scope("loss_head"):
        err = _jnp.square(y.astype(_jnp.float32) - loss_target)
        return 0.5 * _jnp.sum(_jnp.mean(err, axis=-1)) if err.ndim else 0.5 * err


def _adamw(w, g, m, v):
    m = ADAM_B1 * m + (1.0 - ADAM_B1) * g
    v = ADAM_B2 * v + (1.0 - ADAM_B2) * _jnp.square(g)
    m_hat = m / (1.0 - ADAM_B1 ** ADAM_STEP)
    v_hat = v / (1.0 - ADAM_B2 ** ADAM_STEP)
    delta = -ADAM_LR * (m_hat / (_jnp.sqrt(v_hat) + ADAM_EPS) + ADAM_WD * w)
    return delta, m, v


def reference(x, ffn1_norm, ffn1_w_gate, ffn1_w_up, ffn1_w_down, mix_norm, ffn2_norm, ffn2_w_gate, ffn2_w_up, ffn2_w_down, ab_w_in, pool_w, pool_scale, dn_conv_w, dn_a_log, dn_dt_bias, dn_out_norm, ab_w_out, cd_w_in, sgu_norm_g, sgu_norm_b, sgu_w, sgu_bias, sc_conv_w, cd_w_out, final_norm, loss_target, m_ffn1_norm, m_ffn1_w_gate, m_ffn1_w_up, m_ffn1_w_down, m_mix_norm, m_ffn2_norm, m_ffn2_w_gate, m_ffn2_w_up, m_ffn2_w_down, m_ab_w_in, m_pool_w, m_pool_scale, m_dn_conv_w, m_dn_a_log, m_dn_dt_bias, m_dn_out_norm, m_ab_w_out, m_cd_w_in, m_sgu_norm_g, m_sgu_norm_b, m_sgu_w, m_sgu_bias, m_sc_conv_w, m_cd_w_out, m_final_norm, v_ffn1_norm, v_ffn1_w_gate, v_ffn1_w_up, v_ffn1_w_down, v_mix_norm, v_ffn2_norm, v_ffn2_w_gate, v_ffn2_w_up, v_ffn2_w_down, v_ab_w_in, v_pool_w, v_pool_scale, v_dn_conv_w, v_dn_a_log, v_dn_dt_bias, v_dn_out_norm, v_ab_w_out, v_cd_w_in, v_sgu_norm_g, v_sgu_norm_b, v_sgu_w, v_sgu_bias, v_sc_conv_w, v_cd_w_out, v_final_norm):
    given = dict(x=x, ffn1_norm=ffn1_norm, ffn1_w_gate=ffn1_w_gate, ffn1_w_up=ffn1_w_up, ffn1_w_down=ffn1_w_down, mix_norm=mix_norm, ffn2_norm=ffn2_norm, ffn2_w_gate=ffn2_w_gate, ffn2_w_up=ffn2_w_up, ffn2_w_down=ffn2_w_down, ab_w_in=ab_w_in, pool_w=pool_w, pool_scale=pool_scale, dn_conv_w=dn_conv_w, dn_a_log=dn_a_log, dn_dt_bias=dn_dt_bias, dn_out_norm=dn_out_norm, ab_w_out=ab_w_out, cd_w_in=cd_w_in, sgu_norm_g=sgu_norm_g, sgu_norm_b=sgu_norm_b, sgu_w=sgu_w, sgu_bias=sgu_bias, sc_conv_w=sc_conv_w, cd_w_out=cd_w_out, final_norm=final_norm, loss_target=loss_target, m_ffn1_norm=m_ffn1_norm, m_ffn1_w_gate=m_ffn1_w_gate, m_ffn1_w_up=m_ffn1_w_up, m_ffn1_w_down=m_ffn1_w_down, m_mix_norm=m_mix_norm, m_ffn2_norm=m_ffn2_norm, m_ffn2_w_gate=m_ffn2_w_gate, m_ffn2_w_up=m_ffn2_w_up, m_ffn2_w_down=m_ffn2_w_down, m_ab_w_in=m_ab_w_in, m_pool_w=m_pool_w, m_pool_scale=m_pool_scale, m_dn_conv_w=m_dn_conv_w, m_dn_a_log=m_dn_a_log, m_dn_dt_bias=m_dn_dt_bias, m_dn_out_norm=m_dn_out_norm, m_ab_w_out=m_ab_w_out, m_cd_w_in=m_cd_w_in, m_sgu_norm_g=m_sgu_norm_g, m_sgu_norm_b=m_sgu_norm_b, m_sgu_w=m_sgu_w, m_sgu_bias=m_sgu_bias, m_sc_conv_w=m_sc_conv_w, m_cd_w_out=m_cd_w_out, m_final_norm=m_final_norm, v_ffn1_norm=v_ffn1_norm, v_ffn1_w_gate=v_ffn1_w_gate, v_ffn1_w_up=v_ffn1_w_up, v_ffn1_w_down=v_ffn1_w_down, v_mix_norm=v_mix_norm, v_ffn2_norm=v_ffn2_norm, v_ffn2_w_gate=v_ffn2_w_gate, v_ffn2_w_up=v_ffn2_w_up, v_ffn2_w_down=v_ffn2_w_down, v_ab_w_in=v_ab_w_in, v_pool_w=v_pool_w, v_pool_scale=v_pool_scale, v_dn_conv_w=v_dn_conv_w, v_dn_a_log=v_dn_a_log, v_dn_dt_bias=v_dn_dt_bias, v_dn_out_norm=v_dn_out_norm, v_ab_w_out=v_ab_w_out, v_cd_w_in=v_cd_w_in, v_sgu_norm_g=v_sgu_norm_g, v_sgu_norm_b=v_sgu_norm_b, v_sgu_w=v_sgu_w, v_sgu_bias=v_sgu_bias, v_sc_conv_w=v_sc_conv_w, v_cd_w_out=v_cd_w_out, v_final_norm=v_final_norm)
    weights = {n: given[n] for n in TWIN_WEIGHTS}
    shared = {n: given[n] for n in SHARED_INPUTS}
    per_example = {n: given[n] for n in ['x']}
    grad_fn = _jax.value_and_grad(_loss, argnums=(0, 1))

    def one_microbatch(ex, loss_target):
        ex = dict(ex)
        diff = ex.pop(TWIN_DIFF_INPUT)
        return grad_fn(weights, diff, {**shared, **ex}, loss_target)

    if N_MICROBATCH == 1:
        loss, (grad_w, grad_x) = one_microbatch(per_example, given["loss_target"])
    else:
        def body(carry, xs):
            loss_sum, grad_sum = carry
            l_k, (gw_k, gx_k) = one_microbatch(xs[0], xs[1])
            with _jax.named_scope("update"):
                return (loss_sum + l_k, _jax.tree.map(_jnp.add, grad_sum, gw_k)), gx_k

        init = (_jnp.zeros((), _jnp.float32), _jax.tree.map(_jnp.zeros_like, weights))
        (loss, grad_w), grad_x = _jax.lax.scan(body, init, (per_example, given["loss_target"]))
    with _jax.named_scope("update"):
        delta_w, new_m, new_v = {}, {}, {}
        for n in TWIN_WEIGHTS:
            delta_w[n], new_m[n], new_v[n] = _adamw(weights[n], grad_w[n], given["m_" + n], given["v_" + n])
    return (loss, grad_x, *[grad_w[n] for n in TWIN_WEIGHTS], *[delta_w[n] for n in TWIN_WEIGHTS],
            *[new_m[n] for n in TWIN_WEIGHTS], *[new_v[n] for n in TWIN_WEIGHTS])
```

```python
import functools
import math

import jax
import jax.numpy as jnp
from jax import lax
from jax.experimental import pallas as pl
from jax.experimental.pallas import tpu as pltpu

F32, BF16 = jnp.float32, jnp.bfloat16
HIGHEST = lax.Precision.HIGHEST
MESH = pl.DeviceIdType.MESH

EPS = 1e-6
LANES = 128
CH = 64
PAIR = 2 * CH
POOL_WINDOWS = (2, 4, 8, 16)
N_CHIPS = 4
ADAM_LR, ADAM_B1, ADAM_B2, ADAM_EPS, ADAM_WD, ADAM_STEP = 0.001, 0.9, 0.999, 1e-08, 0.01, 10
VMEM_LIMIT = 56 * 1024 * 1024


def _pcall(body, **kw):
    return pl.pallas_call(body, **kw)


def _params(nd):
    return pltpu.CompilerParams(dimension_semantics=("arbitrary",) * nd, vmem_limit_bytes=VMEM_LIMIT)


def _sds(shape, dtype=F32):
    return jax.ShapeDtypeStruct(tuple(shape), dtype)


def _tile(n, cap, mult=LANES):
    if n <= cap:
        return n
    best = None
    for t in range(mult, cap + 1, mult):
        if n % t == 0:
            best = t
    assert best is not None, (n, cap)
    return best


def _dg(a, b, ca, cb):
    return lax.dot_general(a.astype(BF16), b.astype(BF16), (((ca,), (cb,)), ((), ())),
                           preferred_element_type=F32)


@jax.custom_vjp
def _bdot(a, b):
    return _dg(a, b, 1, 0)


def _bdot_fwd(a, b):
    return _dg(a, b, 1, 0), (a, b)


def _bdot_bwd(res, g):
    a, b = res
    return _dg(g, b, 1, 1), _dg(a, g, 0, 0)


_bdot.defvjp(_bdot_fwd, _bdot_bwd)


@jax.custom_vjp
def _bdot_nt(a, b):
    return _dg(a, b, 1, 1)


def _bdot_nt_fwd(a, b):
    return _dg(a, b, 1, 1), (a, b)


def _bdot_nt_bwd(res, g):
    a, b = res
    return _dg(g, b, 1, 0), _dg(g, a, 0, 0)


_bdot_nt.defvjp(_bdot_nt_fwd, _bdot_nt_bwd)


def _hdot(a, b):
    return jnp.dot(a, b, precision=HIGHEST, preferred_element_type=F32)


def _shift_raw(x, k):
    n = x.shape[0]
    rows = lax.broadcasted_iota(jnp.int32, x.shape, 0)
    r = pltpu.roll(x, k % n, 0)
    if k > 0:
        return jnp.where(rows >= k, r, 0.0)
    return jnp.where(rows < n + k, r, 0.0)


@functools.partial(jax.custom_vjp, nondiff_argnums=(1,))
def _shift(x, k):
    return _shift_raw(x, k)


def _shift_fwd(x, k):
    return _shift_raw(x, k), None


def _shift_bwd(k, _, g):
    return (_shift(g, -k),)


_shift.defvjp(_shift_fwd, _shift_bwd)


def _col(x, j):
    lanes = lax.broadcasted_iota(jnp.int32, x.shape, x.ndim - 1)
    return jnp.sum(jnp.where(lanes == j, x, 0.0), axis=-1, keepdims=True)


def _rms(x, g):
    return x * lax.rsqrt(jnp.mean(x * x, axis=-1, keepdims=True) + EPS) * g


def _sigmoid(x):
    return 1.0 / (1.0 + jnp.exp(-x))


def _silu(x):
    return x * _sigmoid(x)


def _softplus(x):
    return jnp.maximum(x, 0.0) + jnp.log(1.0 + jnp.exp(-jnp.abs(x)))


def _gelu(x):
    c = math.sqrt(2.0 / math.pi)
    return 0.5 * x * (1.0 + jnp.tanh(c * (x + 0.044715 * (x * x * x))))


def _first(axes):
    c = None
    for a in axes:
        t = pl.program_id(a) == 0
        c = t if c is None else jnp.logical_and(c, t)
    return c


def _fwd_call(name, fn, grid, ins, outs, acc_axes=()):
    nin = len(ins)

    def body(*refs):
        pids = tuple(pl.program_id(a) for a in range(len(grid)))
        first = _first(acc_axes) if acc_axes else None
        vals = [r[...].astype(F32) for r in refs[:nin]]
        res = fn(pids, *vals)
        for r, o, (_, _, acc) in zip(refs[nin:], res, outs):
            if acc:
                @pl.when(first)
                def _(r=r):
                    r[...] = jnp.zeros_like(r)
                r[...] += o.astype(r.dtype)
            else:
                r[...] = o.astype(r.dtype)

    return _pcall(body, grid=grid, in_specs=[s for _, s in ins], out_specs=[s for _, s, _ in outs],
                  out_shape=[s for s, _, _ in outs], name=name,
                  compiler_params=_params(len(grid)))(*[a for a, _ in ins])


def _bwd_call(name, fn, grid, ins, cts, gouts, acc_axes=(), addends=()):
    nin, nct, nadd = len(ins), len(cts), len(addends)

    def body(*refs):
        pids = tuple(pl.program_id(a) for a in range(len(grid)))
        first = _first(acc_axes) if acc_axes else None
        vals = [r[...].astype(F32) for r in refs[:nin]]
        ctv = tuple(r[...].astype(F32) for r in refs[nin:nin + nct])
        addv = {pos: refs[nin + nct + n][...].astype(F32) for n, (pos, _, _) in enumerate(addends)}
        _, vjp = jax.vjp(lambda *v: tuple(fn(pids, *v)), *vals)
        grads = vjp(ctv)
        for pos, (r, (idx, _, _, acc)) in enumerate(zip(refs[nin + nct + nadd:], gouts)):
            gval = grads[idx]
            if pos in addv:
                gval = gval + addv[pos]
            if acc:
                @pl.when(first)
                def _(r=r):
                    r[...] = jnp.zeros_like(r)
                r[...] += gval.astype(r.dtype)
            else:
                r[...] = gval.astype(r.dtype)

    args = [a for a, _ in ins] + [a for a, _ in cts] + [a for _, a, _ in addends]
    specs = [s for _, s in ins] + [s for _, s in cts] + [s for _, _, s in addends]
    return _pcall(body, grid=grid, in_specs=specs, out_specs=[s for _, _, s, _ in gouts],
                  out_shape=[s for _, s, _, _ in gouts], name=name,
                  compiler_params=_params(len(grid)))(*args)


def _mm(name, a, b, *, ta=False, tb=False, out_dtype=F32, res=None, scale=1.0, tm_cap=512, tn_cap=1024,
        tk_cap=1024):
    if ta:
        K, M = a.shape
    else:
        M, K = a.shape
    N = b.shape[0] if tb else b.shape[1]
    tm, tn, tk = _tile(M, tm_cap), _tile(N, tn_cap), _tile(K, tk_cap)
    nk = K // tk
    a_spec = pl.BlockSpec((tk, tm), lambda i, j, k: (k, i)) if ta else pl.BlockSpec((tm, tk), lambda i, j, k: (i, k))
    b_spec = pl.BlockSpec((tn, tk), lambda i, j, k: (j, k)) if tb else pl.BlockSpec((tk, tn), lambda i, j, k: (k, j))
    o_spec = pl.BlockSpec((tm, tn), lambda i, j, k: (i, j))
    ca, cb = (0 if ta else 1), (1 if tb else 0)
    has_res = res is not None

    def body(*refs):
        a_ref, b_ref = refs[0], refs[1]
        r_ref = refs[2] if has_res else None
        o_ref, acc = refs[-2], refs[-1]
        k = pl.program_id(2)

        @pl.when(k == 0)
        def _():
            acc[...] = jnp.zeros_like(acc)

        acc[...] += _dg(a_ref[...], b_ref[...], ca, cb)

        @pl.when(k == nk - 1)
        def _():
            val = acc[...] * scale if scale != 1.0 else acc[...]
            if has_res:
                val = r_ref[...].astype(F32) + val
            o_ref[...] = val.astype(o_ref.dtype)

    args, specs = [a, b], [a_spec, b_spec]
    if has_res:
        args.append(res)
        specs.append(o_spec)
    return _pcall(body, grid=(M // tm, N // tn, nk), in_specs=specs, out_specs=o_spec,
                  out_shape=_sds((M, N), out_dtype), scratch_shapes=[pltpu.VMEM((tm, tn), F32)], name=name,
                  compiler_params=_params(3))(*args)


def _ffn_fwd(name, h, g, wg, wu, wd, l, tm):
    T, D = h.shape
    ns, _, _, Fs = wg.shape

    def body(h_ref, g_ref, wg_ref, wu_ref, wd_ref, ho_ref, gate_ref, up_ref, hn_s, acc_s):
        s = pl.program_id(1)

        @pl.when(s == 0)
        def _():
            hn_s[...] = _rms(h_ref[...], g_ref[...]).astype(BF16)
            acc_s[...] = jnp.zeros_like(acc_s)

        hn = hn_s[...]
        gate = jnp.dot(hn, wg_ref[...], preferred_element_type=F32)
        up = jnp.dot(hn, wu_ref[...], preferred_element_type=F32)
        gate_ref[...] = gate.astype(BF16)
        up_ref[...] = up.astype(BF16)
        act = (_silu(gate) * up).astype(BF16)
        acc_s[...] += jnp.dot(act, wd_ref[...], preferred_element_type=F32)

        @pl.when(s == ns - 1)
        def _():
            ho_ref[...] = h_ref[...] + 0.5 * acc_s[...]

    row = pl.BlockSpec((tm, D), lambda i, s: (i, 0))
    wcol = pl.BlockSpec((None, None, D, Fs), lambda i, s: (s, l, 0, 0))
    wrow = pl.BlockSpec((None, None, Fs, D), lambda i, s: (s, l, 0, 0))
    slot = pl.BlockSpec((None, tm, Fs), lambda i, s: (s, i, 0))
    return _pcall(body, grid=(T // tm, ns), in_specs=[row, pl.BlockSpec((1, D), lambda i, s: (0, 0)), wcol, wcol, wrow],
                  out_specs=[row, slot, slot],
                  out_shape=[_sds((T, D)), _sds((ns, T, Fs), BF16), _sds((ns, T, Fs), BF16)],
                  scratch_shapes=[pltpu.VMEM((tm, D), BF16), pltpu.VMEM((tm, D), F32)], name=name,
                  compiler_params=_params(2))(h, g, wg, wu, wd)


def _ffn_bwd_x(name, dh, h, g, gate, up, wg, wu, wd, l, tm):
    T, D = h.shape
    ns, _, _, Fs = wg.shape

    def body(dh_ref, h_ref, g_ref, gate_ref, up_ref, wg_ref, wu_ref, wd_ref,
             dho_ref, dgate_ref, dup_ref, dgain_ref, do_s, acc_s):
        i, s = pl.program_id(0), pl.program_id(1)

        @pl.when(s == 0)
        def _():
            do_s[...] = (0.5 * dh_ref[...]).astype(BF16)
            acc_s[...] = jnp.zeros_like(acc_s)

        @pl.when(jnp.logical_and(i == 0, s == 0))
        def _():
            dgain_ref[...] = jnp.zeros_like(dgain_ref)

        dact = _dg(do_s[...], wd_ref[...], 1, 1)
        gt, u = gate_ref[...].astype(F32), up_ref[...].astype(F32)
        sg = _sigmoid(gt)
        dgt = (dact * u * (sg * (1.0 + gt * (1.0 - sg)))).astype(BF16)
        du = (dact * (gt * sg)).astype(BF16)
        dgate_ref[...] = dgt
        dup_ref[...] = du
        acc_s[...] += _dg(dgt, wg_ref[...], 1, 1) + _dg(du, wu_ref[...], 1, 1)

        @pl.when(s == ns - 1)
        def _():
            _, vjp = jax.vjp(_rms, h_ref[...], g_ref[...])
            dx, dgain = vjp(acc_s[...])
            dho_ref[...] = dh_ref[...] + dx
            dgain_ref[...] += dgain

    row = pl.BlockSpec((tm, D), lambda i, s: (i, 0))
    gain = pl.BlockSpec((1, D), lambda i, s: (0, 0))
    wcol = pl.BlockSpec((None, None, D, Fs), lambda i, s: (s, l, 0, 0))
    wrow = pl.BlockSpec((None, None, Fs, D), lambda i, s: (s, l, 0, 0))
    slot = pl.BlockSpec((None, tm, Fs), lambda i, s: (s, i, 0))
    return _pcall(body, grid=(T // tm, ns), in_specs=[row, row, gain, slot, slot, wcol, wcol, wrow],
                  out_specs=[row, slot, slot, gain],
                  out_shape=[_sds((T, D)), _sds((ns, T, Fs), BF16), _sds((ns, T, Fs), BF16), _sds((1, D))],
                  scratch_shapes=[pltpu.VMEM((tm, D), BF16), pltpu.VMEM((tm, D), F32)], name=name,
                  compiler_params=_params(2))(dh, h, g, gate, up, wg, wu, wd)


def _ffn_bwd_w(name, dh, h, g, gate, up, dgate, dup, tm):
    T, D = h.shape
    ns, _, Fs = gate.shape

    def body(dh_ref, h_ref, g_ref, gate_ref, up_ref, dgate_ref, dup_ref, dwg_ref, dwu_ref, dwd_ref):
        @pl.when(pl.program_id(1) == 0)
        def _():
            dwg_ref[...] = jnp.zeros_like(dwg_ref)
            dwu_ref[...] = jnp.zeros_like(dwu_ref)
            dwd_ref[...] = jnp.zeros_like(dwd_ref)

        hn = _rms(h_ref[...], g_ref[...]).astype(BF16)
        act = (_silu(gate_ref[...].astype(F32)) * up_ref[...].astype(F32)).astype(BF16)
        do = (0.5 * dh_ref[...]).astype(BF16)
        dwg_ref[...] += _dg(hn, dgate_ref[...], 0, 0)
        dwu_ref[...] += _dg(hn, dup_ref[...], 0, 0)
        dwd_ref[...] += _dg(act, do, 0, 0)

    row = pl.BlockSpec((tm, D), lambda s, j: (j, 0))
    gain = pl.BlockSpec((1, D), lambda s, j: (0, 0))
    slot = pl.BlockSpec((None, tm, Fs), lambda s, j: (s, j, 0))
    wcol = pl.BlockSpec((None, D, Fs), lambda s, j: (s, 0, 0))
    wrow = pl.BlockSpec((None, Fs, D), lambda s, j: (s, 0, 0))
    return _pcall(body, grid=(ns, T // tm), in_specs=[row, row, gain, slot, slot, slot, slot],
                  out_specs=[wcol, wcol, wrow],
                  out_shape=[_sds((ns, D, Fs)), _sds((ns, D, Fs)), _sds((ns, Fs, D))], name=name,
                  compiler_params=_params(2))(dh, h, g, gate, up, dgate, dup)


def _rms_fn(pids, x, g):
    return (_rms(x, g),)


def _pool_fn(pids, a, w, scale):
    rows = lax.broadcasted_iota(jnp.int32, (a.shape[0], LANES), 0)
    outs = []
    for gi, win in enumerate(POOL_WINDOWS):
        ag = a[:, gi * LANES:(gi + 1) * LANES]
        s, k = ag, 1
        while k < win:
            s = s + _shift(s, k)
            k *= 2
        cnt = jnp.minimum(rows + 1, win).astype(F32)
        pooled = s / cnt - ag
        outs.append(_bdot(pooled, w[gi]) * scale[:, gi * LANES:(gi + 1) * LANES])
    return (jnp.concatenate(outs, axis=-1),)


def _conv_taps(x, cw):
    K = cw.shape[0]
    y = cw[K - 1] * x
    for j in range(K - 1):
        y = y + cw[j] * _shift(x, K - 1 - j)
    return y


def _prep_fn(pids, q, k, v, bg, cwq, cwk, cwv, alog, dtb):
    hd = pids[0]
    q, k, v = _silu(_conv_taps(q, cwq)), _silu(_conv_taps(k, cwk)), _silu(_conv_taps(v, cwv))
    q = q * lax.rsqrt(jnp.sum(q * q, axis=-1, keepdims=True) + EPS) * (LANES ** -0.5)
    k = k * lax.rsqrt(jnp.sum(k * k, axis=-1, keepdims=True) + EPS)
    beta = _sigmoid(_col(bg, hd))
    g = -jnp.exp(_col(alog, hd)) * _softplus(_col(bg, hd + 4) + _col(dtb, hd))
    return q, k, v, jnp.broadcast_to(beta, q.shape), jnp.broadcast_to(g, q.shape)


def _post_fn(pids, o, z, onorm):
    return (_rms(o, onorm) * _silu(z),)


def _sgu_fn(pids, u, v, g, b, ws, bias_t):
    u, v = _gelu(u), _gelu(v)
    mu = jnp.mean(v, axis=-1, keepdims=True)
    xc = v - mu
    vn = xc * lax.rsqrt(jnp.mean(xc * xc, axis=-1, keepdims=True) + EPS) * g + b
    r = lax.broadcasted_iota(jnp.int32, (LANES, LANES), 0)
    c = lax.broadcasted_iota(jnp.int32, (LANES, LANES), 1)
    rows = []
    for n in range(u.shape[0] // LANES):
        heads = []
        for hd in range(4):
            wm = jnp.where(r >= c, ws[hd], 0.0)
            blk = vn[n * LANES:(n + 1) * LANES, hd * LANES:(hd + 1) * LANES]
            heads.append(_bdot(wm, blk) + _col(bias_t, hd))
        rows.append(jnp.concatenate(heads, axis=-1))
    mixed = jnp.concatenate(rows, axis=0) if len(rows) > 1 else rows[0]
    return (u * mixed,)


def _sconv_fn(pids, xd, bgate, cg, cw):
    return (bgate * _conv_taps(cg * xd, cw),)


def _loss_fn(pids, h, g, tgt):
    err = _rms(h, g) - tgt
    tot = 0.5 * jnp.sum(jnp.mean(err * err, axis=-1, keepdims=True), axis=0, keepdims=True)
    return (jnp.broadcast_to(tot, (1, LANES)),)


def _stage_a(q, k, v, beta, g):
    n = q.shape[0]
    r = lax.broadcasted_iota(jnp.int32, (n, n), 0)
    c = lax.broadcasted_iota(jnp.int32, (n, n), 1)
    same = jnp.right_shift(r, 6) == jnp.right_shift(c, 6)
    tri = jnp.logical_and(same, r >= c)
    strict = jnp.logical_and(same, r > c)
    gc = _hdot(tri.astype(F32), g)
    gt = _hdot(same.astype(F32), g)
    gamma = jnp.exp(jnp.where(tri, gc - gc.T, -jnp.inf))
    kb = k * beta
    lmat = jnp.where(strict, _bdot_nt(kb, k) * gamma, 0.0)
    p = -lmat
    tinv = (r == c).astype(F32) + p
    for _ in range(5):
        p = _hdot(p, p)
        tinv = tinv + _hdot(tinv, p)
    egc = jnp.exp(gc)
    u = _hdot(tinv, v * beta)
    w = _hdot(tinv, kb * egc)
    a = _bdot_nt(q, k) * gamma
    return u, w, q * egc, a, k * jnp.exp(gt - gc), jnp.exp(gt)


def _scan_pair_fwd(s0, u, w, qd, a, kd, l0, l1):
    vn0 = u[:CH] - _dg(w[:CH], s0, 1, 0)
    s1 = s0 * l0 + _dg(kd[:CH], vn0, 0, 0)
    vn1 = u[CH:] - _dg(w[CH:], s1, 1, 0)
    s2 = s1 * l1 + _dg(kd[CH:], vn1, 0, 0)
    o = (jnp.concatenate([_dg(qd[:CH], s0, 1, 0), _dg(qd[CH:], s1, 1, 0)], axis=0)
         + _dg(a, jnp.concatenate([vn0, vn1], axis=0), 1, 0))
    return o, s1, s2


def _scan_chunk_bwd(s, ds_next, do, dvn_o, u, w, qd, kd, lrow):
    vn = u - _dg(w, s, 1, 0)
    dvn = dvn_o + _dg(kd, ds_next, 1, 0)
    dkd = _dg(vn, ds_next, 1, 1)
    dl = jnp.sum(ds_next * s, axis=0, keepdims=True)
    dw = -_dg(dvn, s, 1, 1)
    dqd = _dg(do, s, 1, 1)
    ds = _dg(qd, do, 0, 0) + lrow * ds_next - _dg(w, dvn, 0, 0)
    return ds, vn, dvn, dw, dqd, dkd, dl


def _delta_fwd(name, qn, kn, vn, beta, g, B, S):
    T = qn.shape[0]
    npair = S // PAIR

    def body(q_ref, k_ref, v_ref, b_ref, g_ref, o_ref, st_ref, l_s):
        def step(p, s0):
            rows = pl.ds(pl.multiple_of(p * PAIR, PAIR), PAIR)
            u, w, qd, a, kd, l = _stage_a(q_ref[rows, :], k_ref[rows, :], v_ref[rows, :], b_ref[rows, :], g_ref[rows, :])
            l_s[...] = l
            o, s1, s2 = _scan_pair_fwd(s0, u, w, qd, a, kd, l_s[0:1, :], l_s[CH:CH + 1, :])
            o_ref[rows, :] = o
            st_ref[2 * p] = s0
            st_ref[2 * p + 1] = s1
            return s2

        lax.fori_loop(0, npair, step, jnp.zeros((LANES, LANES), F32))

    blk = pl.BlockSpec((S, LANES), lambda b, h: (b, h))
    st = pl.BlockSpec((None, 2 * npair, LANES, LANES), lambda b, h: (b * 4 + h, 0, 0, 0))
    return _pcall(body, grid=(B, 4), in_specs=[blk] * 5, out_specs=[blk, st],
                  out_shape=[_sds((T, 4 * LANES)), _sds((B * 4, 2 * npair, LANES, LANES))],
                  scratch_shapes=[pltpu.VMEM((PAIR, LANES), F32)], name=name,
                  compiler_params=_params(2))(qn, kn, vn, beta, g)


def _delta_bwd(name, qn, kn, vn, beta, g, states, do, B, S):
    T = qn.shape[0]
    npair = S // PAIR

    def body(q_ref, k_ref, v_ref, b_ref, g_ref, st_ref, do_ref, dq_ref, dk_ref, dv_ref, db_ref, dg_ref, l_s):
        rowid = lax.broadcasted_iota(jnp.int32, (PAIR, LANES), 0)

        def step(t, ds2):
            p = npair - 1 - t
            rows = pl.ds(pl.multiple_of(p * PAIR, PAIR), PAIR)
            ins = (q_ref[rows, :], k_ref[rows, :], v_ref[rows, :], b_ref[rows, :], g_ref[rows, :])
            (u, w, qd, a, kd, l), vjp = jax.vjp(_stage_a, *ins)
            l_s[...] = l
            l0, l1 = l_s[0:1, :], l_s[CH:CH + 1, :]
            s0, s1 = st_ref[2 * p], st_ref[2 * p + 1]
            dout = do_ref[rows, :]
            dvn_o = _dg(a, dout, 0, 0)
            ds1, vn1, dvn1, dw1, dqd1, dkd1, dl1 = _scan_chunk_bwd(
                s1, ds2, dout[CH:], dvn_o[CH:], u[CH:], w[CH:], qd[CH:], kd[CH:], l1)
            ds0, vn0, dvn0, dw0, dqd0, dkd0, dl0 = _scan_chunk_bwd(
                s0, ds1, dout[:CH], dvn_o[:CH], u[:CH], w[:CH], qd[:CH], kd[:CH], l0)
            da = _dg(dout, jnp.concatenate([vn0, vn1], axis=0), 1, 1)
            cat = lambda x0, x1: jnp.concatenate([x0, x1], axis=0)
            dl = jnp.where(rowid == 0, dl0, jnp.where(rowid == CH, dl1, 0.0))
            dq, dk, dv, db, dg = vjp((cat(dvn0, dvn1), cat(dw0, dw1), cat(dqd0, dqd1), da, cat(dkd0, dkd1), dl))
            dq_ref[rows, :] = dq
            dk_ref[rows, :] = dk
            dv_ref[rows, :] = dv
            db_ref[rows, :] = db
            dg_ref[rows, :] = dg
            return ds0

        lax.fori_loop(0, npair, step, jnp.zeros((LANES, LANES), F32))

    blk = pl.BlockSpec((S, LANES), lambda b, h: (b, h))
    st = pl.BlockSpec((None, 2 * npair, LANES, LANES), lambda b, h: (b * 4 + h, 0, 0, 0))
    return _pcall(body, grid=(B, 4), in_specs=[blk] * 5 + [st, blk], out_specs=[blk] * 5,
                  out_shape=[_sds((T, 4 * LANES))] * 5, scratch_shapes=[pltpu.VMEM((PAIR, LANES), F32)], name=name,
                  compiler_params=_params(2))(qn, kn, vn, beta, g, states, do)


ROW_TILE = 512
FFN_TILE = 512
SGU_ROWS = 256


def _rows(r, c, off=0):
    return pl.BlockSpec((r, c), lambda i: (i, off))


def _whole(shape, nd):
    zeros = (0,) * len(shape)
    if nd == 1:
        return pl.BlockSpec(shape, lambda i: zeros)
    return pl.BlockSpec(shape, lambda i, j: zeros)


def _norm_fwd(name, h, g):
    T, D = h.shape
    tr = _tile(T, ROW_TILE, 8)
    return _fwd_call(name, _rms_fn, (T // tr,), [(h, _rows(tr, D)), (g, _whole((1, D), 1))],
                     [(_sds((T, D), BF16), _rows(tr, D), False)])[0]


def _norm_bwd(name, h, g, dhn, dh_res):
    T, D = h.shape
    tr = _tile(T, ROW_TILE, 8)
    return _bwd_call(name, _rms_fn, (T // tr,), [(h, _rows(tr, D)), (g, _whole((1, D), 1))], [(dhn, _rows(tr, D))],
                     [(0, _sds((T, D)), _rows(tr, D), False), (1, _sds((1, D)), _whole((1, D), 1), True)],
                     acc_axes=(0,), addends=[(0, dh_res, _rows(tr, D))])


def _ab_specs(B, S, P):
    W4 = 4 * LANES
    seq4 = pl.BlockSpec((S, W4), lambda b: (b, 0))
    pool_ins = lambda proj: [(proj, seq4), (P["pool_w"], _whole((4, LANES, LANES), 1)), (P["pool_scale"], _whole((1, W4), 1))]
    hb = lambda off: pl.BlockSpec((S, LANES), lambda b, h: (b, off + h))
    cw = lambda off: pl.BlockSpec((4, 1, LANES), lambda b, h: (0, 0, off + h))
    small = _whole((1, LANES), 2)
    prep_ins = lambda proj: [(proj, hb(4)), (proj, hb(8)), (proj, hb(12)),
                             (proj, pl.BlockSpec((S, LANES), lambda b, h: (b, 20))),
                             (P["conv_w"], cw(0)), (P["conv_w"], cw(4)), (P["conv_w"], cw(8)),
                             (P["a_log"], small), (P["dt_bias"], small)]
    post_ins = lambda o, proj: [(o, hb(0)), (proj, hb(16)), (P["out_norm"], small)]
    return seq4, pool_ins, hb, prep_ins, post_ins, small


def _ab_fwd(tag, h, P, B, S):
    T, D = h.shape
    W4 = 4 * LANES
    seq4, pool_ins, hb, prep_ins, post_ins, small = _ab_specs(B, S, P)
    hn = _norm_fwd(tag + "_norm", h, P["mix_norm"])
    proj = _mm(tag + "_in", hn, P["w_in"])
    ya = _fwd_call(tag + "_pool", _pool_fn, (B,), pool_ins(proj), [(_sds((T, W4), BF16), seq4, False)])[0]
    qn, kn, vn, beta, g = _fwd_call(tag + "_prep", lambda pids, *v: _prep_fn((pids[1],), *v), (B, 4), prep_ins(proj),
                                    [(_sds((T, W4)), hb(0), False)] * 5)
    o, states = _delta_fwd(tag + "_delta", qn, kn, vn, beta, g, B, S)
    yb = _fwd_call(tag + "_post", _post_fn, (B, 4), post_ins(o, proj), [(_sds((T, W4), BF16), hb(0), False)])[0]
    y = jnp.concatenate([ya, yb], axis=-1)
    h_new = _mm(tag + "_out", y, P["w_out"], res=h)
    return h_new, (h, hn, proj, qn, kn, vn, beta, g, states, o, y)


def _ab_bwd(tag, dh, saved, P, B, S):
    h, hn, proj, qn, kn, vn, beta, g, states, o, y = saved
    T, D = h.shape
    W4 = 4 * LANES
    seq4, pool_ins, hb, prep_ins, post_ins, small = _ab_specs(B, S, P)
    dy = _mm(tag + "_out_dx", dh, P["w_out"], tb=True)
    dw_out = _mm(tag + "_out_dw", y, dh, ta=True)
    do, dz, d_onorm = _bwd_call(tag + "_post_b", _post_fn, (B, 4), post_ins(o, proj), [(dy, hb(4))],
                                [(0, _sds((T, W4)), hb(0), False), (1, _sds((T, W4)), hb(0), False),
                                 (2, _sds((1, LANES)), small, True)], acc_axes=(0, 1))
    dqn, dkn, dvn, dbeta, dg = _delta_bwd(tag + "_delta_b", qn, kn, vn, beta, g, states, do, B, S)
    per_b = pl.BlockSpec((None, 1, LANES), lambda b, h: (b, 0, 0))
    dcw = pl.BlockSpec((None, 4, 1, LANES), lambda b, h: (b, 0, 0, h))
    dq, dk, dv, dbg, dcq, dck, dcv, dalog, ddt = _bwd_call(
        tag + "_prep_b", lambda pids, *v: _prep_fn((pids[1],), *v), (B, 4), prep_ins(proj),
        [(dqn, hb(0)), (dkn, hb(0)), (dvn, hb(0)), (dbeta, hb(0)), (dg, hb(0))],
        [(0, _sds((T, W4)), hb(0), False), (1, _sds((T, W4)), hb(0), False), (2, _sds((T, W4)), hb(0), False),
         (3, _sds((T, LANES)), pl.BlockSpec((S, LANES), lambda b, h: (b, 0)), True),
         (4, _sds((B, 4, 1, W4)), dcw, False), (5, _sds((B, 4, 1, W4)), dcw, False),
         (6, _sds((B, 4, 1, W4)), dcw, False),
         (7, _sds((B, 1, LANES)), per_b, True), (8, _sds((B, 1, LANES)), per_b, True)], acc_axes=(1,))
    d_conv = jnp.concatenate([jnp.sum(d, axis=0)[:, 0, :] for d in (dcq, dck, dcv)], axis=-1)
    da, dpool_w, dpool_scale = _bwd_call(
        tag + "_pool_b", _pool_fn, (B,), pool_ins(proj), [(dy, seq4)],
        [(0, _sds((T, W4)), seq4, False), (1, _sds((4, LANES, LANES)), _whole((4, LANES, LANES), 1), True),
         (2, _sds((1, W4)), _whole((1, W4), 1), True)], acc_axes=(0,))
    dproj = jnp.concatenate([da, dq, dk, dv, dz, dbg], axis=-1)
    dhn = _mm(tag + "_in_dx", dproj, P["w_in"], tb=True)
    dw_in = _mm(tag + "_in_dw", hn, dproj, ta=True)
    dh_new, d_mix = _norm_bwd(tag + "_norm_b", h, P["mix_norm"], dhn, dh)
    grads = dict(w_in=dw_in, w_out=dw_out, mix_norm=d_mix[0], pool_w=dpool_w, pool_scale=dpool_scale[0],
                 conv_w=d_conv, a_log=jnp.sum(dalog, axis=0)[0, :4], dt_bias=jnp.sum(ddt, axis=0)[0, :4],
                 out_norm=d_onorm[0])
    return dh_new, grads


def _cd_specs(B, S, T, P):
    W4 = 4 * LANES
    R = _tile(T, SGU_ROWS, LANES)
    sgu_ins = lambda proj: [(proj, _rows(R, W4, 0)), (proj, _rows(R, W4, 1)), (P["sgu_g"], _whole((1, W4), 1)),
                            (P["sgu_b"], _whole((1, W4), 1)), (P["sgu_w"], _whole((4, LANES, LANES), 1)),
                            (P["bias_t"], _whole((LANES, LANES), 1))]
    jb = lambda off: pl.BlockSpec((S, LANES), lambda j, b: (b, off + j))
    sc_ins = lambda proj: [(proj, jb(8)), (proj, jb(12)), (proj, jb(16)),
                           (P["sc_w"], pl.BlockSpec((3, 1, LANES), lambda j, b: (0, 0, j)))]
    return R, sgu_ins, jb, sc_ins


def _cd_fwd(tag, h, P, B, S):
    T, D = h.shape
    W4 = 4 * LANES
    R, sgu_ins, jb, sc_ins = _cd_specs(B, S, T, P)
    hn = _norm_fwd(tag + "_norm", h, P["mix_norm"])
    proj = _mm(tag + "_in", hn, P["w_in"])
    yc = _fwd_call(tag + "_sgu", _sgu_fn, (T // R,), sgu_ins(proj), [(_sds((T, W4), BF16), _rows(R, W4), False)])[0]
    yd = _fwd_call(tag + "_sconv", _sconv_fn, (4, B), sc_ins(proj), [(_sds((T, W4), BF16), jb(0), False)])[0]
    y = jnp.concatenate([yc, yd], axis=-1)
    h_new = _mm(tag + "_out", y, P["w_out"], res=h)
    return h_new, (h, hn, proj, y)


def _cd_bwd(tag, dh, saved, P, B, S):
    h, hn, proj, y = saved
    T, D = h.shape
    W4 = 4 * LANES
    R, sgu_ins, jb, sc_ins = _cd_specs(B, S, T, P)
    dy = _mm(tag + "_out_dx", dh, P["w_out"], tb=True)
    dw_out = _mm(tag + "_out_dw", y, dh, ta=True)
    du, dv, dsg, dsb, dsw, dbias_t = _bwd_call(
        tag + "_sgu_b", _sgu_fn, (T // R,), sgu_ins(proj), [(dy, _rows(R, W4, 0))],
        [(0, _sds((T, W4)), _rows(R, W4), False), (1, _sds((T, W4)), _rows(R, W4), False),
         (2, _sds((1, W4)), _whole((1, W4), 1), True), (3, _sds((1, W4)), _whole((1, W4), 1), True),
         (4, _sds((4, LANES, LANES)), _whole((4, LANES, LANES), 1), True),
         (5, _sds((LANES, LANES)), _whole((LANES, LANES), 1), True)], acc_axes=(0,))
    dxd, dbgate, dcg, dsc = _bwd_call(
        tag + "_sconv_b", _sconv_fn, (4, B), sc_ins(proj), [(dy, jb(4))],
        [(0, _sds((T, W4)), jb(0), False), (1, _sds((T, W4)), jb(0), False), (2, _sds((T, W4)), jb(0), False),
         (3, _sds((3, 1, W4)), pl.BlockSpec((3, 1, LANES), lambda j, b: (0, 0, j)), True)], acc_axes=(1,))
    dproj = jnp.concatenate([du, dv, dxd, dbgate, dcg], axis=-1)
    dhn = _mm(tag + "_in_dx", dproj, P["w_in"], tb=True)
    dw_in = _mm(tag + "_in_dw", hn, dproj, ta=True)
    dh_new, d_mix = _norm_bwd(tag + "_norm_b", h, P["mix_norm"], dhn, dh)
    grads = dict(w_in=dw_in, w_out=dw_out, mix_norm=d_mix[0], sgu_g=dsg[0], sgu_b=dsb[0], sgu_w=dsw,
                 sgu_bias=dbias_t[:, :4].T, sc_w=dsc[:, 0, :])
    return dh_new, grads


def _loss_fwd_bwd(h, g, tgt):
    T, D = h.shape
    tr = _tile(T, ROW_TILE, 8)
    ins = [(h, _rows(tr, D)), (g, _whole((1, D), 1)), (tgt, _rows(tr, D))]
    vec = _whole((1, LANES), 1)
    loss = _fwd_call("loss", _loss_fn, (T // tr,), ins, [(_sds((1, LANES)), vec, True)], acc_axes=(0,))[0]
    one = jnp.zeros((1, LANES), F32).at[0, 0].set(1.0)
    dh, dg = _bwd_call("loss_b", _loss_fn, (T // tr,), ins, [(one, vec)],
                       [(0, _sds((T, D)), _rows(tr, D), False), (1, _sds((1, D)), _whole((1, D), 1), True)],
                       acc_axes=(0,))
    return loss[0, 0], dh, dg[0]


def _local_step(x2, tgt2, W, B, S):
    L = W["ffn1_norm"].shape[0]
    tm = _tile(x2.shape[0], FFN_TILE, 8)
    h = x2
    saved = []
    for l in range(L):
        e = l // 2
        f1 = (W["ffn1_norm"][l][None], W["ffn1_g"], W["ffn1_u"], W["ffn1_d"])
        f2 = (W["ffn2_norm"][l][None], W["ffn2_g"], W["ffn2_u"], W["ffn2_d"])
        h0 = h
        h, gate1, up1 = _ffn_fwd(f"l{l}_ffn1", h0, *f1, l, tm)
        if l % 2 == 0:
            P = dict(mix_norm=W["mix_norm"][l][None], w_in=W["ab_in"][e], w_out=W["ab_out"][e], pool_w=W["pool_w"][e],
                     pool_scale=W["pool_scale"][e][None], conv_w=W["dn_conv_w"][e][:, None, :],
                     a_log=jnp.pad(W["dn_a_log"][e][None], ((0, 0), (0, LANES - 4))),
                     dt_bias=jnp.pad(W["dn_dt_bias"][e][None], ((0, 0), (0, LANES - 4))),
                     out_norm=W["dn_out_norm"][e][None])
            h1 = h
            h, msave = _ab_fwd(f"l{l}_ab", h1, P, B, S)
        else:
            P = dict(mix_norm=W["mix_norm"][l][None], w_in=W["cd_in"][e], w_out=W["cd_out"][e],
                     sgu_g=W["sgu_norm_g"][e][None], sgu_b=W["sgu_norm_b"][e][None], sgu_w=W["sgu_w"][e],
                     bias_t=jnp.pad(W["sgu_bias"][e].T, ((0, 0), (0, LANES - 4))),
                     sc_w=W["sc_conv_w"][e][:, None, :])
            h1 = h
            h, msave = _cd_fwd(f"l{l}_cd", h1, P, B, S)
        h2 = h
        h, gate2, up2 = _ffn_fwd(f"l{l}_ffn2", h2, *f2, l, tm)
        saved.append((f1, f2, P, h0, gate1, up1, msave, h2, gate2, up2))

    loss, dh, d_final = _loss_fwd_bwd(h, W["final_norm"][None], tgt2)

    G = {k: [None] * L for k in ("ffn1_norm", "ffn1_g", "ffn1_u", "ffn1_d", "ffn2_norm", "ffn2_g", "ffn2_u", "ffn2_d",
                                  "mix_norm")}
    GA, GC = [None] * ((L + 1) // 2), [None] * (L // 2)
    for l in reversed(range(L)):
        f1, f2, P, h0, gate1, up1, msave, h2, gate2, up2 = saved[l]
        dh_in = dh
        dh, dgate, dup, dn2 = _ffn_bwd_x(f"l{l}_ffn2_bx", dh_in, h2, f2[0], gate2, up2, f2[1], f2[2], f2[3], l, tm)
        G["ffn2_g"][l], G["ffn2_u"][l], G["ffn2_d"][l] = _ffn_bwd_w(f"l{l}_ffn2_bw", dh_in, h2, f2[0], gate2, up2,
                                                                     dgate, dup, tm)
        G["ffn2_norm"][l] = dn2[0]
        if l % 2 == 0:
            dh, mg = _ab_bwd(f"l{l}_ab", dh, msave, P, B, S)
            GA[l // 2] = mg
        else:
            dh, mg = _cd_bwd(f"l{l}_cd", dh, msave, P, B, S)
            GC[l // 2] = mg
        G["mix_norm"][l] = mg["mix_norm"]
        dh_in = dh
        dh, dgate, dup, dn1 = _ffn_bwd_x(f"l{l}_ffn1_bx", dh_in, h0, f1[0], gate1, up1, f1[1], f1[2], f1[3], l, tm)
        G["ffn1_g"][l], G["ffn1_u"][l], G["ffn1_d"][l] = _ffn_bwd_w(f"l{l}_ffn1_bw", dh_in, h0, f1[0], gate1, up1,
                                                                     dgate, dup, tm)
        G["ffn1_norm"][l] = dn1[0]

    st = lambda xs: jnp.stack(xs, axis=0)
    big = dict(ffn1_w_gate=st(G["ffn1_g"]), ffn1_w_up=st(G["ffn1_u"]), ffn1_w_down=st(G["ffn1_d"]),
               ffn2_w_gate=st(G["ffn2_g"]), ffn2_w_up=st(G["ffn2_u"]), ffn2_w_down=st(G["ffn2_d"]),
               ab_w_in=st([m["w_in"] for m in GA]), ab_w_out=st([m["w_out"] for m in GA]),
               cd_w_in=st([m["w_in"] for m in GC]), cd_w_out=st([m["w_out"] for m in GC]))
    small = dict(ffn1_norm=st(G["ffn1_norm"]), mix_norm=st(G["mix_norm"]), ffn2_norm=st(G["ffn2_norm"]),
                 pool_w=st([m["pool_w"] for m in GA]), pool_scale=st([m["pool_scale"] for m in GA]),
                 dn_conv_w=st([m["conv_w"] for m in GA]), dn_a_log=st([m["a_log"] for m in GA]),
                 dn_dt_bias=st([m["dt_bias"] for m in GA]), dn_out_norm=st([m["out_norm"] for m in GA]),
                 sgu_norm_g=st([m["sgu_g"] for m in GC]), sgu_norm_b=st([m["sgu_b"] for m in GC]),
                 sgu_w=st([m["sgu_w"] for m in GC]), sgu_bias=st([m["sgu_bias"] for m in GC]),
                 sc_conv_w=st([m["sc_w"] for m in GC]), final_norm=d_final)
    return loss, dh, big, small


ANY = pl.BlockSpec(memory_space=pl.ANY)


def _place():
    return lax.axis_index("x"), lax.axis_index("y"), lax.axis_index("c")


def _exchange(name, srcs, out_shapes, plan):
    n, m = len(srcs), len(out_shapes)

    def body(*refs):
        ins, outs = refs[:n], refs[n:n + m]
        send, recv, loc = refs[n + m:]
        local, remote = plan(_place(), ins, outs)
        lcs = [pltpu.make_async_copy(s, d, loc.at[k]) for k, (s, d) in enumerate(local)]
        for cp in lcs:
            cp.start()
        rcs = []
        for k, (s, d, peer, _) in enumerate(remote):
            cp = pltpu.make_async_remote_copy(src_ref=s, dst_ref=d, send_sem=send.at[k], recv_sem=recv.at[k],
                                              device_id=peer, device_id_type=MESH)
            cp.start()
            rcs.append(cp)
        for k, (s, _, peer, land) in enumerate(remote):
            pltpu.make_async_remote_copy(src_ref=s, dst_ref=land, send_sem=send.at[k], recv_sem=recv.at[k],
                                         device_id=peer, device_id_type=MESH).wait_recv()
        for cp in rcs:
            cp.wait_send()
        for cp in lcs:
            cp.wait()

    nl, nr = plan((0, 0, 0), [None] * n, [None] * m, count_only=True)
    return _pcall(body, in_specs=[ANY] * n, out_specs=[ANY] * m, out_shape=out_shapes,
                  scratch_shapes=[pltpu.SemaphoreType.DMA((nr,)), pltpu.SemaphoreType.DMA((nr,)),
                                  pltpu.SemaphoreType.DMA((max(nl, 1),))], name=name)(*srcs)


def _other_chips(x, y):
    return [(1 - x, y), (x, 1 - y), (1 - x, 1 - y)]


def _gather_chips(name, xs):
    n = len(xs)

    def plan(place, ins, outs, count_only=False):
        if count_only:
            return n, 3 * n
        x, y, c = place
        me = 2 * x + y
        local = [(ins[t], outs[t].at[me]) for t in range(n)]
        remote = [(ins[t], outs[t].at[me], (px, py, c), outs[t].at[2 * px + py])
                  for (px, py) in _other_chips(x, y) for t in range(n)]
        return local, remote

    return _exchange(name, xs, [_sds((N_CHIPS,) + a.shape, a.dtype) for a in xs], plan)


def _pair_split(name, gs):
    n = len(gs)

    def plan(place, ins, outs, count_only=False):
        if count_only:
            return n, n
        x, y, c = place
        local = [(ins[t].at[:, :, c], outs[t]) for t in range(n)]
        remote = [(ins[t].at[:, :, 1 - c], outs[n + t], (x, y, 1 - c), outs[n + t]) for t in range(n)]
        return local, remote

    half = [_sds(a.shape[:2] + a.shape[3:], a.dtype) for a in gs]
    res = _exchange(name, gs, half + half, plan)
    return res[:n], res[n:]


def _chip_scatter(name, ps):
    n = len(ps)

    def plan(place, ins, outs, count_only=False):
        if count_only:
            return n, 3 * n
        x, y, c = place
        me = 2 * x + y
        local = [(ins[t].at[:, me], outs[t].at[me]) for t in range(n)]
        remote = [(ins[t].at[:, 2 * px + py], outs[t].at[me], (px, py, c), outs[t].at[2 * px + py])
                  for (px, py) in _other_chips(x, y) for t in range(n)]
        return local, remote

    return _exchange(name, ps, [_sds((N_CHIPS, a.shape[0]) + a.shape[2:], a.dtype) for a in ps], plan)


def _pair_share(name, ts):
    n = len(ts)

    def plan(place, ins, outs, count_only=False):
        if count_only:
            return n, n
        x, y, c = place
        local = [(ins[t], outs[t].at[:, c]) for t in range(n)]
        remote = [(ins[t], outs[t].at[:, c], (x, y, 1 - c), outs[t].at[:, 1 - c]) for t in range(n)]
        return local, remote

    return _exchange(name, ts, [_sds((a.shape[0], 2) + a.shape[1:], a.dtype) for a in ts], plan)


def _gather_devices(name, v):
    flips = [(fx, fy, fc) for fx in (0, 1) for fy in (0, 1) for fc in (0, 1)][1:]

    def plan(place, ins, outs, count_only=False):
        if count_only:
            return 1, len(flips)
        x, y, c = place
        me = 4 * x + 2 * y + c
        remote = []
        for fx, fy, fc in flips:
            px, py, pc = (1 - x if fx else x), (1 - y if fy else y), (1 - c if fc else c)
            remote.append((ins[0], outs[0].at[me], (px, py, pc), outs[0].at[4 * px + 2 * py + pc]))
        return [(ins[0], outs[0].at[me])], remote

    return _exchange(name, [v], [_sds((8,) + v.shape, v.dtype)], plan)[0]


def _sum_slots(name, a):
    k, rows, cols = a.shape
    tr = _tile(rows, ROW_TILE, 8)
    ins = [(a, pl.BlockSpec((None, tr, cols), lambda i, j=j: (j, i, 0))) for j in range(k)]
    return _fwd_call(name, lambda pids, *v: (functools.reduce(lambda p, q: p + q, v),), (rows // tr,), ins,
                     [(_sds((rows, cols)), _rows(tr, cols), False)])[0]


def _add2(name, a, b):
    rows, cols = a.shape
    tr = _tile(rows, ROW_TILE, 8)
    return _fwd_call(name, lambda pids, p, q: (p + q,), (rows // tr,), [(a, _rows(tr, cols)), (b, _rows(tr, cols))],
                     [(_sds((rows, cols)), _rows(tr, cols), False)])[0]


def _adam_fn(pids, w, g, m, v):
    m2 = ADAM_B1 * m + (1.0 - ADAM_B1) * g
    v2 = ADAM_B2 * v + (1.0 - ADAM_B2) * (g * g)
    m_hat = m2 / (1.0 - ADAM_B1 ** ADAM_STEP)
    v_hat = v2 / (1.0 - ADAM_B2 ** ADAM_STEP)
    return -ADAM_LR * (m_hat / (jnp.sqrt(v_hat) + ADAM_EPS) + ADAM_WD * w), m2, v2


def _adam(name, w, g, m, v):
    shape = w.shape
    cols = shape[-1]
    rows = w.size // cols
    tr = _tile(rows, ROW_TILE, 8)
    ins = [(a.reshape(rows, cols), _rows(tr, cols)) for a in (w, g, m, v)]
    outs = _fwd_call(name, _adam_fn, (rows // tr,), ins, [(_sds((rows, cols)), _rows(tr, cols), False)] * 3)
    return [o.reshape(shape) for o in outs]


def _pack(xs):
    flat = jnp.concatenate([a.reshape(-1).astype(F32) for a in xs])
    pad = (-flat.size) % (8 * LANES)
    return jnp.pad(flat, (0, pad)).reshape(-1, LANES)


def _unpack(buf, shapes):
    flat, out, off = buf.reshape(-1), [], 0
    for s in shapes:
        n = math.prod(s)
        out.append(flat[off:off + n].reshape(s))
        off += n
    return out


_WEIGHTS = ("ffn1_norm", "ffn1_w_gate", "ffn1_w_up", "ffn1_w_down", "mix_norm", "ffn2_norm", "ffn2_w_gate", "ffn2_w_up",
            "ffn2_w_down", "ab_w_in", "pool_w", "pool_scale", "dn_conv_w", "dn_a_log", "dn_dt_bias", "dn_out_norm",
            "ab_w_out", "cd_w_in", "sgu_norm_g", "sgu_norm_b", "sgu_w", "sgu_bias", "sc_conv_w", "cd_w_out", "final_norm")
_BIG = ("ffn1_w_gate", "ffn1_w_up", "ffn1_w_down", "ffn2_w_gate", "ffn2_w_up", "ffn2_w_down", "ab_w_in", "ab_w_out",
        "cd_w_in", "cd_w_out")
_ROW_SHARDED = ("ffn1_w_down", "ffn2_w_down", "ab_w_out", "cd_w_out")
_SMALL_SHARDED = ("dn_conv_w", "sgu_norm_g", "sgu_norm_b", "sc_conv_w")
_SMALL = tuple(n for n in _WEIGHTS if n not in _BIG)


def _to_slots(name, g):
    L, A, Bt = g.shape
    if name in _ROW_SHARDED:
        return g.reshape(L, N_CHIPS, A // N_CHIPS, Bt)
    return g.reshape(L, A, N_CHIPS, Bt // N_CHIPS).transpose(0, 2, 1, 3)


def kernel(x, ffn1_norm, ffn1_w_gate, ffn1_w_up, ffn1_w_down, mix_norm, ffn2_norm, ffn2_w_gate, ffn2_w_up, ffn2_w_down,
           ab_w_in, pool_w, pool_scale, dn_conv_w, dn_a_log, dn_dt_bias, dn_out_norm, ab_w_out, cd_w_in, sgu_norm_g,
           sgu_norm_b, sgu_w, sgu_bias, sc_conv_w, cd_w_out, final_norm, loss_target,
           m_ffn1_norm, m_ffn1_w_gate, m_ffn1_w_up, m_ffn1_w_down, m_mix_norm, m_ffn2_norm, m_ffn2_w_gate, m_ffn2_w_up,
           m_ffn2_w_down, m_ab_w_in, m_pool_w, m_pool_scale, m_dn_conv_w, m_dn_a_log, m_dn_dt_bias, m_dn_out_norm,
           m_ab_w_out, m_cd_w_in, m_sgu_norm_g, m_sgu_norm_b, m_sgu_w, m_sgu_bias, m_sc_conv_w, m_cd_w_out, m_final_norm,
           v_ffn1_norm, v_ffn1_w_gate, v_ffn1_w_up, v_ffn1_w_down, v_mix_norm, v_ffn2_norm, v_ffn2_w_gate, v_ffn2_w_up,
           v_ffn2_w_down, v_ab_w_in, v_pool_w, v_pool_scale, v_dn_conv_w, v_dn_a_log, v_dn_dt_bias, v_dn_out_norm,
           v_ab_w_out, v_cd_w_in, v_sgu_norm_g, v_sgu_norm_b, v_sgu_w, v_sgu_bias, v_sc_conv_w, v_cd_w_out, v_final_norm):
    A = dict(locals())
    B, S, D = x.shape
    T = B * S
    xi, yi, _ = _place()
    chip = 2 * xi + yi

    sh_shapes = [A[n].shape for n in _SMALL_SHARDED]
    gathered = _gather_chips("gather_weights", [A[n].astype(BF16) for n in _BIG] + [_pack([A[n] for n in _SMALL_SHARDED])])
    gw = dict(zip(_BIG, gathered[:-1]))
    per_chip = [_unpack(gathered[-1][p], sh_shapes) for p in range(N_CHIPS)]
    W = {n: A[n] for n in _SMALL if n not in _SMALL_SHARDED}
    for j, n in enumerate(_SMALL_SHARDED):
        W[n] = jnp.concatenate([per_chip[p][j] for p in range(N_CHIPS)], axis=-1)
    for f in ("ffn1", "ffn2"):
        W[f + "_g"], W[f + "_u"], W[f + "_d"] = gw[f + "_w_gate"], gw[f + "_w_up"], gw[f + "_w_down"]
    cols = lambda g: jnp.concatenate([g[p] for p in range(N_CHIPS)], axis=-1)
    rows = lambda g: jnp.concatenate([g[p] for p in range(N_CHIPS)], axis=1)
    ab_in = cols(gw["ab_w_in"])
    ab_cols = ab_in.shape[-1]
    ab_pad = (-ab_cols) % LANES
    W["ab_in"] = jnp.pad(ab_in, ((0, 0), (0, 0), (0, ab_pad)))
    W["cd_in"], W["ab_out"], W["cd_out"] = cols(gw["cd_w_in"]), rows(gw["ab_w_out"]), rows(gw["cd_w_out"])

    loss, dx, big, small = _local_step(x.reshape(T, D), loss_target.reshape(T, D), W, B, S)
    loss = lax.psum(loss, ("x", "y", "c"))

    big["ab_w_in"] = big["ab_w_in"][:, :, :ab_cols]
    slots = []
    for n in _BIG:
        g = big[n] if big[n].ndim == 4 else _to_slots(n, big[n])
        L, P4, Aa, Bt = g.shape
        slots.append(g.reshape(L, P4, 2, Aa // 2, Bt))
    mine, theirs = _pair_split("grads_pair_split", slots)
    pair = [_add2(f"grads_pair_sum_{n}", a.reshape(-1, a.shape[-1]), b.reshape(-1, a.shape[-1])).reshape(a.shape)
            for n, a, b in zip(_BIG, mine, theirs)]
    parts = _chip_scatter("grads_chip_scatter", pair)
    tot = [_sum_slots(f"grads_chip_sum_{n}", p.reshape(N_CHIPS, -1, p.shape[-1])).reshape(p.shape[1:])
           for n, p in zip(_BIG, parts)]
    full = _pair_share("grads_pair_share", tot)
    grads = {n: f.reshape(A[n].shape) for n, f in zip(_BIG, full)}

    sm_shapes = [small[n].shape for n in _SMALL]
    every = _gather_devices("gather_small_grads", _pack([small[n] for n in _SMALL]))
    for n, g in zip(_SMALL, _unpack(_sum_slots("small_grads_sum", every), sm_shapes)):
        if n in _SMALL_SHARDED:
            w_loc = A[n].shape[-1]
            g = lax.dynamic_slice_in_dim(g, chip * w_loc, w_loc, axis=g.ndim - 1)
        grads[n] = g.reshape(A[n].shape)

    delta, new_m, new_v = {}, {}, {}
    for n in _BIG:
        delta[n], new_m[n], new_v[n] = _adam(f"adam_{n}", A[n], grads[n], A["m_" + n], A["v_" + n])
    packed = [_pack([d[n] for n in _SMALL]) for d in (A, grads, {n: A["m_" + n] for n in _SMALL},
                                                       {n: A["v_" + n] for n in _SMALL})]
    loc_shapes = [A[n].shape for n in _SMALL]
    for d, buf in zip((delta, new_m, new_v), _adam("adam_small", *packed)):
        d.update(zip(_SMALL, _unpack(buf, loc_shapes)))

    return (loss, dx.reshape(B, S, D), *[grads[n] for n in _WEIGHTS], *[delta[n] for n in _WEIGHTS],
            *[new_m[n] for n in _WEIGHTS], *[new_v[n] for n in _WEIGHTS])
```

```python
import functools
import math

import jax
import jax.numpy as jnp
from jax import lax
from jax.experimental import pallas as pl
from jax.experimental.pallas import tpu as pltpu

F32, BF16 = jnp.float32, jnp.bfloat16
HIGHEST = lax.Precision.HIGHEST
MESH = pl.DeviceIdType.MESH

EPS = 1e-6
LANES = 128
CH = 64
PAIR = 2 * CH
POOL_WINDOWS = (2, 4, 8, 16)
N_CHIPS = 4
ADAM_LR, ADAM_B1, ADAM_B2, ADAM_EPS, ADAM_WD, ADAM_STEP = 0.001, 0.9, 0.999, 1e-08, 0.01, 10
VMEM_LIMIT = 56 * 1024 * 1024


def _pcall(body, **kw):
    return pl.pallas_call(body, **kw)


def _params(nd):
    return pltpu.CompilerParams(dimension_semantics=("arbitrary",) * nd, vmem_limit_bytes=VMEM_LIMIT)


def _sds(shape, dtype=F32):
    return jax.ShapeDtypeStruct(tuple(shape), dtype)


def _tile(n, cap, mult=LANES):
    if n <= cap:
        return n
    best = None
    for t in range(mult, cap + 1, mult):
        if n % t == 0:
            best = t
    assert best is not None, (n, cap)
    return best


def _dg(a, b, ca, cb):
    return lax.dot_general(a.astype(BF16), b.astype(BF16), (((ca,), (cb,)), ((), ())),
                           preferred_element_type=F32)


@jax.custom_vjp
def _bdot(a, b):
    return _dg(a, b, 1, 0)


def _bdot_fwd(a, b):
    return _dg(a, b, 1, 0), (a, b)


def _bdot_bwd(res, g):
    a, b = res
    return _dg(g, b, 1, 1), _dg(a, g, 0, 0)


_bdot.defvjp(_bdot_fwd, _bdot_bwd)


@jax.custom_vjp
def _bdot_nt(a, b):
    return _dg(a, b, 1, 1)


def _bdot_nt_fwd(a, b):
    return _dg(a, b, 1, 1), (a, b)


def _bdot_nt_bwd(res, g):
    a, b = res
    return _dg(g, b, 1, 0), _dg(g, a, 0, 0)


_bdot_nt.defvjp(_bdot_nt_fwd, _bdot_nt_bwd)


def _hdot(a, b):
    return jnp.dot(a, b, precision=HIGHEST, preferred_element_type=F32)


def _shift_raw(x, k):
    n = x.shape[0]
    rows = lax.broadcasted_iota(jnp.int32, x.shape, 0)
    r = pltpu.roll(x, k % n, 0)
    if k > 0:
        return jnp.where(rows >= k, r, 0.0)
    return jnp.where(rows < n + k, r, 0.0)


@functools.partial(jax.custom_vjp, nondiff_argnums=(1,))
def _shift(x, k):
    return _shift_raw(x, k)


def _shift_fwd(x, k):
    return _shift_raw(x, k), None


def _shift_bwd(k, _, g):
    return (_shift(g, -k),)


_shift.defvjp(_shift_fwd, _shift_bwd)


def _col(x, j):
    lanes = lax.broadcasted_iota(jnp.int32, x.shape, x.ndim - 1)
    return jnp.sum(jnp.where(lanes == j, x, 0.0), axis=-1, keepdims=True)


def _rms(x, g):
    return x * lax.rsqrt(jnp.mean(x * x, axis=-1, keepdims=True) + EPS) * g


def _sigmoid(x):
    return 1.0 / (1.0 + jnp.exp(-x))


def _silu(x):
    return x * _sigmoid(x)


def _softplus(x):
    return jnp.maximum(x, 0.0) + jnp.log(1.0 + jnp.exp(-jnp.abs(x)))


def _gelu(x):
    c = math.sqrt(2.0 / math.pi)
    return 0.5 * x * (1.0 + jnp.tanh(c * (x + 0.044715 * (x * x * x))))


def _first(axes):
    c = None
    for a in axes:
        t = pl.program_id(a) == 0
        c = t if c is None else jnp.logical_and(c, t)
    return c


def _fwd_call(name, fn, grid, ins, outs, acc_axes=()):
    nin = len(ins)

    def body(*refs):
        pids = tuple(pl.program_id(a) for a in range(len(grid)))
        first = _first(acc_axes) if acc_axes else None
        vals = [r[...].astype(F32) for r in refs[:nin]]
        res = fn(pids, *vals)
        for r, o, (_, _, acc) in zip(refs[nin:], res, outs):
            if acc:
                @pl.when(first)
                def _(r=r):
                    r[...] = jnp.zeros_like(r)
                r[...] += o.astype(r.dtype)
            else:
                r[...] = o.astype(r.dtype)

    return _pcall(body, grid=grid, in_specs=[s for _, s in ins], out_specs=[s for _, s, _ in outs],
                  out_shape=[s for s, _, _ in outs], name=name,
                  compiler_params=_params(len(grid)))(*[a for a, _ in ins])


def _bwd_call(name, fn, grid, ins, cts, gouts, acc_axes=(), addends=()):
    nin, nct, nadd = len(ins), len(cts), len(addends)

    def body(*refs):
        pids = tuple(pl.program_id(a) for a in range(len(grid)))
        first = _first(acc_axes) if acc_axes else None
        vals = [r[...].astype(F32) for r in refs[:nin]]
        ctv = tuple(r[...].astype(F32) for r in refs[nin:nin + nct])
        addv = {pos: refs[nin + nct + n][...].astype(F32) for n, (pos, _, _) in enumerate(addends)}
        _, vjp = jax.vjp(lambda *v: tuple(fn(pids, *v)), *vals)
        grads = vjp(ctv)
        for pos, (r, (idx, _, _, acc)) in enumerate(zip(refs[nin + nct + nadd:], gouts)):
            gval = grads[idx]
            if pos in addv:
                gval = gval + addv[pos]
            if acc:
                @pl.when(first)
                def _(r=r):
                    r[...] = jnp.zeros_like(r)
                r[...] += gval.astype(r.dtype)
            else:
                r[...] = gval.astype(r.dtype)

    args = [a for a, _ in ins] + [a for a, _ in cts] + [a for _, a, _ in addends]
    specs = [s for _, s in ins] + [s for _, s in cts] + [s for _, _, s in addends]
    return _pcall(body, grid=grid, in_specs=specs, out_specs=[s for _, _, s, _ in gouts],
                  out_shape=[s for _, s, _, _ in gouts], name=name,
                  compiler_params=_params(len(grid)))(*args)


def _mm(name, a, b, *, ta=False, tb=False, out_dtype=F32, res=None, scale=1.0, tm_cap=512, tn_cap=1024,
        tk_cap=1024):
    if ta:
        K, M = a.shape
    else:
        M, K = a.shape
    N = b.shape[0] if tb else b.shape[1]
    tm, tn, tk = _tile(M, tm_cap), _tile(N, tn_cap), _tile(K, tk_cap)
    nk = K // tk
    a_spec = pl.BlockSpec((tk, tm), lambda i, j, k: (k, i)) if ta else pl.BlockSpec((tm, tk), lambda i, j, k: (i, k))
    b_spec = pl.BlockSpec((tn, tk), lambda i, j, k: (j, k)) if tb else pl.BlockSpec((tk, tn), lambda i, j, k: (k, j))
    o_spec = pl.BlockSpec((tm, tn), lambda i, j, k: (i, j))
    ca, cb = (0 if ta else 1), (1 if tb else 0)
    has_res = res is not None

    def body(*refs):
        a_ref, b_ref = refs[0], refs[1]
        r_ref = refs[2] if has_res else None
        o_ref, acc = refs[-2], refs[-1]
        k = pl.program_id(2)

        @pl.when(k == 0)
        def _():
            acc[...] = jnp.zeros_like(acc)

        acc[...] += _dg(a_ref[...], b_ref[...], ca, cb)

        @pl.when(k == nk - 1)
        def _():
            val = acc[...] * scale if scale != 1.0 else acc[...]
            if has_res:
                val = r_ref[...].astype(F32) + val
            o_ref[...] = val.astype(o_ref.dtype)

    args, specs = [a, b], [a_spec, b_spec]
    if has_res:
        args.append(res)
        specs.append(o_spec)
    return _pcall(body, grid=(M // tm, N // tn, nk), in_specs=specs, out_specs=o_spec,
                  out_shape=_sds((M, N), out_dtype), scratch_shapes=[pltpu.VMEM((tm, tn), F32)], name=name,
                  compiler_params=_params(3))(*args)


def _ffn_fwd(name, h, g, wg, wu, wd, l, tm):
    T, D = h.shape
    ns, _, _, Fs = wg.shape

    def body(h_ref, g_ref, wg_ref, wu_ref, wd_ref, ho_ref, gate_ref, up_ref, hn_s, acc_s):
        s = pl.program_id(1)

        @pl.when(s == 0)
        def _():
            hn_s[...] = _rms(h_ref[...], g_ref[...]).astype(BF16)
            acc_s[...] = jnp.zeros_like(acc_s)

        hn = hn_s[...]
        gate = jnp.dot(hn, wg_ref[...], preferred_element_type=F32)
        up = jnp.dot(hn, wu_ref[...], preferred_element_type=F32)
        gate_ref[...] = gate.astype(BF16)
        up_ref[...] = up.astype(BF16)
        act = (_silu(gate) * up).astype(BF16)
        acc_s[...] += jnp.dot(act, wd_ref[...], preferred_element_type=F32)

        @pl.when(s == ns - 1)
        def _():
            ho_ref[...] = h_ref[...] + 0.5 * acc_s[...]

    row = pl.BlockSpec((tm, D), lambda i, s: (i, 0))
    wcol = pl.BlockSpec((None, None, D, Fs), lambda i, s: (s, l, 0, 0))
    wrow = pl.BlockSpec((None, None, Fs, D), lambda i, s: (s, l, 0, 0))
    slot = pl.BlockSpec((None, tm, Fs), lambda i, s: (s, i, 0))
    return _pcall(body, grid=(T // tm, ns), in_specs=[row, pl.BlockSpec((1, D), lambda i, s: (0, 0)), wcol, wcol, wrow],
                  out_specs=[row, slot, slot],
                  out_shape=[_sds((T, D)), _sds((ns, T, Fs), BF16), _sds((ns, T, Fs), BF16)],
                  scratch_shapes=[pltpu.VMEM((tm, D), BF16), pltpu.VMEM((tm, D), F32)], name=name,
                  compiler_params=_params(2))(h, g, wg, wu, wd)


def _ffn_bwd_x(name, dh, h, g, gate, up, wg, wu, wd, l, tm):
    T, D = h.shape
    ns, _, _, Fs = wg.shape

    def body(dh_ref, h_ref, g_ref, gate_ref, up_ref, wg_ref, wu_ref, wd_ref,
             dho_ref, dgate_ref, dup_ref, dgain_ref, do_s, acc_s):
        i, s = pl.program_id(0), pl.program_id(1)

        @pl.when(s == 0)
        def _():
            do_s[...] = (0.5 * dh_ref[...]).astype(BF16)
            acc_s[...] = jnp.zeros_like(acc_s)

        @pl.when(jnp.logical_and(i == 0, s == 0))
        def _():
            dgain_ref[...] = jnp.zeros_like(dgain_ref)

        dact = _dg(do_s[...], wd_ref[...], 1, 1)
        gt, u = gate_ref[...].astype(F32), up_ref[...].astype(F32)
        sg = _sigmoid(gt)
        dgt = (dact * u * (sg * (1.0 + gt * (1.0 - sg)))).astype(BF16)
        du = (dact * (gt * sg)).astype(BF16)
        dgate_ref[...] = dgt
        dup_ref[...] = du
        acc_s[...] += _dg(dgt, wg_ref[...], 1, 1) + _dg(du, wu_ref[...], 1, 1)

        @pl.when(s == ns - 1)
        def _():
            _, vjp = jax.vjp(_rms, h_ref[...], g_ref[...])
            dx, dgain = vjp(acc_s[...])
            dho_ref[...] = dh_ref[...] + dx
            dgain_ref[...] += dgain

    row = pl.BlockSpec((tm, D), lambda i, s: (i, 0))
    gain = pl.BlockSpec((1, D), lambda i, s: (0, 0))
    wcol = pl.BlockSpec((None, None, D, Fs), lambda i, s: (s, l, 0, 0))
    wrow = pl.BlockSpec((None, None, Fs, D), lambda i, s: (s, l, 0, 0))
    slot = pl.BlockSpec((None, tm, Fs), lambda i, s: (s, i, 0))
    return _pcall(body, grid=(T // tm, ns), in_specs=[row, row, gain, slot, slot, wcol, wcol, wrow],
                  out_specs=[row, slot, slot, gain],
                  out_shape=[_sds((T, D)), _sds((ns, T, Fs), BF16), _sds((ns, T, Fs), BF16), _sds((1, D))],
                  scratch_shapes=[pltpu.VMEM((tm, D), BF16), pltpu.VMEM((tm, D), F32)], name=name,
                  compiler_params=_params(2))(dh, h, g, gate, up, wg, wu, wd)


def _ffn_bwd_w(name, dh, h, g, gate, up, dgate, dup, tm):
    T, D = h.shape
    ns, _, Fs = gate.shape

    def body(dh_ref, h_ref, g_ref, gate_ref, up_ref, dgate_ref, dup_ref, dwg_ref, dwu_ref, dwd_ref):
        @pl.when(pl.program_id(1) == 0)
        def _():
            dwg_ref[...] = jnp.zeros_like(dwg_ref)
            dwu_ref[...] = jnp.zeros_like(dwu_ref)
            dwd_ref[...] = jnp.zeros_like(dwd_ref)

        hn = _rms(h_ref[...], g_ref[...]).astype(BF16)
        act = (_silu(gate_ref[...].astype(F32)) * up_ref[...].astype(F32)).astype(BF16)
        do = (0.5 * dh_ref[...]).astype(BF16)
        dwg_ref[...] += _dg(hn, dgate_ref[...], 0, 0)
        dwu_ref[...] += _dg(hn, dup_ref[...], 0, 0)
        dwd_ref[...] += _dg(act, do, 0, 0)

    row = pl.BlockSpec((tm, D), lambda s, j: (j, 0))
    gain = pl.BlockSpec((1, D), lambda s, j: (0, 0))
    slot = pl.BlockSpec((None, tm, Fs), lambda s, j: (s, j, 0))
    wcol = pl.BlockSpec((None, D, Fs), lambda s, j: (s, 0, 0))
    wrow = pl.BlockSpec((None, Fs, D), lambda s, j: (s, 0, 0))
    return _pcall(body, grid=(ns, T // tm), in_specs=[row, row, gain, slot, slot, slot, slot],
                  out_specs=[wcol, wcol, wrow],
                  out_shape=[_sds((ns, D, Fs)), _sds((ns, D, Fs)), _sds((ns, Fs, D))], name=name,
                  compiler_params=_params(2))(dh, h, g, gate, up, dgate, dup)


def _rms_fn(pids, x, g):
    return (_rms(x, g),)


def _pool_fn(pids, a, w, scale):
    rows = lax.broadcasted_iota(jnp.int32, (a.shape[0], LANES), 0)
    outs = []
    for gi, win in enumerate(POOL_WINDOWS):
        ag = a[:, gi * LANES:(gi + 1) * LANES]
        s, k = ag, 1
        while k < win:
            s = s + _shift(s, k)
            k *= 2
        cnt = jnp.minimum(rows + 1, win).astype(F32)
        pooled = s / cnt - ag
        outs.append(_bdot(pooled, w[gi]) * scale[:, gi * LANES:(gi + 1) * LANES])
    return (jnp.concatenate(outs, axis=-1),)


def _conv_taps(x, cw):
    K = cw.shape[0]
    y = cw[K - 1] * x
    for j in range(K - 1):
        y = y + cw[j] * _shift(x, K - 1 - j)
    return y


def _prep_fn(pids, q, k, v, bg, cwq, cwk, cwv, alog, dtb):
    hd = pids[0]
    q, k, v = _silu(_conv_taps(q, cwq)), _silu(_conv_taps(k, cwk)), _silu(_conv_taps(v, cwv))
    q = q * lax.rsqrt(jnp.sum(q * q, axis=-1, keepdims=True) + EPS) * (LANES ** -0.5)
    k = k * lax.rsqrt(jnp.sum(k * k, axis=-1, keepdims=True) + EPS)
    beta = _sigmoid(_col(bg, hd))
    g = -jnp.exp(_col(alog, hd)) * _softplus(_col(bg, hd + 4) + _col(dtb, hd))
    return q, k, v, jnp.broadcast_to(beta, q.shape), jnp.broadcast_to(g, q.shape)


def _post_fn(pids, o, z, onorm):
    return (_rms(o, onorm) * _silu(z),)


def _sgu_fn(pids, u, v, g, b, ws, bias_t):
    u, v = _gelu(u), _gelu(v)
    mu = jnp.mean(v, axis=-1, keepdims=True)
    xc = v - mu
    vn = xc * lax.rsqrt(jnp.mean(xc * xc, axis=-1, keepdims=True) + EPS) * g + b
    r = lax.broadcasted_iota(jnp.int32, (LANES, LANES), 0)
    c = lax.broadcasted_iota(jnp.int32, (LANES, LANES), 1)
    rows = []
    for n in range(u.shape[0] // LANES):
        heads = []
        for hd in range(4):
            wm = jnp.where(r >= c, ws[hd], 0.0)
            blk = vn[n * LANES:(n + 1) * LANES, hd * LANES:(hd + 1) * LANES]
            heads.append(_bdot(wm, blk) + _col(bias_t, hd))
        rows.append(jnp.concatenate(heads, axis=-1))
    mixed = jnp.concatenate(rows, axis=0) if len(rows) > 1 else rows[0]
    return (u * mixed,)


def _sconv_fn(pids, xd, bgate, cg, cw):
    return (bgate * _conv_taps(cg * xd, cw),)


def _loss_fn(pids, h, g, tgt):
    err = _rms(h, g) - tgt
    tot = 0.5 * jnp.sum(jnp.mean(err * err, axis=-1, keepdims=True), axis=0, keepdims=True)
    return (jnp.broadcast_to(tot, (1, LANES)),)


def _stage_a(q, k, v, beta, g):
    n = q.shape[0]
    r = lax.broadcasted_iota(jnp.int32, (n, n), 0)
    c = lax.broadcasted_iota(jnp.int32, (n, n), 1)
    same = jnp.right_shift(r, 6) == jnp.right_shift(c, 6)
    tri = jnp.logical_and(same, r >= c)
    strict = jnp.logical_and(same, r > c)
    gc = _hdot(tri.astype(F32), g)
    gt = _hdot(same.astype(F32), g)
    gamma = jnp.exp(jnp.where(tri, gc - gc.T, -jnp.inf))
    kb = k * beta
    lmat = jnp.where(strict, _bdot_nt(kb, k) * gamma, 0.0)
    p = -lmat
    tinv = (r == c).astype(F32) + p
    for _ in range(5):
        p = _bdot(p, p)
        tinv = tinv + _bdot(tinv, p)
    egc = jnp.exp(gc)
    u = _bdot(tinv, v * beta)
    w = _bdot(tinv, kb * egc)
    a = _bdot_nt(q, k) * gamma
    return u, w, q * egc, a, k * jnp.exp(gt - gc), jnp.exp(gt)


def _scan_pair_fwd(s0, u, w, qd, a, kd, l0, l1):
    vn0 = u[:CH] - _dg(w[:CH], s0, 1, 0)
    s1 = s0 * l0 + _dg(kd[:CH], vn0, 0, 0)
    vn1 = u[CH:] - _dg(w[CH:], s1, 1, 0)
    s2 = s1 * l1 + _dg(kd[CH:], vn1, 0, 0)
    o = (jnp.concatenate([_dg(qd[:CH], s0, 1, 0), _dg(qd[CH:], s1, 1, 0)], axis=0)
         + _dg(a, jnp.concatenate([vn0, vn1], axis=0), 1, 0))
    return o, s1, s2


def _scan_chunk_bwd(s, ds_next, do, dvn_o, u, w, qd, kd, lrow):
    vn = u - _dg(w, s, 1, 0)
    dvn = dvn_o + _dg(kd, ds_next, 1, 0)
    dkd = _dg(vn, ds_next, 1, 1)
    dl = jnp.sum(ds_next * s, axis=0, keepdims=True)
    dw = -_dg(dvn, s, 1, 1)
    dqd = _dg(do, s, 1, 1)
    ds = _dg(qd, do, 0, 0) + lrow * ds_next - _dg(w, dvn, 0, 0)
    return ds, vn, dvn, dw, dqd, dkd, dl


def _delta_fwd(name, qn, kn, vn, beta, g, B, S):
    T = qn.shape[0]
    npair = S // PAIR

    def body(q_ref, k_ref, v_ref, b_ref, g_ref, o_ref, st_ref, l_s):
        def step(p, s0, slot):
            rows = pl.ds(pl.multiple_of(p * PAIR, PAIR), PAIR)
            u, w, qd, a, kd, l = _stage_a(q_ref[rows, :], k_ref[rows, :], v_ref[rows, :], b_ref[rows, :], g_ref[rows, :])
            l_s[slot] = l
            o, s1, s2 = _scan_pair_fwd(s0, u, w, qd, a, kd, l_s[slot, 0:1, :], l_s[slot, CH:CH + 1, :])
            o_ref[rows, :] = o
            st_ref[2 * p] = s0
            st_ref[2 * p + 1] = s1
            return s2

        lax.fori_loop(0, npair // 2, lambda i, s: step(2 * i + 1, step(2 * i, s, 0), 1), jnp.zeros((LANES, LANES), F32))

    blk = pl.BlockSpec((S, LANES), lambda b, h: (b, h))
    st = pl.BlockSpec((None, 2 * npair, LANES, LANES), lambda b, h: (b * 4 + h, 0, 0, 0))
    return _pcall(body, grid=(B, 4), in_specs=[blk] * 5, out_specs=[blk, st],
                  out_shape=[_sds((T, 4 * LANES)), _sds((B * 4, 2 * npair, LANES, LANES))],
                  scratch_shapes=[pltpu.VMEM((2, PAIR, LANES), F32)], name=name,
                  compiler_params=_params(2))(qn, kn, vn, beta, g)


def _delta_bwd(name, qn, kn, vn, beta, g, states, do, B, S):
    T = qn.shape[0]
    npair = S // PAIR

    def body(q_ref, k_ref, v_ref, b_ref, g_ref, st_ref, do_ref, dq_ref, dk_ref, dv_ref, db_ref, dg_ref, l_s):
        rowid = lax.broadcasted_iota(jnp.int32, (PAIR, LANES), 0)

        def step(t, ds2, slot):
            p = npair - 1 - t
            rows = pl.ds(pl.multiple_of(p * PAIR, PAIR), PAIR)
            ins = (q_ref[rows, :], k_ref[rows, :], v_ref[rows, :], b_ref[rows, :], g_ref[rows, :])
            (u, w, qd, a, kd, l), vjp = jax.vjp(_stage_a, *ins)
            l_s[slot] = l
            l0, l1 = l_s[slot, 0:1, :], l_s[slot, CH:CH + 1, :]
            s0, s1 = st_ref[2 * p], st_ref[2 * p + 1]
            dout = do_ref[rows, :]
            dvn_o = _dg(a, dout, 0, 0)
            ds1, vn1, dvn1, dw1, dqd1, dkd1, dl1 = _scan_chunk_bwd(
                s1, ds2, dout[CH:], dvn_o[CH:], u[CH:], w[CH:], qd[CH:], kd[CH:], l1)
            ds0, vn0, dvn0, dw0, dqd0, dkd0, dl0 = _scan_chunk_bwd(
                s0, ds1, dout[:CH], dvn_o[:CH], u[:CH], w[:CH], qd[:CH], kd[:CH], l0)
            da = _dg(dout, jnp.concatenate([vn0, vn1], axis=0), 1, 1)
            cat = lambda x0, x1: jnp.concatenate([x0, x1], axis=0)
            dl = jnp.where(rowid == 0, dl0, jnp.where(rowid == CH, dl1, 0.0))
            dq, dk, dv, db, dg = vjp((cat(dvn0, dvn1), cat(dw0, dw1), cat(dqd0, dqd1), da, cat(dkd0, dkd1), dl))
            dq_ref[rows, :] = dq
            dk_ref[rows, :] = dk
            dv_ref[rows, :] = dv
            db_ref[rows, :] = db
            dg_ref[rows, :] = dg
            return ds0

        lax.fori_loop(0, npair // 2, lambda i, s: step(2 * i + 1, step(2 * i, s, 0), 1), jnp.zeros((LANES, LANES), F32))

    blk = pl.BlockSpec((S, LANES), lambda b, h: (b, h))
    st = pl.BlockSpec((None, 2 * npair, LANES, LANES), lambda b, h: (b * 4 + h, 0, 0, 0))
    return _pcall(body, grid=(B, 4), in_specs=[blk] * 5 + [st, blk], out_specs=[blk] * 5,
                  out_shape=[_sds((T, 4 * LANES))] * 5, scratch_shapes=[pltpu.VMEM((2, PAIR, LANES), F32)], name=name,
                  compiler_params=_params(2))(qn, kn, vn, beta, g, states, do)


ROW_TILE = 512
FFN_TILE = 512
SGU_ROWS = 256


def _rows(r, c, off=0):
    return pl.BlockSpec((r, c), lambda i: (i, off))


def _whole(shape, nd):
    zeros = (0,) * len(shape)
    if nd == 1:
        return pl.BlockSpec(shape, lambda i: zeros)
    return pl.BlockSpec(shape, lambda i, j: zeros)


def _norm_fwd(name, h, g):
    T, D = h.shape
    tr = _tile(T, ROW_TILE, 8)
    return _fwd_call(name, _rms_fn, (T // tr,), [(h, _rows(tr, D)), (g, _whole((1, D), 1))],
                     [(_sds((T, D), BF16), _rows(tr, D), False)])[0]


def _norm_bwd(name, h, g, dhn, dh_res):
    T, D = h.shape
    tr = _tile(T, ROW_TILE, 8)
    return _bwd_call(name, _rms_fn, (T // tr,), [(h, _rows(tr, D)), (g, _whole((1, D), 1))], [(dhn, _rows(tr, D))],
                     [(0, _sds((T, D)), _rows(tr, D), False), (1, _sds((1, D)), _whole((1, D), 1), True)],
                     acc_axes=(0,), addends=[(0, dh_res, _rows(tr, D))])


def _ab_specs(B, S, P):
    W4 = 4 * LANES
    seq4 = pl.BlockSpec((S, W4), lambda b: (b, 0))
    pool_ins = lambda proj: [(proj, seq4), (P["pool_w"], _whole((4, LANES, LANES), 1)), (P["pool_scale"], _whole((1, W4), 1))]
    hb = lambda off: pl.BlockSpec((S, LANES), lambda b, h: (b, off + h))
    cw = lambda off: pl.BlockSpec((4, 1, LANES), lambda b, h: (0, 0, off + h))
    small = _whole((1, LANES), 2)
    prep_ins = lambda proj: [(proj, hb(4)), (proj, hb(8)), (proj, hb(12)),
                             (proj, pl.BlockSpec((S, LANES), lambda b, h: (b, 20))),
                             (P["conv_w"], cw(0)), (P["conv_w"], cw(4)), (P["conv_w"], cw(8)),
                             (P["a_log"], small), (P["dt_bias"], small)]
    post_ins = lambda o, proj: [(o, hb(0)), (proj, hb(16)), (P["out_norm"], small)]
    return seq4, pool_ins, hb, prep_ins, post_ins, small


def _ab_fwd(tag, h, P, B, S):
    T, D = h.shape
    W4 = 4 * LANES
    seq4, pool_ins, hb, prep_ins, post_ins, small = _ab_specs(B, S, P)
    hn = _norm_fwd(tag + "_norm", h, P["mix_norm"])
    proj = _mm(tag + "_in", hn, P["w_in"])
    ya = _fwd_call(tag + "_pool", _pool_fn, (B,), pool_ins(proj), [(_sds((T, W4), BF16), seq4, False)])[0]
    qn, kn, vn, beta, g = _fwd_call(tag + "_prep", lambda pids, *v: _prep_fn((pids[1],), *v), (B, 4), prep_ins(proj),
                                    [(_sds((T, W4)), hb(0), False)] * 5)
    o, states = _delta_fwd(tag + "_delta", qn, kn, vn, beta, g, B, S)
    yb = _fwd_call(tag + "_post", _post_fn, (B, 4), post_ins(o, proj), [(_sds((T, W4), BF16), hb(0), False)])[0]
    y = jnp.concatenate([ya, yb], axis=-1)
    h_new = _mm(tag + "_out", y, P["w_out"], res=h)
    return h_new, (h, hn, proj, qn, kn, vn, beta, g, states, o, y)


def _ab_bwd(tag, dh, saved, P, B, S):
    h, hn, proj, qn, kn, vn, beta, g, states, o, y = saved
    T, D = h.shape
    W4 = 4 * LANES
    seq4, pool_ins, hb, prep_ins, post_ins, small = _ab_specs(B, S, P)
    dy = _mm(tag + "_out_dx", dh, P["w_out"], tb=True)
    dw_out = _mm(tag + "_out_dw", y, dh, ta=True)
    do, dz, d_onorm = _bwd_call(tag + "_post_b", _post_fn, (B, 4), post_ins(o, proj), [(dy, hb(4))],
                                [(0, _sds((T, W4)), hb(0), False), (1, _sds((T, W4)), hb(0), False),
                                 (2, _sds((1, LANES)), small, True)], acc_axes=(0, 1))
    dqn, dkn, dvn, dbeta, dg = _delta_bwd(tag + "_delta_b", qn, kn, vn, beta, g, states, do, B, S)
    per_b = pl.BlockSpec((None, 1, LANES), lambda b, h: (b, 0, 0))
    dcw = pl.BlockSpec((None, 4, 1, LANES), lambda b, h: (b, 0, 0, h))
    dq, dk, dv, dbg, dcq, dck, dcv, dalog, ddt = _bwd_call(
        tag + "_prep_b", lambda pids, *v: _prep_fn((pids[1],), *v), (B, 4), prep_ins(proj),
        [(dqn, hb(0)), (dkn, hb(0)), (dvn, hb(0)), (dbeta, hb(0)), (dg, hb(0))],
        [(0, _sds((T, W4)), hb(0), False), (1, _sds((T, W4)), hb(0), False), (2, _sds((T, W4)), hb(0), False),
         (3, _sds((T, LANES)), pl.BlockSpec((S, LANES), lambda b, h: (b, 0)), True),
         (4, _sds((B, 4, 1, W4)), dcw, False), (5, _sds((B, 4, 1, W4)), dcw, False),
         (6, _sds((B, 4, 1, W4)), dcw, False),
         (7, _sds((B, 1, LANES)), per_b, True), (8, _sds((B, 1, LANES)), per_b, True)], acc_axes=(1,))
    d_conv = jnp.concatenate([jnp.sum(d, axis=0)[:, 0, :] for d in (dcq, dck, dcv)], axis=-1)
    da, dpool_w, dpool_scale = _bwd_call(
        tag + "_pool_b", _pool_fn, (B,), pool_ins(proj), [(dy, seq4)],
        [(0, _sds((T, W4)), seq4, False), (1, _sds((4, LANES, LANES)), _whole((4, LANES, LANES), 1), True),
         (2, _sds((1, W4)), _whole((1, W4), 1), True)], acc_axes=(0,))
    dproj = jnp.concatenate([da, dq, dk, dv, dz, dbg], axis=-1)
    dhn = _mm(tag + "_in_dx", dproj, P["w_in"], tb=True)
    dw_in = _mm(tag + "_in_dw", hn, dproj, ta=True)
    dh_new, d_mix = _norm_bwd(tag + "_norm_b", h, P["mix_norm"], dhn, dh)
    grads = dict(w_in=dw_in, w_out=dw_out, mix_norm=d_mix[0], pool_w=dpool_w, pool_scale=dpool_scale[0],
                 conv_w=d_conv, a_log=jnp.sum(dalog, axis=0)[0, :4], dt_bias=jnp.sum(ddt, axis=0)[0, :4],
                 out_norm=d_onorm[0])
    return dh_new, grads


def _cd_specs(B, S, T, P):
    W4 = 4 * LANES
    R = _tile(T, SGU_ROWS, LANES)
    sgu_ins = lambda proj: [(proj, _rows(R, W4, 0)), (proj, _rows(R, W4, 1)), (P["sgu_g"], _whole((1, W4), 1)),
                            (P["sgu_b"], _whole((1, W4), 1)), (P["sgu_w"], _whole((4, LANES, LANES), 1)),
                            (P["bias_t"], _whole((LANES, LANES), 1))]
    jb = lambda off: pl.BlockSpec((S, LANES), lambda j, b: (b, off + j))
    sc_ins = lambda proj: [(proj, jb(8)), (proj, jb(12)), (proj, jb(16)),
                           (P["sc_w"], pl.BlockSpec((3, 1, LANES), lambda j, b: (0, 0, j)))]
    return R, sgu_ins, jb, sc_ins


def _cd_fwd(tag, h, P, B, S):
    T, D = h.shape
    W4 = 4 * LANES
    R, sgu_ins, jb, sc_ins = _cd_specs(B, S, T, P)
    hn = _norm_fwd(tag + "_norm", h, P["mix_norm"])
    proj = _mm(tag + "_in", hn, P["w_in"])
    yc = _fwd_call(tag + "_sgu", _sgu_fn, (T // R,), sgu_ins(proj), [(_sds((T, W4), BF16), _rows(R, W4), False)])[0]
    yd = _fwd_call(tag + "_sconv", _sconv_fn, (4, B), sc_ins(proj), [(_sds((T, W4), BF16), jb(0), False)])[0]
    y = jnp.concatenate([yc, yd], axis=-1)
    h_new = _mm(tag + "_out", y, P["w_out"], res=h)
    return h_new, (h, hn, proj, y)


def _cd_bwd(tag, dh, saved, P, B, S):
    h, hn, proj, y = saved
    T, D = h.shape
    W4 = 4 * LANES
    R, sgu_ins, jb, sc_ins = _cd_specs(B, S, T, P)
    dy = _mm(tag + "_out_dx", dh, P["w_out"], tb=True)
    dw_out = _mm(tag + "_out_dw", y, dh, ta=True)
    du, dv, dsg, dsb, dsw, dbias_t = _bwd_call(
        tag + "_sgu_b", _sgu_fn, (T // R,), sgu_ins(proj), [(dy, _rows(R, W4, 0))],
        [(0, _sds((T, W4)), _rows(R, W4), False), (1, _sds((T, W4)), _rows(R, W4), False),
         (2, _sds((1, W4)), _whole((1, W4), 1), True), (3, _sds((1, W4)), _whole((1, W4), 1), True),
         (4, _sds((4, LANES, LANES)), _whole((4, LANES, LANES), 1), True),
         (5, _sds((LANES, LANES)), _whole((LANES, LANES), 1), True)], acc_axes=(0,))
    dxd, dbgate, dcg, dsc = _bwd_call(
        tag + "_sconv_b", _sconv_fn, (4, B), sc_ins(proj), [(dy, jb(4))],
        [(0, _sds((T, W4)), jb(0), False), (1, _sds((T, W4)), jb(0), False), (2, _sds((T, W4)), jb(0), False),
         (3, _sds((3, 1, W4)), pl.BlockSpec((3, 1, LANES), lambda j, b: (0, 0, j)), True)], acc_axes=(1,))
    dproj = jnp.concatenate([du, dv, dxd, dbgate, dcg], axis=-1)
    dhn = _mm(tag + "_in_dx", dproj, P["w_in"], tb=True)
    dw_in = _mm(tag + "_in_dw", hn, dproj, ta=True)
    dh_new, d_mix = _norm_bwd(tag + "_norm_b", h, P["mix_norm"], dhn, dh)
    grads = dict(w_in=dw_in, w_out=dw_out, mix_norm=d_mix[0], sgu_g=dsg[0], sgu_b=dsb[0], sgu_w=dsw,
                 sgu_bias=dbias_t[:, :4].T, sc_w=dsc[:, 0, :])
    return dh_new, grads


def _loss_fwd_bwd(h, g, tgt):
    T, D = h.shape
    tr = _tile(T, ROW_TILE, 8)
    ins = [(h, _rows(tr, D)), (g, _whole((1, D), 1)), (tgt, _rows(tr, D))]
    vec = _whole((1, LANES), 1)
    loss = _fwd_call("loss", _loss_fn, (T // tr,), ins, [(_sds((1, LANES)), vec, True)], acc_axes=(0,))[0]
    one = jnp.zeros((1, LANES), F32).at[0, 0].set(1.0)
    dh, dg = _bwd_call("loss_b", _loss_fn, (T // tr,), ins, [(one, vec)],
                       [(0, _sds((T, D)), _rows(tr, D), False), (1, _sds((1, D)), _whole((1, D), 1), True)],
                       acc_axes=(0,))
    return loss[0, 0], dh, dg[0]


def _local_step(x2, tgt2, W, B, S):
    L = W["ffn1_norm"].shape[0]
    tm = _tile(x2.shape[0], FFN_TILE, 8)
    h = x2
    saved = []
    for l in range(L):
        e = l // 2
        f1 = (W["ffn1_norm"][l][None], W["ffn1_g"], W["ffn1_u"], W["ffn1_d"])
        f2 = (W["ffn2_norm"][l][None], W["ffn2_g"], W["ffn2_u"], W["ffn2_d"])
        h0 = h
        h, gate1, up1 = _ffn_fwd(f"l{l}_ffn1", h0, *f1, l, tm)
        if l % 2 == 0:
            P = dict(mix_norm=W["mix_norm"][l][None], w_in=W["ab_in"][e], w_out=W["ab_out"][e], pool_w=W["pool_w"][e],
                     pool_scale=W["pool_scale"][e][None], conv_w=W["dn_conv_w"][e][:, None, :],
                     a_log=jnp.pad(W["dn_a_log"][e][None], ((0, 0), (0, LANES - 4))),
                     dt_bias=jnp.pad(W["dn_dt_bias"][e][None], ((0, 0), (0, LANES - 4))),
                     out_norm=W["dn_out_norm"][e][None])
            h1 = h
            h, msave = _ab_fwd(f"l{l}_ab", h1, P, B, S)
        else:
            P = dict(mix_norm=W["mix_norm"][l][None], w_in=W["cd_in"][e], w_out=W["cd_out"][e],
                     sgu_g=W["sgu_norm_g"][e][None], sgu_b=W["sgu_norm_b"][e][None], sgu_w=W["sgu_w"][e],
                     bias_t=jnp.pad(W["sgu_bias"][e].T, ((0, 0), (0, LANES - 4))),
                     sc_w=W["sc_conv_w"][e][:, None, :])
            h1 = h
            h, msave = _cd_fwd(f"l{l}_cd", h1, P, B, S)
        h2 = h
        h, gate2, up2 = _ffn_fwd(f"l{l}_ffn2", h2, *f2, l, tm)
        saved.append((f1, f2, P, h0, gate1, up1, msave, h2, gate2, up2))

    loss, dh, d_final = _loss_fwd_bwd(h, W["final_norm"][None], tgt2)

    G = {k: [None] * L for k in ("ffn1_norm", "ffn1_g", "ffn1_u", "ffn1_d", "ffn2_norm", "ffn2_g", "ffn2_u", "ffn2_d",
                                  "mix_norm")}
    GA, GC = [None] * ((L + 1) // 2), [None] * (L // 2)
    for l in reversed(range(L)):
        f1, f2, P, h0, gate1, up1, msave, h2, gate2, up2 = saved[l]
        dh_in = dh
        dh, dgate, dup, dn2 = _ffn_bwd_x(f"l{l}_ffn2_bx", dh_in, h2, f2[0], gate2, up2, f2[1], f2[2], f2[3], l, tm)
        G["ffn2_g"][l], G["ffn2_u"][l], G["ffn2_d"][l] = _ffn_bwd_w(f"l{l}_ffn2_bw", dh_in, h2, f2[0], gate2, up2,
                                                                     dgate, dup, tm)
        G["ffn2_norm"][l] = dn2[0]
        if l % 2 == 0:
            dh, mg = _ab_bwd(f"l{l}_ab", dh, msave, P, B, S)
            GA[l // 2] = mg
        else:
            dh, mg = _cd_bwd(f"l{l}_cd", dh, msave, P, B, S)
            GC[l // 2] = mg
        G["mix_norm"][l] = mg["mix_norm"]
        dh_in = dh
        dh, dgate, dup, dn1 = _ffn_bwd_x(f"l{l}_ffn1_bx", dh_in, h0, f1[0], gate1, up1, f1[1], f1[2], f1[3], l, tm)
        G["ffn1_g"][l], G["ffn1_u"][l], G["ffn1_d"][l] = _ffn_bwd_w(f"l{l}_ffn1_bw", dh_in, h0, f1[0], gate1, up1,
                                                                     dgate, dup, tm)
        G["ffn1_norm"][l] = dn1[0]

    st = lambda xs: jnp.stack(xs, axis=0)
    big = dict(ffn1_w_gate=st(G["ffn1_g"]), ffn1_w_up=st(G["ffn1_u"]), ffn1_w_down=st(G["ffn1_d"]),
               ffn2_w_gate=st(G["ffn2_g"]), ffn2_w_up=st(G["ffn2_u"]), ffn2_w_down=st(G["ffn2_d"]),
               ab_w_in=st([m["w_in"] for m in GA]), ab_w_out=st([m["w_out"] for m in GA]),
               cd_w_in=st([m["w_in"] for m in GC]), cd_w_out=st([m["w_out"] for m in GC]))
    small = dict(ffn1_norm=st(G["ffn1_norm"]), mix_norm=st(G["mix_norm"]), ffn2_norm=st(G["ffn2_norm"]),
                 pool_w=st([m["pool_w"] for m in GA]), pool_scale=st([m["pool_scale"] for m in GA]),
                 dn_conv_w=st([m["conv_w"] for m in GA]), dn_a_log=st([m["a_log"] for m in GA]),
                 dn_dt_bias=st([m["dt_bias"] for m in GA]), dn_out_norm=st([m["out_norm"] for m in GA]),
                 sgu_norm_g=st([m["sgu_g"] for m in GC]), sgu_norm_b=st([m["sgu_b"] for m in GC]),
                 sgu_w=st([m["sgu_w"] for m in GC]), sgu_bias=st([m["sgu_bias"] for m in GC]),
                 sc_conv_w=st([m["sc_w"] for m in GC]), final_norm=d_final)
    return loss, dh, big, small


ANY = pl.BlockSpec(memory_space=pl.ANY)


def _place():
    return lax.axis_index("x"), lax.axis_index("y"), lax.axis_index("c")


def _exchange(name, srcs, out_shapes, plan, n_copies):
    n, m = len(srcs), len(out_shapes)

    def body(*refs):
        ins, outs = refs[:n], refs[n:n + m]
        send, recv = refs[n + m:]
        remote = plan(_place(), ins, outs)
        assert len(remote) == n_copies
        sends = []
        for k, (s, d, peer, _) in enumerate(remote):
            cp = pltpu.make_async_remote_copy(src_ref=s, dst_ref=d, send_sem=send.at[k], recv_sem=recv.at[k],
                                              device_id=peer, device_id_type=MESH)
            cp.start()
            sends.append(cp)
        for k, (s, _, peer, land) in enumerate(remote):
            pltpu.make_async_remote_copy(src_ref=s, dst_ref=land, send_sem=send.at[k], recv_sem=recv.at[k],
                                         device_id=peer, device_id_type=MESH).wait_recv()
        for cp in sends:
            cp.wait_send()

    return _pcall(body, in_specs=[ANY] * n, out_specs=[ANY] * m, out_shape=out_shapes,
                  scratch_shapes=[pltpu.SemaphoreType.DMA((n_copies,)), pltpu.SemaphoreType.DMA((n_copies,))],
                  name=name)(*srcs)


def _other_chips(x, y):
    return [(1 - x, y), (x, 1 - y), (1 - x, 1 - y)]


def _gather_chips(name, xs):
    n = len(xs)
    split = [a.ndim >= 3 and a.shape[0] % 2 == 0 for a in xs]
    n_fwd = sum(split)
    fwd_of = {t: j for j, t in enumerate(t for t in range(n) if split[t])}

    def body(*refs):
        ins, outs = refs[:n], refs[n:2 * n]
        send1, recv1, send2, recv2 = refs[2 * n:]
        x, y, c = _place()
        me = 2 * x + y
        chips = _other_chips(x, y)

        def part(t, cc):
            half = xs[t].shape[0] // 2
            return pl.ds(cc * half, half) if split[t] else pl.ds(0, xs[t].shape[0])

        def first(r, t, started):
            px, py = chips[r]
            dst = outs[t].at[me, part(t, c)] if started else outs[t].at[2 * px + py, part(t, c)]
            return pltpu.make_async_remote_copy(src_ref=ins[t].at[part(t, c)], dst_ref=dst, send_sem=send1.at[r, t],
                                                recv_sem=recv1.at[r, t], device_id=(px, py, c), device_id_type=MESH)

        def second(r, t, started):
            px, py = chips[r]
            rows = part(t, c) if started else part(t, 1 - c)
            blk = outs[t].at[2 * px + py, rows]
            return pltpu.make_async_remote_copy(src_ref=blk, dst_ref=blk, send_sem=send2.at[r, fwd_of[t]],
                                                recv_sem=recv2.at[r, fwd_of[t]], device_id=(x, y, 1 - c),
                                                device_id_type=MESH)

        sends = [first(r, t, True) for r in range(3) for t in range(n)]
        for cp in sends:
            cp.start()
        for r in range(3):
            for t in range(n):
                first(r, t, False).wait_recv()
                if split[t]:
                    cp = second(r, t, True)
                    cp.start()
                    sends.append(cp)
        for r in range(3):
            for t in range(n):
                if split[t]:
                    second(r, t, False).wait_recv()
        for cp in sends:
            cp.wait_send()

    return _pcall(body, in_specs=[ANY] * n, out_specs=[ANY] * n,
                  out_shape=[_sds((N_CHIPS,) + a.shape, a.dtype) for a in xs],
                  scratch_shapes=[pltpu.SemaphoreType.DMA((3, n)), pltpu.SemaphoreType.DMA((3, n)),
                                  pltpu.SemaphoreType.DMA((3, max(n_fwd, 1))), pltpu.SemaphoreType.DMA((3, max(n_fwd, 1)))],
                  name=name)(*xs)


def _pair_split(name, gs):
    n = len(gs)

    def plan(place, ins, outs):
        x, y, c = place
        return [(ins[t].at[:, :, 1 - c], outs[t], (x, y, 1 - c), outs[t]) for t in range(n)]

    return _exchange(name, gs, [_sds(a.shape[:2] + a.shape[3:], a.dtype) for a in gs], plan, n)


def _chip_scatter(name, ps):
    n = len(ps)

    def plan(place, ins, outs):
        x, y, c = place
        me = 2 * x + y
        return [(ins[t].at[:, 2 * px + py], outs[t].at[me], (px, py, c), outs[t].at[2 * px + py])
                for (px, py) in _other_chips(x, y) for t in range(n)]

    return _exchange(name, ps, [_sds((N_CHIPS, a.shape[0]) + a.shape[2:], a.dtype) for a in ps], plan, 3 * n)


def _pair_share(name, ts):
    n = len(ts)

    def plan(place, ins, outs):
        x, y, c = place
        return [(ins[t], outs[t], (x, y, 1 - c), outs[t]) for t in range(n)]

    return _exchange(name, ts, [_sds(a.shape, a.dtype) for a in ts], plan, n)


def _gather_devices(name, v):
    flips = [(fx, fy, fc) for fx in (0, 1) for fy in (0, 1) for fc in (0, 1)][1:]

    def plan(place, ins, outs):
        x, y, c = place
        me = 4 * x + 2 * y + c
        remote = []
        for fx, fy, fc in flips:
            px, py, pc = (1 - x if fx else x), (1 - y if fy else y), (1 - c if fc else c)
            remote.append((ins[0], outs[0].at[me], (px, py, pc), outs[0].at[4 * px + 2 * py + pc]))
        return remote

    return _exchange(name, [v], [_sds((8,) + v.shape, v.dtype)], plan, len(flips))[0]


def _sum_slots(name, a):
    k, rows, cols = a.shape
    tr = _tile(rows, ROW_TILE, 8)
    ins = [(a, pl.BlockSpec((None, tr, cols), lambda i, j=j: (j, i, 0))) for j in range(k)]
    return _fwd_call(name, lambda pids, *v: (functools.reduce(lambda p, q: p + q, v),), (rows // tr,), ins,
                     [(_sds((rows, cols)), _rows(tr, cols), False)])[0]


def _half_tile(ah):
    return _tile(ah, 256, 8)


def _pair_sum(name, g5, theirs, cf, out_dtype):
    L, P4, _, Ah, Bt = g5.shape
    ta = _half_tile(Ah)
    half = lambda hh: pl.BlockSpec((None, None, None, ta, Bt), lambda l, p, i: (l, p, hh, i, 0))
    blk = pl.BlockSpec((None, None, ta, Bt), lambda l, p, i: (l, p, i, 0))
    fn = lambda pids, g0, g1, r, c: (jnp.where(jnp.max(c) > 0.5, g1, g0) + r,)
    return _fwd_call(name, fn, (L, P4, Ah // ta), [(g5, half(0)), (g5, half(1)), (theirs, blk),
                                                   (cf, pl.BlockSpec((1, LANES), lambda l, p, i: (0, 0)))],
                     [(_sds((L, P4, Ah, Bt), out_dtype), blk, False)])[0]


def _adam_terms(w, g, m, v):
    m2 = ADAM_B1 * m + (1.0 - ADAM_B1) * g
    v2 = ADAM_B2 * v + (1.0 - ADAM_B2) * (g * g)
    m_hat = m2 / (1.0 - ADAM_B1 ** ADAM_STEP)
    v_hat = v2 / (1.0 - ADAM_B2 ** ADAM_STEP)
    return -ADAM_LR * (m_hat / (jnp.sqrt(v_hat) + ADAM_EPS) + ADAM_WD * w), m2, v2


def _adam_halves(name, w, m, v, mine, theirs, cf):
    L, Aa, Bt = w.shape
    Ah = Aa // 2
    ta = _half_tile(Ah)
    full = pl.BlockSpec((None, None, ta, Bt), lambda l, hh, i: (l, hh, i, 0))
    part = pl.BlockSpec((None, ta, Bt), lambda l, hh, i: (l, i, 0))

    def fn(pids, w_, m_, v_, a, b, c):
        g = jnp.where(jnp.max(c) == pids[1].astype(F32), a, b)
        return (g,) + _adam_terms(w_, g, m_, v_)

    ins = [(a.reshape(L, 2, Ah, Bt), full) for a in (w, m, v)] + [(mine, part), (theirs, part),
                                                                 (cf, pl.BlockSpec((1, LANES), lambda l, hh, i: (0, 0)))]
    outs = _fwd_call(name, fn, (L, 2, Ah // ta), ins, [(_sds((L, 2, Ah, Bt)), full, False)] * 4)
    return [o.reshape(w.shape) for o in outs]


def _adam_rows(name, w, g, m, v):
    rows, cols = w.shape
    tr = _tile(rows, ROW_TILE, 8)
    ins = [(a, _rows(tr, cols)) for a in (w, g, m, v)]
    return _fwd_call(name, lambda pids, *a: _adam_terms(*a), (rows // tr,), ins,
                     [(_sds((rows, cols)), _rows(tr, cols), False)] * 3)


def _pack(xs):
    flat = jnp.concatenate([a.reshape(-1).astype(F32) for a in xs])
    pad = (-flat.size) % (8 * LANES)
    return jnp.pad(flat, (0, pad)).reshape(-1, LANES)


def _unpack(buf, shapes):
    flat, out, off = buf.reshape(-1), [], 0
    for s in shapes:
        n = math.prod(s)
        out.append(flat[off:off + n].reshape(s))
        off += n
    return out


_WEIGHTS = ("ffn1_norm", "ffn1_w_gate", "ffn1_w_up", "ffn1_w_down", "mix_norm", "ffn2_norm", "ffn2_w_gate", "ffn2_w_up",
            "ffn2_w_down", "ab_w_in", "pool_w", "pool_scale", "dn_conv_w", "dn_a_log", "dn_dt_bias", "dn_out_norm",
            "ab_w_out", "cd_w_in", "sgu_norm_g", "sgu_norm_b", "sgu_w", "sgu_bias", "sc_conv_w", "cd_w_out", "final_norm")
_BIG = ("ffn1_w_gate", "ffn1_w_up", "ffn1_w_down", "ffn2_w_gate", "ffn2_w_up", "ffn2_w_down", "ab_w_in", "ab_w_out",
        "cd_w_in", "cd_w_out")
_ROW_SHARDED = ("ffn1_w_down", "ffn2_w_down", "ab_w_out", "cd_w_out")
_SMALL_SHARDED = ("dn_conv_w", "sgu_norm_g", "sgu_norm_b", "sc_conv_w")
_SMALL = tuple(n for n in _WEIGHTS if n not in _BIG)


def _to_slots(name, g):
    L, A, Bt = g.shape
    if name in _ROW_SHARDED:
        return g.reshape(L, N_CHIPS, A // N_CHIPS, Bt)
    return g.reshape(L, A, N_CHIPS, Bt // N_CHIPS).transpose(0, 2, 1, 3)


def kernel(x, ffn1_norm, ffn1_w_gate, ffn1_w_up, ffn1_w_down, mix_norm, ffn2_norm, ffn2_w_gate, ffn2_w_up, ffn2_w_down,
           ab_w_in, pool_w, pool_scale, dn_conv_w, dn_a_log, dn_dt_bias, dn_out_norm, ab_w_out, cd_w_in, sgu_norm_g,
           sgu_norm_b, sgu_w, sgu_bias, sc_conv_w, cd_w_out, final_norm, loss_target,
           m_ffn1_norm, m_ffn1_w_gate, m_ffn1_w_up, m_ffn1_w_down, m_mix_norm, m_ffn2_norm, m_ffn2_w_gate, m_ffn2_w_up,
           m_ffn2_w_down, m_ab_w_in, m_pool_w, m_pool_scale, m_dn_conv_w, m_dn_a_log, m_dn_dt_bias, m_dn_out_norm,
           m_ab_w_out, m_cd_w_in, m_sgu_norm_g, m_sgu_norm_b, m_sgu_w, m_sgu_bias, m_sc_conv_w, m_cd_w_out, m_final_norm,
           v_ffn1_norm, v_ffn1_w_gate, v_ffn1_w_up, v_ffn1_w_down, v_mix_norm, v_ffn2_norm, v_ffn2_w_gate, v_ffn2_w_up,
           v_ffn2_w_down, v_ab_w_in, v_pool_w, v_pool_scale, v_dn_conv_w, v_dn_a_log, v_dn_dt_bias, v_dn_out_norm,
           v_ab_w_out, v_cd_w_in, v_sgu_norm_g, v_sgu_norm_b, v_sgu_w, v_sgu_bias, v_sc_conv_w, v_cd_w_out, v_final_norm):
    A = dict(locals())
    B, S, D = x.shape
    T = B * S
    xi, yi, ci = _place()
    chip = 2 * xi + yi
    cf = jnp.broadcast_to(ci.astype(F32), (1, LANES))
    own = lambda buf, mine, slot: lax.dynamic_update_index_in_dim(buf, mine, slot, 0)

    sh_shapes = [A[n].shape for n in _SMALL_SHARDED]
    local = [A[n].astype(BF16) for n in _BIG] + [_pack([A[n] for n in _SMALL_SHARDED])]
    gathered = [own(g, a, chip) for g, a in zip(_gather_chips("gather_weights", local), local)]
    gw = dict(zip(_BIG, gathered[:-1]))
    per_chip = [_unpack(gathered[-1][p], sh_shapes) for p in range(N_CHIPS)]
    W = {n: A[n] for n in _SMALL if n not in _SMALL_SHARDED}
    for j, n in enumerate(_SMALL_SHARDED):
        W[n] = jnp.concatenate([per_chip[p][j] for p in range(N_CHIPS)], axis=-1)
    for f in ("ffn1", "ffn2"):
        W[f + "_g"], W[f + "_u"], W[f + "_d"] = gw[f + "_w_gate"], gw[f + "_w_up"], gw[f + "_w_down"]
    cols = lambda g: jnp.concatenate([g[p] for p in range(N_CHIPS)], axis=-1)
    rows = lambda g: jnp.concatenate([g[p] for p in range(N_CHIPS)], axis=1)
    ab_in = cols(gw["ab_w_in"])
    ab_cols = ab_in.shape[-1]
    ab_pad = (-ab_cols) % LANES
    W["ab_in"] = jnp.pad(ab_in, ((0, 0), (0, 0), (0, ab_pad)))
    W["cd_in"], W["ab_out"], W["cd_out"] = cols(gw["cd_w_in"]), rows(gw["ab_w_out"]), rows(gw["cd_w_out"])

    loss, dx, big, small = _local_step(x.reshape(T, D), loss_target.reshape(T, D), W, B, S)
    loss = lax.psum(loss, ("x", "y", "c"))

    big["ab_w_in"] = big["ab_w_in"][:, :, :ab_cols]
    slots = []
    for n in _BIG:
        g = big[n] if big[n].ndim == 4 else _to_slots(n, big[n])
        L, P4, Aa, Bt = g.shape
        slots.append(g.reshape(L, P4, 2, Aa // 2, Bt))
    theirs = _pair_split("grads_pair_split", slots)
    pair = [_pair_sum(f"grads_pair_sum_{n}", g5, r, cf, BF16) for n, g5, r in zip(_BIG, slots, theirs)]
    parts = [own(buf, lax.dynamic_index_in_dim(p, chip, 1, keepdims=False), chip)
             for buf, p in zip(_chip_scatter("grads_chip_scatter", pair), pair)]
    tot = [_sum_slots(f"grads_chip_sum_{n}", p.reshape(N_CHIPS, -1, p.shape[-1])).reshape(p.shape[1:])
           for n, p in zip(_BIG, parts)]
    other = _pair_share("grads_pair_share", tot)

    sm_shapes = [small[n].shape for n in _SMALL]
    sm_local = _pack([small[n] for n in _SMALL])
    every = own(_gather_devices("gather_small_grads", sm_local), sm_local, 2 * chip + ci)
    grads = {}
    for n, g in zip(_SMALL, _unpack(_sum_slots("small_grads_sum", every), sm_shapes)):
        if n in _SMALL_SHARDED:
            w_loc = A[n].shape[-1]
            g = lax.dynamic_slice_in_dim(g, chip * w_loc, w_loc, axis=g.ndim - 1)
        grads[n] = g.reshape(A[n].shape)

    delta, new_m, new_v = {}, {}, {}
    for n, a, b in zip(_BIG, tot, other):
        grads[n], delta[n], new_m[n], new_v[n] = _adam_halves(f"adam_{n}", A[n], A["m_" + n], A["v_" + n], a, b, cf)
    packed = [_pack([d[n] for n in _SMALL]) for d in (A, grads, {n: A["m_" + n] for n in _SMALL},
                                                       {n: A["v_" + n] for n in _SMALL})]
    loc_shapes = [A[n].shape for n in _SMALL]
    for d, buf in zip((delta, new_m, new_v), _adam_rows("adam_small", *packed)):
        d.update(zip(_SMALL, _unpack(buf, loc_shapes)))

    return (loss, dx.reshape(B, S, D), *[grads[n] for n in _WEIGHTS], *[delta[n] for n in _WEIGHTS],
            *[new_m[n] for n in _WEIGHTS], *[new_v[n] for n in _WEIGHTS])
```

```python
import functools
import math

import jax
import jax.numpy as jnp
from jax import lax
from jax.experimental import pallas as pl
from jax.experimental.pallas import tpu as pltpu

F32, BF16 = jnp.float32, jnp.bfloat16
HIGHEST = lax.Precision.HIGHEST
MESH = pl.DeviceIdType.MESH

EPS = 1e-6
LANES = 128
CH = 64
PAIR = 2 * CH
POOL_WINDOWS = (2, 4, 8, 16)
N_CHIPS = 4
ADAM_LR, ADAM_B1, ADAM_B2, ADAM_EPS, ADAM_WD, ADAM_STEP = 0.001, 0.9, 0.999, 1e-08, 0.01, 10
VMEM_LIMIT = 56 * 1024 * 1024


def _pcall(body, **kw):
    return pl.pallas_call(body, **kw)


def _params(nd):
    return pltpu.CompilerParams(dimension_semantics=("arbitrary",) * nd, vmem_limit_bytes=VMEM_LIMIT)


def _sds(shape, dtype=F32):
    return jax.ShapeDtypeStruct(tuple(shape), dtype)


def _tile(n, cap, mult=LANES):
    if n <= cap:
        return n
    best = None
    for t in range(mult, cap + 1, mult):
        if n % t == 0:
            best = t
    assert best is not None, (n, cap)
    return best


def _dg(a, b, ca, cb):
    return lax.dot_general(a.astype(BF16), b.astype(BF16), (((ca,), (cb,)), ((), ())),
                           preferred_element_type=F32)


@jax.custom_vjp
def _bdot(a, b):
    return _dg(a, b, 1, 0)


def _bdot_fwd(a, b):
    return _dg(a, b, 1, 0), (a, b)


def _bdot_bwd(res, g):
    a, b = res
    return _dg(g, b, 1, 1), _dg(a, g, 0, 0)


_bdot.defvjp(_bdot_fwd, _bdot_bwd)


@jax.custom_vjp
def _bdot_nt(a, b):
    return _dg(a, b, 1, 1)


def _bdot_nt_fwd(a, b):
    return _dg(a, b, 1, 1), (a, b)


def _bdot_nt_bwd(res, g):
    a, b = res
    return _dg(g, b, 1, 0), _dg(g, a, 0, 0)


_bdot_nt.defvjp(_bdot_nt_fwd, _bdot_nt_bwd)


def _hdot(a, b):
    return jnp.dot(a, b, precision=HIGHEST, preferred_element_type=F32)


def _shift_raw(x, k):
    n = x.shape[0]
    rows = lax.broadcasted_iota(jnp.int32, x.shape, 0)
    r = pltpu.roll(x, k % n, 0)
    if k > 0:
        return jnp.where(rows >= k, r, 0.0)
    return jnp.where(rows < n + k, r, 0.0)


@functools.partial(jax.custom_vjp, nondiff_argnums=(1,))
def _shift(x, k):
    return _shift_raw(x, k)


def _shift_fwd(x, k):
    return _shift_raw(x, k), None


def _shift_bwd(k, _, g):
    return (_shift(g, -k),)


_shift.defvjp(_shift_fwd, _shift_bwd)


def _col(x, j):
    lanes = lax.broadcasted_iota(jnp.int32, x.shape, x.ndim - 1)
    return jnp.sum(jnp.where(lanes == j, x, 0.0), axis=-1, keepdims=True)


def _rms(x, g):
    return x * lax.rsqrt(jnp.mean(x * x, axis=-1, keepdims=True) + EPS) * g


def _sigmoid(x):
    return 1.0 / (1.0 + jnp.exp(-x))


def _silu(x):
    return x * _sigmoid(x)


def _sigmoid_fast(x):
    return pl.reciprocal(1.0 + jnp.exp(-x), approx=True)


def _softplus(x):
    return jnp.maximum(x, 0.0) + jnp.log(1.0 + jnp.exp(-jnp.abs(x)))


def _gelu(x):
    c = math.sqrt(2.0 / math.pi)
    return 0.5 * x * (1.0 + jnp.tanh(c * (x + 0.044715 * (x * x * x))))


def _first(axes):
    c = None
    for a in axes:
        t = pl.program_id(a) == 0
        c = t if c is None else jnp.logical_and(c, t)
    return c


def _fwd_call(name, fn, grid, ins, outs, acc_axes=()):
    nin = len(ins)

    def body(*refs):
        pids = tuple(pl.program_id(a) for a in range(len(grid)))
        first = _first(acc_axes) if acc_axes else None
        vals = [r[...].astype(F32) for r in refs[:nin]]
        res = fn(pids, *vals)
        for r, o, (_, _, acc) in zip(refs[nin:], res, outs):
            if acc:
                @pl.when(first)
                def _(r=r):
                    r[...] = jnp.zeros_like(r)
                r[...] += o.astype(r.dtype)
            else:
                r[...] = o.astype(r.dtype)

    return _pcall(body, grid=grid, in_specs=[s for _, s in ins], out_specs=[s for _, s, _ in outs],
                  out_shape=[s for s, _, _ in outs], name=name,
                  compiler_params=_params(len(grid)))(*[a for a, _ in ins])


def _bwd_call(name, fn, grid, ins, cts, gouts, acc_axes=(), addends=()):
    nin, nct, nadd = len(ins), len(cts), len(addends)

    def body(*refs):
        pids = tuple(pl.program_id(a) for a in range(len(grid)))
        first = _first(acc_axes) if acc_axes else None
        vals = [r[...].astype(F32) for r in refs[:nin]]
        ctv = tuple(r[...].astype(F32) for r in refs[nin:nin + nct])
        addv = {pos: refs[nin + nct + n][...].astype(F32) for n, (pos, _, _) in enumerate(addends)}
        _, vjp = jax.vjp(lambda *v: tuple(fn(pids, *v)), *vals)
        grads = vjp(ctv)
        for pos, (r, (idx, _, _, acc)) in enumerate(zip(refs[nin + nct + nadd:], gouts)):
            gval = grads[idx]
            if pos in addv:
                gval = gval + addv[pos]
            if acc:
                @pl.when(first)
                def _(r=r):
                    r[...] = jnp.zeros_like(r)
                r[...] += gval.astype(r.dtype)
            else:
                r[...] = gval.astype(r.dtype)

    args = [a for a, _ in ins] + [a for a, _ in cts] + [a for _, a, _ in addends]
    specs = [s for _, s in ins] + [s for _, s in cts] + [s for _, _, s in addends]
    return _pcall(body, grid=grid, in_specs=specs, out_specs=[s for _, _, s, _ in gouts],
                  out_shape=[s for _, s, _, _ in gouts], name=name,
                  compiler_params=_params(len(grid)))(*args)


def _mm(name, a, b, *, ta=False, tb=False, out_dtype=F32, res=None, scale=1.0, tm_cap=512, tn_cap=1024,
        tk_cap=1024):
    if ta:
        K, M = a.shape
    else:
        M, K = a.shape
    N = b.shape[0] if tb else b.shape[1]
    tm, tn, tk = _tile(M, tm_cap), _tile(N, tn_cap), _tile(K, tk_cap)
    nk = K // tk
    a_spec = pl.BlockSpec((tk, tm), lambda i, j, k: (k, i)) if ta else pl.BlockSpec((tm, tk), lambda i, j, k: (i, k))
    b_spec = pl.BlockSpec((tn, tk), lambda i, j, k: (j, k)) if tb else pl.BlockSpec((tk, tn), lambda i, j, k: (k, j))
    o_spec = pl.BlockSpec((tm, tn), lambda i, j, k: (i, j))
    ca, cb = (0 if ta else 1), (1 if tb else 0)
    has_res = res is not None

    def body(*refs):
        a_ref, b_ref = refs[0], refs[1]
        r_ref = refs[2] if has_res else None
        o_ref, acc = refs[-2], refs[-1]
        k = pl.program_id(2)

        @pl.when(k == 0)
        def _():
            acc[...] = jnp.zeros_like(acc)

        acc[...] += _dg(a_ref[...], b_ref[...], ca, cb)

        @pl.when(k == nk - 1)
        def _():
            val = acc[...] * scale if scale != 1.0 else acc[...]
            if has_res:
                val = r_ref[...].astype(F32) + val
            o_ref[...] = val.astype(o_ref.dtype)

    args, specs = [a, b], [a_spec, b_spec]
    if has_res:
        args.append(res)
        specs.append(o_spec)
    return _pcall(body, grid=(M // tm, N // tn, nk), in_specs=specs, out_specs=o_spec,
                  out_shape=_sds((M, N), out_dtype), scratch_shapes=[pltpu.VMEM((tm, tn), F32)], name=name,
                  compiler_params=_params(3))(*args)


def _ffn_fwd(name, h, g, wg, wu, wd, l, tm):
    T, D = h.shape
    ns, _, _, Fs = wg.shape

    def body(h_ref, g_ref, wg_ref, wu_ref, wd_ref, ho_ref, gate_ref, up_ref, hn_s, acc_s):
        s = pl.program_id(1)

        @pl.when(s == 0)
        def _():
            hn_s[...] = _rms(h_ref[...], g_ref[...]).astype(BF16)
            acc_s[...] = jnp.zeros_like(acc_s)

        hn = hn_s[...]
        gate = jnp.dot(hn, wg_ref[...], preferred_element_type=F32)
        up = jnp.dot(hn, wu_ref[...], preferred_element_type=F32)
        gate_ref[...] = gate.astype(BF16)
        up_ref[...] = up.astype(BF16)
        act = (gate * _sigmoid_fast(gate) * up).astype(BF16)
        acc_s[...] += jnp.dot(act, wd_ref[...], preferred_element_type=F32)

        @pl.when(s == ns - 1)
        def _():
            ho_ref[...] = h_ref[...] + 0.5 * acc_s[...]

    row = pl.BlockSpec((tm, D), lambda i, s: (i, 0))
    wcol = pl.BlockSpec((None, None, D, Fs), lambda i, s: (s, l, 0, 0))
    wrow = pl.BlockSpec((None, None, Fs, D), lambda i, s: (s, l, 0, 0))
    slot = pl.BlockSpec((None, tm, Fs), lambda i, s: (s, i, 0))
    return _pcall(body, grid=(T // tm, ns), in_specs=[row, pl.BlockSpec((1, D), lambda i, s: (0, 0)), wcol, wcol, wrow],
                  out_specs=[row, slot, slot],
                  out_shape=[_sds((T, D)), _sds((ns, T, Fs), BF16), _sds((ns, T, Fs), BF16)],
                  scratch_shapes=[pltpu.VMEM((tm, D), BF16), pltpu.VMEM((tm, D), F32)], name=name,
                  compiler_params=_params(2))(h, g, wg, wu, wd)


def _ffn_bwd_x(name, dh, h, g, gate, up, wg, wu, wd, l, tm):
    T, D = h.shape
    ns, _, _, Fs = wg.shape

    def body(dh_ref, h_ref, g_ref, gate_ref, up_ref, wg_ref, wu_ref, wd_ref,
             dho_ref, dgate_ref, dup_ref, dgain_ref, do_s, acc_s):
        i, s = pl.program_id(0), pl.program_id(1)

        @pl.when(s == 0)
        def _():
            do_s[...] = (0.5 * dh_ref[...]).astype(BF16)
            acc_s[...] = jnp.zeros_like(acc_s)

        @pl.when(jnp.logical_and(i == 0, s == 0))
        def _():
            dgain_ref[...] = jnp.zeros_like(dgain_ref)

        dact = _dg(do_s[...], wd_ref[...], 1, 1)
        gt, u = gate_ref[...].astype(F32), up_ref[...].astype(F32)
        sg = _sigmoid_fast(gt)
        dgt =(dact * u * (sg * (1.0 + gt * (1.0 - sg)))).astype(BF16)
        du = (dact * (gt * sg)).astype(BF16)
        dgate_ref[...] = dgt
        dup_ref[...] = du
        acc_s[...] += _dg(dgt, wg_ref[...], 1, 1) + _dg(du, wu_ref[...], 1, 1)

        @pl.when(s == ns - 1)
        def _():
            _, vjp = jax.vjp(_rms, h_ref[...], g_ref[...])
            dx, dgain = vjp(acc_s[...])
            dho_ref[...] = dh_ref[...] + dx
            dgain_ref[...] += dgain

    row = pl.BlockSpec((tm, D), lambda i, s: (i, 0))
    gain = pl.BlockSpec((1, D), lambda i, s: (0, 0))
    wcol = pl.BlockSpec((None, None, D, Fs), lambda i, s: (s, l, 0, 0))
    wrow = pl.BlockSpec((None, None, Fs, D), lambda i, s: (s, l, 0, 0))
    slot = pl.BlockSpec((None, tm, Fs), lambda i, s: (s, i, 0))
    return _pcall(body, grid=(T // tm, ns), in_specs=[row, row, gain, slot, slot, wcol, wcol, wrow],
                  out_specs=[row, slot, slot, gain],
                  out_shape=[_sds((T, D)), _sds((ns, T, Fs), BF16), _sds((ns, T, Fs), BF16), _sds((1, D))],
                  scratch_shapes=[pltpu.VMEM((tm, D), BF16), pltpu.VMEM((tm, D), F32)], name=name,
                  compiler_params=_params(2))(dh, h, g, gate, up, wg, wu, wd)


def _ffn_bwd_w(name, dh, h, g, gate, up, dgate, dup, bufs, l, nl, tm):
    T, D = h.shape
    ns, _, Fs = gate.shape

    def body(dh_ref, h_ref, g_ref, gate_ref, up_ref, dgate_ref, dup_ref, dwg_ref, dwu_ref, dwd_ref):
        @pl.when(pl.program_id(1) == 0)
        def _():
            dwg_ref[...] = jnp.zeros_like(dwg_ref)
            dwu_ref[...] = jnp.zeros_like(dwu_ref)
            dwd_ref[...] = jnp.zeros_like(dwd_ref)

        hn = _rms(h_ref[...], g_ref[...]).astype(BF16)
        gt = gate_ref[...].astype(F32)
        act = (gt * _sigmoid_fast(gt) * up_ref[...].astype(F32)).astype(BF16)
        do = (0.5 * dh_ref[...]).astype(BF16)
        dwg_ref[...] += _dg(hn, dgate_ref[...], 0, 0)
        dwu_ref[...] += _dg(hn, dup_ref[...], 0, 0)
        dwd_ref[...] += _dg(act, do, 0, 0)

    row = pl.BlockSpec((tm, D), lambda s, j: (j, 0))
    gain = pl.BlockSpec((1, D), lambda s, j: (0, 0))
    slot = pl.BlockSpec((None, tm, Fs), lambda s, j: (s, j, 0))
    wcol = pl.BlockSpec((None, None, D, Fs), lambda s, j: (l, s, 0, 0))
    wrow = pl.BlockSpec((None, None, Fs, D), lambda s, j: (l, s, 0, 0))
    args, specs, alias = [dh, h, g, gate, up, dgate, dup], [row, row, gain, slot, slot, slot, slot], {}
    if bufs is not None:
        alias = {len(args) + k: k for k in range(3)}
        args, specs = args + list(bufs), specs + [pl.BlockSpec(memory_space=pl.ANY)] * 3
    return _pcall(lambda *refs: body(*refs[:7], *refs[-3:]), grid=(ns, T // tm), in_specs=specs,
                  out_specs=[wcol, wcol, wrow], input_output_aliases=alias,
                  out_shape=[_sds((nl, ns, D, Fs)), _sds((nl, ns, D, Fs)), _sds((nl, ns, Fs, D))], name=name,
                  compiler_params=_params(2))(*args)


def _rms_fn(pids, x, g):
    return (_rms(x, g),)


def _pool_fn(pids, a, w, scale):
    rows = lax.broadcasted_iota(jnp.int32, (a.shape[0], LANES), 0)
    outs = []
    for gi, win in enumerate(POOL_WINDOWS):
        ag = a[:, gi * LANES:(gi + 1) * LANES]
        s, k = ag, 1
        while k < win:
            s = s + _shift(s, k)
            k *= 2
        cnt = jnp.minimum(rows + 1, win).astype(F32)
        pooled = s / cnt - ag
        outs.append(_bdot(pooled, w[gi]) * scale[:, gi * LANES:(gi + 1) * LANES])
    return (jnp.concatenate(outs, axis=-1),)


def _conv_taps(x, cw):
    K = cw.shape[0]
    y = cw[K - 1] * x
    for j in range(K - 1):
        y = y + cw[j] * _shift(x, K - 1 - j)
    return y


def _prep_fn(pids, q, k, v, bg, cwq, cwk, cwv, alog, dtb):
    hd = pids[0]
    q, k, v = _silu(_conv_taps(q, cwq)), _silu(_conv_taps(k, cwk)), _silu(_conv_taps(v, cwv))
    q = q * lax.rsqrt(jnp.sum(q * q, axis=-1, keepdims=True) + EPS) * (LANES ** -0.5)
    k = k * lax.rsqrt(jnp.sum(k * k, axis=-1, keepdims=True) + EPS)
    beta = _sigmoid(_col(bg, hd))
    g = -jnp.exp(_col(alog, hd)) * _softplus(_col(bg, hd + 4) + _col(dtb, hd))
    return q, k, v, jnp.broadcast_to(beta, q.shape), jnp.broadcast_to(g, q.shape)


def _post_fn(pids, o, z, onorm):
    return (_rms(o, onorm) * _silu(z),)


def _sgu_fn(pids, u, v, g, b, ws, bias_t):
    u, v = _gelu(u), _gelu(v)
    mu = jnp.mean(v, axis=-1, keepdims=True)
    xc = v - mu
    vn = xc * lax.rsqrt(jnp.mean(xc * xc, axis=-1, keepdims=True) + EPS) * g + b
    r = lax.broadcasted_iota(jnp.int32, (LANES, LANES), 0)
    c = lax.broadcasted_iota(jnp.int32, (LANES, LANES), 1)
    rows = []
    for n in range(u.shape[0] // LANES):
        heads = []
        for hd in range(4):
            wm = jnp.where(r >= c, ws[hd], 0.0)
            blk = vn[n * LANES:(n + 1) * LANES, hd * LANES:(hd + 1) * LANES]
            heads.append(_bdot(wm, blk) + _col(bias_t, hd))
        rows.append(jnp.concatenate(heads, axis=-1))
    mixed = jnp.concatenate(rows, axis=0) if len(rows) > 1 else rows[0]
    return (u * mixed,)


def _sconv_fn(pids, xd, bgate, cg, cw):
    return (bgate * _conv_taps(cg * xd, cw),)


def _loss_fn(pids, h, g, tgt):
    err = _rms(h, g) - tgt
    tot = 0.5 * jnp.sum(jnp.mean(err * err, axis=-1, keepdims=True), axis=0, keepdims=True)
    return (jnp.broadcast_to(tot, (1, LANES)),)


DELTA_PAIRS = 2
DELTA_ROWS = DELTA_PAIRS * PAIR


def _each(fn, *lists):
    return [fn(*args) for args in zip(*lists)]


def _chunk_cumsum(g):
    pos = jnp.bitwise_and(lax.broadcasted_iota(jnp.int32, g.shape, 0), CH - 1)
    s, k = g, 1
    while k < CH:
        s = s + jnp.where(pos >= k, _shift(s, k), 0.0)
        k *= 2
    tot = [jnp.broadcast_to(jnp.sum(g[j * CH:(j + 1) * CH], axis=0, keepdims=True), (CH, g.shape[1])) for j in (0, 1)]
    return s, jnp.concatenate(tot, axis=0)


def _stage_a(blocks):
    q, k, v, beta, g = (list(x) for x in zip(*blocks))
    r = lax.broadcasted_iota(jnp.int32, (PAIR, PAIR), 0)
    c = lax.broadcasted_iota(jnp.int32, (PAIR, PAIR), 1)
    same = jnp.right_shift(r, 6) == jnp.right_shift(c, 6)
    tri = jnp.logical_and(same, r >= c)
    strict = jnp.logical_and(same, r > c)
    eye = (r == c).astype(F32)
    sums = _each(_chunk_cumsum, g)
    gc, gt = [a for a, _ in sums], [b for _, b in sums]
    gamma = _each(lambda x: jnp.exp(jnp.where(tri, x - x.T, -jnp.inf)), gc)
    kb = _each(lambda a, b: a * b, k, beta)
    kk = _each(_bdot_nt, kb, k)
    p = _each(lambda a, gm: -jnp.where(strict, a * gm, 0.0), kk, gamma)
    tinv = _each(lambda x: eye + x, p)
    for _ in range(5):
        p = _each(_bdot, p, p)
        tinv = _each(lambda t, x: t + x, tinv, _each(_bdot, tinv, p))
    egc = _each(jnp.exp, gc)
    u = _each(_bdot, tinv, _each(lambda a, b: a * b, v, beta))
    w = _each(_bdot, tinv, _each(lambda a, e: a * e, kb, egc))
    a = _each(lambda x, gm: x * gm, _each(_bdot_nt, q, k), gamma)
    qd = _each(lambda x, e: x * e, q, egc)
    kd = _each(lambda x, t, s: x * jnp.exp(t - s), k, gt, gc)
    return list(zip(u, w, qd, a, kd, _each(jnp.exp, gt)))


def _nn(a, b):
    return _dg(a, b, 1, 0)


def _nt(a, b):
    return _dg(a, b, 1, 1)


def _tn(a, b):
    return _dg(a, b, 0, 0)


def _scan_pair_fwd(s0, u, w, qd, a, kd, l0, l1):
    lo = lambda xs: [x[:CH] for x in xs]
    hi = lambda xs: [x[CH:] for x in xs]
    vn0 = _each(lambda x, y: x - y, lo(u), _each(_nn, lo(w), s0))
    s1 = _each(lambda s, l, x: s * l + x, s0, l0, _each(_tn, lo(kd), vn0))
    vn1 = _each(lambda x, y: x - y, hi(u), _each(_nn, hi(w), s1))
    s2 = _each(lambda s, l, x: s * l + x, s1, l1, _each(_tn, hi(kd), vn1))
    inter = _each(lambda x, y: jnp.concatenate([x, y], axis=0), _each(_nn, lo(qd), s0), _each(_nn, hi(qd), s1))
    intra = _each(_nn, a, _each(lambda x, y: jnp.concatenate([x, y], axis=0), vn0, vn1))
    return _each(lambda x, y: x + y, inter, intra), s1, s2


def _scan_chunk_bwd(s, ds_next, do, dvn_o, u, w, qd, kd, lrow):
    vn = _each(lambda x, y: x - y, u, _each(_nn, w, s))
    dvn = _each(lambda x, y: x + y, dvn_o, _each(_nn, kd, ds_next))
    dkd = _each(_nt, vn, ds_next)
    dl = _each(lambda a, b: jnp.sum(a * b, axis=0, keepdims=True), ds_next, s)
    dw = _each(lambda x: -x, _each(_nt, dvn, s))
    dqd = _each(_nt, do, s)
    ds = _each(lambda x, l, d, y: x + l * d - y, _each(_tn, qd, do), lrow, ds_next, _each(_tn, w, dvn))
    return ds, vn, dvn, dw, dqd, dkd, dl


def _delta_blocks(refs):
    return [tuple(r[p * PAIR:(p + 1) * PAIR, h * LANES:(h + 1) * LANES] for r in refs)
            for p in range(DELTA_PAIRS) for h in range(4)]


def _delta_specs(B, S, reverse):
    nstep = S // DELTA_ROWS
    at = (lambda i: nstep - 1 - i) if reverse else (lambda i: i)
    blk = pl.BlockSpec((DELTA_ROWS, 4 * LANES), lambda b, i: (b * nstep + at(i), 0))
    st = pl.BlockSpec((None, 4, 2 * DELTA_PAIRS, LANES, LANES), lambda b, i: (b, 0, at(i), 0, 0))
    scratch = [pltpu.VMEM((4, LANES, LANES), F32), pltpu.VMEM((4 * DELTA_PAIRS, PAIR, LANES), F32)]
    return nstep, blk, st, scratch


def _delta_fwd(name, qn, kn, vn, beta, g, B, S):
    T = qn.shape[0]
    nstep, blk, st, scratch = _delta_specs(B, S, False)

    def body(q_ref, k_ref, v_ref, b_ref, g_ref, o_ref, st_ref, s_s, l_s):
        @pl.when(pl.program_id(1) == 0)
        def _():
            s_s[...] = jnp.zeros_like(s_s)

        outs = _stage_a(_delta_blocks((q_ref, k_ref, v_ref, b_ref, g_ref)))
        for j, blk_out in enumerate(outs):
            l_s[j] = blk_out[5]
        state = [s_s[h] for h in range(4)]
        for p in range(DELTA_PAIRS):
            u, w, qd, a, kd, _ = (list(x) for x in zip(*outs[4 * p:4 * p + 4]))
            l0 = [l_s[4 * p + h, 0:1, :] for h in range(4)]
            l1 = [l_s[4 * p + h, CH:CH + 1, :] for h in range(4)]
            o, mid, end = _scan_pair_fwd(state, u, w, qd, a, kd, l0, l1)
            for h in range(4):
                o_ref[p * PAIR:(p + 1) * PAIR, h * LANES:(h + 1) * LANES] = o[h]
                st_ref[h, 2 * p] = state[h]
                st_ref[h, 2 * p + 1] = mid[h]
            state = end
        for h in range(4):
            s_s[h] = state[h]

    return _pcall(body, grid=(B, nstep), in_specs=[blk] * 5, out_specs=[blk, st],
                  out_shape=[_sds((T, 4 * LANES)), _sds((B, 4, S // CH, LANES, LANES))],
                  scratch_shapes=scratch, name=name, compiler_params=_params(2))(qn, kn, vn, beta, g)


def _delta_bwd(name, qn, kn, vn, beta, g, states, do, B, S):
    T = qn.shape[0]
    nstep, blk, st, scratch = _delta_specs(B, S, True)

    def body(q_ref, k_ref, v_ref, b_ref, g_ref, st_ref, do_ref, dq_ref, dk_ref, dv_ref, db_ref, dg_ref, ds_s, l_s):
        @pl.when(pl.program_id(1) == 0)
        def _():
            ds_s[...] = jnp.zeros_like(ds_s)

        rowid = lax.broadcasted_iota(jnp.int32, (PAIR, LANES), 0)
        lo = lambda xs: [x[:CH] for x in xs]
        hi = lambda xs: [x[CH:] for x in xs]
        cat = lambda xs, ys: _each(lambda x, y: jnp.concatenate([x, y], axis=0), xs, ys)
        outs, vjp = jax.vjp(_stage_a, _delta_blocks((q_ref, k_ref, v_ref, b_ref, g_ref)))
        for j, blk_out in enumerate(outs):
            l_s[j] = blk_out[5]
        ds = [ds_s[h] for h in range(4)]
        cts = [None] * (4 * DELTA_PAIRS)
        for p in reversed(range(DELTA_PAIRS)):
            u, w, qd, a, kd, _ = (list(x) for x in zip(*outs[4 * p:4 * p + 4]))
            l0 = [l_s[4 * p + h, 0:1, :] for h in range(4)]
            l1 = [l_s[4 * p + h, CH:CH + 1, :] for h in range(4)]
            s0 = [st_ref[h, 2 * p] for h in range(4)]
            s1 = [st_ref[h, 2 * p + 1] for h in range(4)]
            dout = [do_ref[p * PAIR:(p + 1) * PAIR, h * LANES:(h + 1) * LANES] for h in range(4)]
            dvn_o = _each(_tn, a, dout)
            ds1, vn1, dvn1, dw1, dqd1, dkd1, dl1 = _scan_chunk_bwd(s1, ds, hi(dout), hi(dvn_o), hi(u), hi(w), hi(qd),
                                                                   hi(kd), l1)
            ds, vn0, dvn0, dw0, dqd0, dkd0, dl0 = _scan_chunk_bwd(s0, ds1, lo(dout), lo(dvn_o), lo(u), lo(w), lo(qd),
                                                                  lo(kd), l0)
            da = _each(_nt, dout, cat(vn0, vn1))
            dl = _each(lambda x, y: jnp.where(rowid == 0, x, jnp.where(rowid == CH, y, 0.0)), dl0, dl1)
            for h, ct in enumerate(zip(cat(dvn0, dvn1), cat(dw0, dw1), cat(dqd0, dqd1), da, cat(dkd0, dkd1), dl)):
                cts[4 * p + h] = ct
        for h in range(4):
            ds_s[h] = ds[h]
        (grads,) = vjp(cts)
        for j, blk_grads in enumerate(grads):
            p, h = divmod(j, 4)
            for ref, val in zip((dq_ref, dk_ref, dv_ref, db_ref, dg_ref), blk_grads):
                ref[p * PAIR:(p + 1) * PAIR, h * LANES:(h + 1) * LANES] = val

    return _pcall(body, grid=(B, nstep), in_specs=[blk] * 5 + [st, blk], out_specs=[blk] * 5,
                  out_shape=[_sds((T, 4 * LANES))] * 5, scratch_shapes=scratch, name=name,
                  compiler_params=_params(2))(qn, kn, vn, beta, g, states, do)


ROW_TILE = 512
FFN_TILE = 1024
FFN_TILE_BX = 512
SGU_ROWS = 256


def _rows(r, c, off=0):
    return pl.BlockSpec((r, c), lambda i: (i, off))


def _whole(shape, nd):
    zeros = (0,) * len(shape)
    if nd == 1:
        return pl.BlockSpec(shape, lambda i: zeros)
    return pl.BlockSpec(shape, lambda i, j: zeros)


def _norm_fwd(name, h, g):
    T, D = h.shape
    tr = _tile(T, ROW_TILE, 8)
    return _fwd_call(name, _rms_fn, (T // tr,), [(h, _rows(tr, D)), (g, _whole((1, D), 1))],
                     [(_sds((T, D), BF16), _rows(tr, D), False)])[0]


def _norm_bwd(name, h, g, dhn, dh_res):
    T, D = h.shape
    tr = _tile(T, ROW_TILE, 8)
    return _bwd_call(name, _rms_fn, (T // tr,), [(h, _rows(tr, D)), (g, _whole((1, D), 1))], [(dhn, _rows(tr, D))],
                     [(0, _sds((T, D)), _rows(tr, D), False), (1, _sds((1, D)), _whole((1, D), 1), True)],
                     acc_axes=(0,), addends=[(0, dh_res, _rows(tr, D))])


def _ab_specs(B, S, P):
    W4 = 4 * LANES
    seq4 = pl.BlockSpec((S, W4), lambda b: (b, 0))
    pool_ins = lambda proj: [(proj, seq4), (P["pool_w"], _whole((4, LANES, LANES), 1)), (P["pool_scale"], _whole((1, W4), 1))]
    hb = lambda off: pl.BlockSpec((S, LANES), lambda b, h: (b, off + h))
    cw = lambda off: pl.BlockSpec((4, 1, LANES), lambda b, h: (0, 0, off + h))
    small = _whole((1, LANES), 2)
    prep_ins = lambda proj: [(proj, hb(4)), (proj, hb(8)), (proj, hb(12)),
                             (proj, pl.BlockSpec((S, LANES), lambda b, h: (b, 20))),
                             (P["conv_w"], cw(0)), (P["conv_w"], cw(4)), (P["conv_w"], cw(8)),
                             (P["a_log"], small), (P["dt_bias"], small)]
    post_ins = lambda o, proj: [(o, hb(0)), (proj, hb(16)), (P["out_norm"], small)]
    return seq4, pool_ins, hb, prep_ins, post_ins, small


def _ab_fwd(tag, h, P, B, S):
    T, D = h.shape
    W4 = 4 * LANES
    seq4, pool_ins, hb, prep_ins, post_ins, small = _ab_specs(B, S, P)
    hn = _norm_fwd(tag + "_norm", h, P["mix_norm"])
    proj = _mm(tag + "_in", hn, P["w_in"])
    ya = _fwd_call(tag + "_pool", _pool_fn, (B,), pool_ins(proj), [(_sds((T, W4), BF16), seq4, False)])[0]
    qn, kn, vn, beta, g = _fwd_call(tag + "_prep", lambda pids, *v: _prep_fn((pids[1],), *v), (B, 4), prep_ins(proj),
                                    [(_sds((T, W4)), hb(0), False)] * 5)
    o, states = _delta_fwd(tag + "_delta", qn, kn, vn, beta, g, B, S)
    yb = _fwd_call(tag + "_post", _post_fn, (B, 4), post_ins(o, proj), [(_sds((T, W4), BF16), hb(0), False)])[0]
    y = jnp.concatenate([ya, yb], axis=-1)
    h_new = _mm(tag + "_out", y, P["w_out"], res=h)
    return h_new, (h, hn, proj, qn, kn, vn, beta, g, states, o, y)


def _ab_bwd(tag, dh, saved, P, B, S):
    h, hn, proj, qn, kn, vn, beta, g, states, o, y = saved
    T, D = h.shape
    W4 = 4 * LANES
    seq4, pool_ins, hb, prep_ins, post_ins, small = _ab_specs(B, S, P)
    dy = _mm(tag + "_out_dx", dh, P["w_out"], tb=True)
    dw_out = _mm(tag + "_out_dw", y, dh, ta=True)
    do, dz, d_onorm = _bwd_call(tag + "_post_b", _post_fn, (B, 4), post_ins(o, proj), [(dy, hb(4))],
                                [(0, _sds((T, W4)), hb(0), False), (1, _sds((T, W4)), hb(0), False),
                                 (2, _sds((1, LANES)), small, True)], acc_axes=(0, 1))
    dqn, dkn, dvn, dbeta, dg = _delta_bwd(tag + "_delta_b", qn, kn, vn, beta, g, states, do, B, S)
    per_b = pl.BlockSpec((None, 1, LANES), lambda b, h: (b, 0, 0))
    dcw = pl.BlockSpec((None, 4, 1, LANES), lambda b, h: (b, 0, 0, h))
    dq, dk, dv, dbg, dcq, dck, dcv, dalog, ddt = _bwd_call(
        tag + "_prep_b", lambda pids, *v: _prep_fn((pids[1],), *v), (B, 4), prep_ins(proj),
        [(dqn, hb(0)), (dkn, hb(0)), (dvn, hb(0)), (dbeta, hb(0)), (dg, hb(0))],
        [(0, _sds((T, W4)), hb(0), False), (1, _sds((T, W4)), hb(0), False), (2, _sds((T, W4)), hb(0), False),
         (3, _sds((T, LANES)), pl.BlockSpec((S, LANES), lambda b, h: (b, 0)), True),
         (4, _sds((B, 4, 1, W4)), dcw, False), (5, _sds((B, 4, 1, W4)), dcw, False),
         (6, _sds((B, 4, 1, W4)), dcw, False),
         (7, _sds((B, 1, LANES)), per_b, True), (8, _sds((B, 1, LANES)), per_b, True)], acc_axes=(1,))
    d_conv = jnp.concatenate([jnp.sum(d, axis=0)[:, 0, :] for d in (dcq, dck, dcv)], axis=-1)
    da, dpool_w, dpool_scale = _bwd_call(
        tag + "_pool_b", _pool_fn, (B,), pool_ins(proj), [(dy, seq4)],
        [(0, _sds((T, W4)), seq4, False), (1, _sds((4, LANES, LANES)), _whole((4, LANES, LANES), 1), True),
         (2, _sds((1, W4)), _whole((1, W4), 1), True)], acc_axes=(0,))
    dproj = jnp.concatenate([da, dq, dk, dv, dz, dbg], axis=-1)
    dhn = _mm(tag + "_in_dx", dproj, P["w_in"], tb=True)
    dw_in = _mm(tag + "_in_dw", hn, dproj, ta=True)
    dh_new, d_mix = _norm_bwd(tag + "_norm_b", h, P["mix_norm"], dhn, dh)
    grads = dict(w_in=dw_in, w_out=dw_out, mix_norm=d_mix[0], pool_w=dpool_w, pool_scale=dpool_scale[0],
                 conv_w=d_conv, a_log=jnp.sum(dalog, axis=0)[0, :4], dt_bias=jnp.sum(ddt, axis=0)[0, :4],
                 out_norm=d_onorm[0])
    return dh_new, grads


def _cd_specs(B, S, T, P):
    W4 = 4 * LANES
    R = _tile(T, SGU_ROWS, LANES)
    sgu_ins = lambda proj: [(proj, _rows(R, W4, 0)), (proj, _rows(R, W4, 1)), (P["sgu_g"], _whole((1, W4), 1)),
                            (P["sgu_b"], _whole((1, W4), 1)), (P["sgu_w"], _whole((4, LANES, LANES), 1)),
                            (P["bias_t"], _whole((LANES, LANES), 1))]
    jb = lambda off: pl.BlockSpec((S, LANES), lambda j, b: (b, off + j))
    sc_ins = lambda proj: [(proj, jb(8)), (proj, jb(12)), (proj, jb(16)),
                           (P["sc_w"], pl.BlockSpec((3, 1, LANES), lambda j, b: (0, 0, j)))]
    return R, sgu_ins, jb, sc_ins


def _cd_fwd(tag, h, P, B, S):
    T, D = h.shape
    W4 = 4 * LANES
    R, sgu_ins, jb, sc_ins = _cd_specs(B, S, T, P)
    hn = _norm_fwd(tag + "_norm", h, P["mix_norm"])
    proj = _mm(tag + "_in", hn, P["w_in"])
    yc = _fwd_call(tag + "_sgu", _sgu_fn, (T // R,), sgu_ins(proj), [(_sds((T, W4), BF16), _rows(R, W4), False)])[0]
    yd = _fwd_call(tag + "_sconv", _sconv_fn, (4, B), sc_ins(proj), [(_sds((T, W4), BF16), jb(0), False)])[0]
    y = jnp.concatenate([yc, yd], axis=-1)
    h_new = _mm(tag + "_out", y, P["w_out"], res=h)
    return h_new, (h, hn, proj, y)


def _cd_bwd(tag, dh, saved, P, B, S):
    h, hn, proj, y = saved
    T, D = h.shape
    W4 = 4 * LANES
    R, sgu_ins, jb, sc_ins = _cd_specs(B, S, T, P)
    dy = _mm(tag + "_out_dx", dh, P["w_out"], tb=True)
    dw_out = _mm(tag + "_out_dw", y, dh, ta=True)
    du, dv, dsg, dsb, dsw, dbias_t = _bwd_call(
        tag + "_sgu_b", _sgu_fn, (T // R,), sgu_ins(proj), [(dy, _rows(R, W4, 0))],
        [(0, _sds((T, W4)), _rows(R, W4), False), (1, _sds((T, W4)), _rows(R, W4), False),
         (2, _sds((1, W4)), _whole((1, W4), 1), True), (3, _sds((1, W4)), _whole((1, W4), 1), True),
         (4, _sds((4, LANES, LANES)), _whole((4, LANES, LANES), 1), True),
         (5, _sds((LANES, LANES)), _whole((LANES, LANES), 1), True)], acc_axes=(0,))
    dxd, dbgate, dcg, dsc = _bwd_call(
        tag + "_sconv_b", _sconv_fn, (4, B), sc_ins(proj), [(dy, jb(4))],
        [(0, _sds((T, W4)), jb(0), False), (1, _sds((T, W4)), jb(0), False), (2, _sds((T, W4)), jb(0), False),
         (3, _sds((3, 1, W4)), pl.BlockSpec((3, 1, LANES), lambda j, b: (0, 0, j)), True)], acc_axes=(1,))
    dproj = jnp.concatenate([du, dv, dxd, dbgate, dcg], axis=-1)
    dhn = _mm(tag + "_in_dx", dproj, P["w_in"], tb=True)
    dw_in = _mm(tag + "_in_dw", hn, dproj, ta=True)
    dh_new, d_mix = _norm_bwd(tag + "_norm_b", h, P["mix_norm"], dhn, dh)
    grads = dict(w_in=dw_in, w_out=dw_out, mix_norm=d_mix[0], sgu_g=dsg[0], sgu_b=dsb[0], sgu_w=dsw,
                 sgu_bias=dbias_t[:, :4].T, sc_w=dsc[:, 0, :])
    return dh_new, grads


def _loss_fwd_bwd(h, g, tgt):
    T, D = h.shape
    tr = _tile(T, ROW_TILE, 8)
    ins = [(h, _rows(tr, D)), (g, _whole((1, D), 1)), (tgt, _rows(tr, D))]
    vec = _whole((1, LANES), 1)
    loss = _fwd_call("loss", _loss_fn, (T // tr,), ins, [(_sds((1, LANES)), vec, True)], acc_axes=(0,))[0]
    one = jnp.zeros((1, LANES), F32).at[0, 0].set(1.0)
    dh, dg = _bwd_call("loss_b", _loss_fn, (T // tr,), ins, [(one, vec)],
                       [(0, _sds((T, D)), _rows(tr, D), False), (1, _sds((1, D)), _whole((1, D), 1), True)],
                       acc_axes=(0,))
    return loss[0, 0], dh, dg[0]


def _local_step(x2, tgt2, W, B, S):
    L = W["ffn1_norm"].shape[0]
    tm = _tile(x2.shape[0], FFN_TILE, 8)
    tmx = _tile(x2.shape[0], FFN_TILE_BX, 8)
    h = x2
    saved = []
    for l in range(L):
        e = l // 2
        f1 = (W["ffn1_norm"][l][None], W["ffn1_g"], W["ffn1_u"], W["ffn1_d"])
        f2 = (W["ffn2_norm"][l][None], W["ffn2_g"], W["ffn2_u"], W["ffn2_d"])
        h0 = h
        h, gate1, up1 = _ffn_fwd(f"l{l}_ffn1", h0, *f1, l, tm)
        if l % 2 == 0:
            P = dict(mix_norm=W["mix_norm"][l][None], w_in=W["ab_in"][e], w_out=W["ab_out"][e], pool_w=W["pool_w"][e],
                     pool_scale=W["pool_scale"][e][None], conv_w=W["dn_conv_w"][e][:, None, :],
                     a_log=jnp.pad(W["dn_a_log"][e][None], ((0, 0), (0, LANES - 4))),
                     dt_bias=jnp.pad(W["dn_dt_bias"][e][None], ((0, 0), (0, LANES - 4))),
                     out_norm=W["dn_out_norm"][e][None])
            h1 = h
            h, msave = _ab_fwd(f"l{l}_ab", h1, P, B, S)
        else:
            P = dict(mix_norm=W["mix_norm"][l][None], w_in=W["cd_in"][e], w_out=W["cd_out"][e],
                     sgu_g=W["sgu_norm_g"][e][None], sgu_b=W["sgu_norm_b"][e][None], sgu_w=W["sgu_w"][e],
                     bias_t=jnp.pad(W["sgu_bias"][e].T, ((0, 0), (0, LANES - 4))),
                     sc_w=W["sc_conv_w"][e][:, None, :])
            h1 = h
            h, msave = _cd_fwd(f"l{l}_cd", h1, P, B, S)
        h2 = h
        h, gate2, up2 = _ffn_fwd(f"l{l}_ffn2", h2, *f2, l, tm)
        saved.append((f1, f2, P, h0, gate1, up1, msave, h2, gate2, up2))

    loss, dh, d_final = _loss_fwd_bwd(h, W["final_norm"][None], tgt2)

    G = {k: [None] * L for k in ("ffn1_norm", "ffn2_norm", "mix_norm")}
    bufs1 = bufs2 = None
    GA, GC = [None] * ((L + 1) // 2), [None] * (L // 2)
    for l in reversed(range(L)):
        f1, f2, P, h0, gate1, up1, msave, h2, gate2, up2 = saved[l]
        dh_in = dh
        dh, dgate, dup, dn2 = _ffn_bwd_x(f"l{l}_ffn2_bx", dh_in, h2, f2[0], gate2, up2, f2[1], f2[2], f2[3], l, tmx)
        bufs2 = _ffn_bwd_w(f"l{l}_ffn2_bw", dh_in, h2, f2[0], gate2, up2, dgate, dup, bufs2, l, L, tm)
        G["ffn2_norm"][l] = dn2[0]
        if l % 2 == 0:
            dh, mg = _ab_bwd(f"l{l}_ab", dh, msave, P, B, S)
            GA[l // 2] = mg
        else:
            dh, mg = _cd_bwd(f"l{l}_cd", dh, msave, P, B, S)
            GC[l // 2] = mg
        G["mix_norm"][l] = mg["mix_norm"]
        dh_in = dh
        dh, dgate, dup, dn1 = _ffn_bwd_x(f"l{l}_ffn1_bx", dh_in, h0, f1[0], gate1, up1, f1[1], f1[2], f1[3], l, tmx)
        bufs1 = _ffn_bwd_w(f"l{l}_ffn1_bw", dh_in, h0, f1[0], gate1, up1, dgate, dup, bufs1, l, L, tm)
        G["ffn1_norm"][l] = dn1[0]

    st = lambda xs: jnp.stack(xs, axis=0)
    big = dict(ffn1_w_gate=bufs1[0], ffn1_w_up=bufs1[1], ffn1_w_down=bufs1[2],
               ffn2_w_gate=bufs2[0], ffn2_w_up=bufs2[1], ffn2_w_down=bufs2[2],
               ab_w_in=st([m["w_in"] for m in GA]), ab_w_out=st([m["w_out"] for m in GA]),
               cd_w_in=st([m["w_in"] for m in GC]), cd_w_out=st([m["w_out"] for m in GC]))
    small = dict(ffn1_norm=st(G["ffn1_norm"]), mix_norm=st(G["mix_norm"]), ffn2_norm=st(G["ffn2_norm"]),
                 pool_w=st([m["pool_w"] for m in GA]), pool_scale=st([m["pool_scale"] for m in GA]),
                 dn_conv_w=st([m["conv_w"] for m in GA]), dn_a_log=st([m["a_log"] for m in GA]),
                 dn_dt_bias=st([m["dt_bias"] for m in GA]), dn_out_norm=st([m["out_norm"] for m in GA]),
                 sgu_norm_g=st([m["sgu_g"] for m in GC]), sgu_norm_b=st([m["sgu_b"] for m in GC]),
                 sgu_w=st([m["sgu_w"] for m in GC]), sgu_bias=st([m["sgu_bias"] for m in GC]),
                 sc_conv_w=st([m["sc_w"] for m in GC]), final_norm=d_final)
    return loss, dh, big, small


ANY = pl.BlockSpec(memory_space=pl.ANY)


def _place():
    return lax.axis_index("x"), lax.axis_index("y"), lax.axis_index("c")


def _exchange(name, srcs, out_shapes, plan, n_copies, bases=None):
    n, m = len(srcs), len(out_shapes)
    nb = m if bases is not None else 0

    def body(*refs):
        ins, outs = refs[:n], refs[n + nb:n + nb + m]
        send, recv = refs[n + nb + m:]
        remote = plan(_place(), ins, outs)
        assert len(remote) == n_copies
        sends = []
        for k, (s, d, peer, _) in enumerate(remote):
            cp = pltpu.make_async_remote_copy(src_ref=s, dst_ref=d, send_sem=send.at[k], recv_sem=recv.at[k],
                                              device_id=peer, device_id_type=MESH)
            cp.start()
            sends.append(cp)
        for k, (s, _, peer, land) in enumerate(remote):
            pltpu.make_async_remote_copy(src_ref=s, dst_ref=land, send_sem=send.at[k], recv_sem=recv.at[k],
                                         device_id=peer, device_id_type=MESH).wait_recv()
        for cp in sends:
            cp.wait_send()

    return _pcall(body, in_specs=[ANY] * (n + nb), out_specs=[ANY] * m, out_shape=out_shapes,
                  input_output_aliases={n + k: k for k in range(nb)},
                  scratch_shapes=[pltpu.SemaphoreType.DMA((n_copies,)), pltpu.SemaphoreType.DMA((n_copies,))],
                  name=name)(*srcs, *(bases or ()))


def _other_chips(x, y):
    return [(1 - x, y), (x, 1 - y), (1 - x, 1 - y)]


def _gather_chips(name, xs, bases):
    n = len(xs)
    split = [a.ndim >= 3 and a.shape[0] % 2 == 0 for a in xs]
    n_fwd = sum(split)
    fwd_of = {t: j for j, t in enumerate(t for t in range(n) if split[t])}

    def body(*refs):
        ins, outs = refs[:n], refs[2 * n:3 * n]
        send1, recv1, send2, recv2 = refs[3 * n:]
        x, y, c = _place()
        me = 2 * x + y
        chips = _other_chips(x, y)

        def part(t, cc):
            half = xs[t].shape[0] // 2
            return pl.ds(cc * half, half) if split[t] else pl.ds(0, xs[t].shape[0])

        def first(r, t, started):
            px, py = chips[r]
            dst = outs[t].at[me, part(t, c)] if started else outs[t].at[2 * px + py, part(t, c)]
            return pltpu.make_async_remote_copy(src_ref=ins[t].at[part(t, c)], dst_ref=dst, send_sem=send1.at[r, t],
                                                recv_sem=recv1.at[r, t], device_id=(px, py, c), device_id_type=MESH)

        def second(r, t, started):
            px, py = chips[r]
            rows = part(t, c) if started else part(t, 1 - c)
            blk = outs[t].at[2 * px + py, rows]
            return pltpu.make_async_remote_copy(src_ref=blk, dst_ref=blk, send_sem=send2.at[r, fwd_of[t]],
                                                recv_sem=recv2.at[r, fwd_of[t]], device_id=(x, y, 1 - c),
                                                device_id_type=MESH)

        sends = [first(r, t, True) for r in range(3) for t in range(n)]
        for cp in sends:
            cp.start()
        for r in range(3):
            for t in range(n):
                first(r, t, False).wait_recv()
                if split[t]:
                    cp = second(r, t, True)
                    cp.start()
                    sends.append(cp)
        for r in range(3):
            for t in range(n):
                if split[t]:
                    second(r, t, False).wait_recv()
        for cp in sends:
            cp.wait_send()

    return _pcall(body, in_specs=[ANY] * (2 * n), out_specs=[ANY] * n,
                  out_shape=[_sds((N_CHIPS,) + a.shape, a.dtype) for a in xs],
                  input_output_aliases={n + k: k for k in range(n)},
                  scratch_shapes=[pltpu.SemaphoreType.DMA((3, n)), pltpu.SemaphoreType.DMA((3, n)),
                                  pltpu.SemaphoreType.DMA((3, max(n_fwd, 1))), pltpu.SemaphoreType.DMA((3, max(n_fwd, 1)))],
                  name=name)(*xs, *bases)


def _pair_split(name, gs):
    n = len(gs)

    def plan(place, ins, outs):
        x, y, c = place
        return [(ins[t].at[:, :, 1 - c], outs[t], (x, y, 1 - c), outs[t]) for t in range(n)]

    return _exchange(name, gs, [_sds(a.shape[:2] + a.shape[3:], a.dtype) for a in gs], plan, n)


def _chip_scatter(name, ps, bases):
    n = len(ps)

    def plan(place, ins, outs):
        x, y, c = place
        me = 2 * x + y
        return [(ins[t].at[:, 2 * px + py], outs[t].at[me], (px, py, c), outs[t].at[2 * px + py])
                for (px, py) in _other_chips(x, y) for t in range(n)]

    return _exchange(name, ps, [_sds((N_CHIPS, a.shape[0]) + a.shape[2:], a.dtype) for a in ps], plan, 3 * n, bases)


def _pair_share(name, ts):
    n = len(ts)

    def plan(place, ins, outs):
        x, y, c = place
        return [(ins[t], outs[t], (x, y, 1 - c), outs[t]) for t in range(n)]

    return _exchange(name, ts, [_sds(a.shape, a.dtype) for a in ts], plan, n)


def _gather_devices(name, v, base):
    flips = [(fx, fy, fc) for fx in (0, 1) for fy in (0, 1) for fc in (0, 1)][1:]

    def plan(place, ins, outs):
        x, y, c = place
        me = 4 * x + 2 * y + c
        remote = []
        for fx, fy, fc in flips:
            px, py, pc = (1 - x if fx else x), (1 - y if fy else y), (1 - c if fc else c)
            remote.append((ins[0], outs[0].at[me], (px, py, pc), outs[0].at[4 * px + 2 * py + pc]))
        return remote

    return _exchange(name, [v], [_sds((8,) + v.shape, v.dtype)], plan, len(flips), [base])[0]


def _sum_slots(name, a):
    k, rows, cols = a.shape
    tr = _tile(rows, ROW_TILE, 8)
    ins = [(a, pl.BlockSpec((None, tr, cols), lambda i, j=j: (j, i, 0))) for j in range(k)]
    return _fwd_call(name, lambda pids, *v: (functools.reduce(lambda p, q: p + q, v),), (rows // tr,), ins,
                     [(_sds((rows, cols)), _rows(tr, cols), False)])[0]


def _half_tile(ah):
    return _tile(ah, 256, 8)


def _pair_sum(name, g5, theirs, cf, out_dtype):
    L, P4, _, Ah, Bt = g5.shape
    ta = _half_tile(Ah)
    half = lambda hh: pl.BlockSpec((None, None, None, ta, Bt), lambda l, p, i: (l, p, hh, i, 0))
    blk = pl.BlockSpec((None, None, ta, Bt), lambda l, p, i: (l, p, i, 0))
    fn = lambda pids, g0, g1, r, c: (jnp.where(jnp.max(c) > 0.5, g1, g0) + r,)
    return _fwd_call(name, fn, (L, P4, Ah // ta), [(g5, half(0)), (g5, half(1)), (theirs, blk),
                                                   (cf, pl.BlockSpec((1, LANES), lambda l, p, i: (0, 0)))],
                     [(_sds((L, P4, Ah, Bt), out_dtype), blk, False)])[0]


def _adam_terms(w, g, m, v):
    m2 = ADAM_B1 * m + (1.0 - ADAM_B1) * g
    v2 = ADAM_B2 * v + (1.0 - ADAM_B2) * (g * g)
    m_hat = m2 / (1.0 - ADAM_B1 ** ADAM_STEP)
    v_hat = v2 / (1.0 - ADAM_B2 ** ADAM_STEP)
    return -ADAM_LR * (m_hat / (jnp.sqrt(v_hat) + ADAM_EPS) + ADAM_WD * w), m2, v2


def _adam_halves(name, w, m, v, mine, theirs, cf):
    L, Aa, Bt = w.shape
    Ah = Aa // 2
    ta = _half_tile(Ah)
    full = pl.BlockSpec((None, None, ta, Bt), lambda l, hh, i: (l, hh, i, 0))
    part = pl.BlockSpec((None, ta, Bt), lambda l, hh, i: (l, i, 0))

    def fn(pids, w_, m_, v_, a, b, c):
        g = jnp.where(jnp.max(c) == pids[1].astype(F32), a, b)
        return (g,) + _adam_terms(w_, g, m_, v_)

    ins = [(a.reshape(L, 2, Ah, Bt), full) for a in (w, m, v)] + [(mine, part), (theirs, part),
                                                                 (cf, pl.BlockSpec((1, LANES), lambda l, hh, i: (0, 0)))]
    outs = _fwd_call(name, fn, (L, 2, Ah // ta), ins, [(_sds((L, 2, Ah, Bt)), full, False)] * 4)
    return [o.reshape(w.shape) for o in outs]


def _adam_rows(name, w, g, m, v):
    rows, cols = w.shape
    tr = _tile(rows, ROW_TILE, 8)
    ins = [(a, _rows(tr, cols)) for a in (w, g, m, v)]
    return _fwd_call(name, lambda pids, *a: _adam_terms(*a), (rows // tr,), ins,
                     [(_sds((rows, cols)), _rows(tr, cols), False)] * 3)


def _pack(xs):
    flat = jnp.concatenate([a.reshape(-1).astype(F32) for a in xs])
    pad = (-flat.size) % (8 * LANES)
    return jnp.pad(flat, (0, pad)).reshape(-1, LANES)


def _unpack(buf, shapes):
    flat, out, off = buf.reshape(-1), [], 0
    for s in shapes:
        n = math.prod(s)
        out.append(flat[off:off + n].reshape(s))
        off += n
    return out


_WEIGHTS = ("ffn1_norm", "ffn1_w_gate", "ffn1_w_up", "ffn1_w_down", "mix_norm", "ffn2_norm", "ffn2_w_gate", "ffn2_w_up",
            "ffn2_w_down", "ab_w_in", "pool_w", "pool_scale", "dn_conv_w", "dn_a_log", "dn_dt_bias", "dn_out_norm",
            "ab_w_out", "cd_w_in", "sgu_norm_g", "sgu_norm_b", "sgu_w", "sgu_bias", "sc_conv_w", "cd_w_out", "final_norm")
_BIG = ("ffn1_w_gate", "ffn1_w_up", "ffn1_w_down", "ffn2_w_gate", "ffn2_w_up", "ffn2_w_down", "ab_w_in", "ab_w_out",
        "cd_w_in", "cd_w_out")
_ROW_SHARDED = ("ffn1_w_down", "ffn2_w_down", "ab_w_out", "cd_w_out")
_SMALL_SHARDED = ("dn_conv_w", "sgu_norm_g", "sgu_norm_b", "sc_conv_w")
_SMALL = tuple(n for n in _WEIGHTS if n not in _BIG)


def _to_slots(name, g):
    L, A, Bt = g.shape
    if name in _ROW_SHARDED:
        return g.reshape(L, N_CHIPS, A // N_CHIPS, Bt)
    return g.reshape(L, A, N_CHIPS, Bt // N_CHIPS).transpose(0, 2, 1, 3)


def kernel(x, ffn1_norm, ffn1_w_gate, ffn1_w_up, ffn1_w_down, mix_norm, ffn2_norm, ffn2_w_gate, ffn2_w_up, ffn2_w_down,
           ab_w_in, pool_w, pool_scale, dn_conv_w, dn_a_log, dn_dt_bias, dn_out_norm, ab_w_out, cd_w_in, sgu_norm_g,
           sgu_norm_b, sgu_w, sgu_bias, sc_conv_w, cd_w_out, final_norm, loss_target,
           m_ffn1_norm, m_ffn1_w_gate, m_ffn1_w_up, m_ffn1_w_down, m_mix_norm, m_ffn2_norm, m_ffn2_w_gate, m_ffn2_w_up,
           m_ffn2_w_down, m_ab_w_in, m_pool_w, m_pool_scale, m_dn_conv_w, m_dn_a_log, m_dn_dt_bias, m_dn_out_norm,
           m_ab_w_out, m_cd_w_in, m_sgu_norm_g, m_sgu_norm_b, m_sgu_w, m_sgu_bias, m_sc_conv_w, m_cd_w_out, m_final_norm,
           v_ffn1_norm, v_ffn1_w_gate, v_ffn1_w_up, v_ffn1_w_down, v_mix_norm, v_ffn2_norm, v_ffn2_w_gate, v_ffn2_w_up,
           v_ffn2_w_down, v_ab_w_in, v_pool_w, v_pool_scale, v_dn_conv_w, v_dn_a_log, v_dn_dt_bias, v_dn_out_norm,
           v_ab_w_out, v_cd_w_in, v_sgu_norm_g, v_sgu_norm_b, v_sgu_w, v_sgu_bias, v_sc_conv_w, v_cd_w_out, v_final_norm):
    A = dict(locals())
    B, S, D = x.shape
    T = B * S
    xi, yi, ci = _place()
    chip = 2 * xi + yi
    cf = jnp.broadcast_to(ci.astype(F32), (1, LANES))
    own = lambda n, mine, slot: lax.dynamic_update_index_in_dim(lax.empty((n,) + mine.shape, mine.dtype), mine, slot, 0)

    sh_shapes = [A[n].shape for n in _SMALL_SHARDED]
    local = [A[n].astype(BF16) for n in _BIG] + [_pack([A[n] for n in _SMALL_SHARDED])]
    gathered = _gather_chips("gather_weights", local, [own(N_CHIPS, a, chip) for a in local])
    gw = dict(zip(_BIG, gathered[:-1]))
    per_chip = [_unpack(gathered[-1][p], sh_shapes) for p in range(N_CHIPS)]
    W = {n: A[n] for n in _SMALL if n not in _SMALL_SHARDED}
    for j, n in enumerate(_SMALL_SHARDED):
        W[n] = jnp.concatenate([per_chip[p][j] for p in range(N_CHIPS)], axis=-1)
    for f in ("ffn1", "ffn2"):
        W[f + "_g"], W[f + "_u"], W[f + "_d"] = gw[f + "_w_gate"], gw[f + "_w_up"], gw[f + "_w_down"]
    cols = lambda g: jnp.concatenate([g[p] for p in range(N_CHIPS)], axis=-1)
    rows = lambda g: jnp.concatenate([g[p] for p in range(N_CHIPS)], axis=1)
    ab_in = cols(gw["ab_w_in"])
    ab_cols = ab_in.shape[-1]
    ab_pad = (-ab_cols) % LANES
    W["ab_in"] = jnp.pad(ab_in, ((0, 0), (0, 0), (0, ab_pad)))
    W["cd_in"], W["ab_out"], W["cd_out"] = cols(gw["cd_w_in"]), rows(gw["ab_w_out"]), rows(gw["cd_w_out"])

    loss, dx, big, small = _local_step(x.reshape(T, D), loss_target.reshape(T, D), W, B, S)
    loss = lax.psum(loss, ("x", "y", "c"))

    big["ab_w_in"] = big["ab_w_in"][:, :, :ab_cols]
    slots = []
    for n in _BIG:
        g = big[n] if big[n].ndim == 4 else _to_slots(n, big[n])
        L, P4, Aa, Bt = g.shape
        slots.append(g.reshape(L, P4, 2, Aa // 2, Bt))
    theirs = _pair_split("grads_pair_split", slots)
    pair = [_pair_sum(f"grads_pair_sum_{n}", g5, r, cf, BF16) for n, g5, r in zip(_BIG, slots, theirs)]
    parts = _chip_scatter("grads_chip_scatter", pair,
                          [own(N_CHIPS, lax.dynamic_index_in_dim(p, chip, 1, keepdims=False), chip) for p in pair])
    tot = [_sum_slots(f"grads_chip_sum_{n}", p.reshape(N_CHIPS, -1, p.shape[-1])).reshape(p.shape[1:])
           for n, p in zip(_BIG, parts)]
    other = _pair_share("grads_pair_share", tot)

    sm_shapes = [small[n].shape for n in _SMALL]
    sm_local = _pack([small[n] for n in _SMALL])
    every = _gather_devices("gather_small_grads", sm_local, own(8, sm_local, 2 * chip + ci))
    grads = {}
    for n, g in zip(_SMALL, _unpack(_sum_slots("small_grads_sum", every), sm_shapes)):
        if n in _SMALL_SHARDED:
            w_loc = A[n].shape[-1]
            g = lax.dynamic_slice_in_dim(g, chip * w_loc, w_loc, axis=g.ndim - 1)
        grads[n] = g.reshape(A[n].shape)

    delta, new_m, new_v = {}, {}, {}
    for n, a, b in zip(_BIG, tot, other):
        grads[n], delta[n], new_m[n], new_v[n] = _adam_halves(f"adam_{n}", A[n], A["m_" + n], A["v_" + n], a, b, cf)
    packed = [_pack([d[n] for n in _SMALL]) for d in (A, grads, {n: A["m_" + n] for n in _SMALL},
                                                       {n: A["v_" + n] for n in _SMALL})]
    loc_shapes = [A[n].shape for n in _SMALL]
    for d, buf in zip((delta, new_m, new_v), _adam_rows("adam_small", *packed)):
        d.update(zip(_SMALL, _unpack(buf, loc_shapes)))

    return (loss, dx.reshape(B, S, D), *[grads[n] for n in _WEIGHTS], *[delta[n] for n in _WEIGHTS],
            *[new_m[n] for n in _WEIGHTS], *[new_v[n] for n in _WEIGHTS])
```

```python
import functools
import math

import jax
import jax.numpy as jnp
from jax import lax
from jax.experimental import pallas as pl
from jax.experimental.pallas import tpu as pltpu

F32, BF16 = jnp.float32, jnp.bfloat16
HIGHEST = lax.Precision.HIGHEST
MESH = pl.DeviceIdType.MESH

EPS = 1e-6
LANES = 128
CH = 64
PAIR = 2 * CH
POOL_WINDOWS = (2, 4, 8, 16)
N_CHIPS = 4
ADAM_LR, ADAM_B1, ADAM_B2, ADAM_EPS, ADAM_WD, ADAM_STEP = 0.001, 0.9, 0.999, 1e-08, 0.01, 10
VMEM_LIMIT = 60 * 1024 * 1024


def _pcall(body, **kw):
    return pl.pallas_call(body, **kw)


def _params(nd):
    return pltpu.CompilerParams(dimension_semantics=("arbitrary",) * nd, vmem_limit_bytes=VMEM_LIMIT)


def _sds(shape, dtype=F32):
    return jax.ShapeDtypeStruct(tuple(shape), dtype)


def _tile(n, cap, mult=LANES):
    if n <= cap:
        return n
    best = None
    for t in range(mult, cap + 1, mult):
        if n % t == 0:
            best = t
    assert best is not None, (n, cap)
    return best


def _dg(a, b, ca, cb):
    return lax.dot_general(a.astype(BF16), b.astype(BF16), (((ca,), (cb,)), ((), ())),
                           preferred_element_type=F32)


@jax.custom_vjp
def _bdot(a, b):
    return _dg(a, b, 1, 0)


def _bdot_fwd(a, b):
    return _dg(a, b, 1, 0), (a, b)


def _bdot_bwd(res, g):
    a, b = res
    return _dg(g, b, 1, 1), _dg(a, g, 0, 0)


_bdot.defvjp(_bdot_fwd, _bdot_bwd)


@jax.custom_vjp
def _bdot_nt(a, b):
    return _dg(a, b, 1, 1)


def _bdot_nt_fwd(a, b):
    return _dg(a, b, 1, 1), (a, b)


def _bdot_nt_bwd(res, g):
    a, b = res
    return _dg(g, b, 1, 0), _dg(g, a, 0, 0)


_bdot_nt.defvjp(_bdot_nt_fwd, _bdot_nt_bwd)


def _hdot(a, b):
    return jnp.dot(a, b, precision=HIGHEST, preferred_element_type=F32)


def _shift_raw(x, k):
    n = x.shape[0]
    rows = lax.broadcasted_iota(jnp.int32, x.shape, 0)
    r = pltpu.roll(x, k % n, 0)
    if k > 0:
        return jnp.where(rows >= k, r, 0.0)
    return jnp.where(rows < n + k, r, 0.0)


@functools.partial(jax.custom_vjp, nondiff_argnums=(1,))
def _shift(x, k):
    return _shift_raw(x, k)


def _shift_fwd(x, k):
    return _shift_raw(x, k), None


def _shift_bwd(k, _, g):
    return (_shift(g, -k),)


_shift.defvjp(_shift_fwd, _shift_bwd)


def _col(x, j):
    lanes = lax.broadcasted_iota(jnp.int32, x.shape, x.ndim - 1)
    return jnp.sum(jnp.where(lanes == j, x, 0.0), axis=-1, keepdims=True)


def _rms(x, g):
    return x * lax.rsqrt(jnp.mean(x * x, axis=-1, keepdims=True) + EPS) * g


def _sigmoid(x):
    return 1.0 / (1.0 + jnp.exp(-x))


def _silu(x):
    return x * _sigmoid(x)


def _sigmoid_fast(x):
    return pl.reciprocal(1.0 + jnp.exp(-x), approx=True)


def _softplus(x):
    return jnp.maximum(x, 0.0) + jnp.log(1.0 + jnp.exp(-jnp.abs(x)))


def _gelu(x):
    c = math.sqrt(2.0 / math.pi)
    return 0.5 * x * (1.0 + jnp.tanh(c * (x + 0.044715 * (x * x * x))))


def _first(axes):
    c = None
    for a in axes:
        t = pl.program_id(a) == 0
        c = t if c is None else jnp.logical_and(c, t)
    return c


def _fwd_call(name, fn, grid, ins, outs, acc_axes=()):
    nin = len(ins)

    def body(*refs):
        pids = tuple(pl.program_id(a) for a in range(len(grid)))
        first = _first(acc_axes) if acc_axes else None
        vals = [r[...].astype(F32) for r in refs[:nin]]
        res = fn(pids, *vals)
        for r, o, (_, _, acc) in zip(refs[nin:], res, outs):
            if acc:
                @pl.when(first)
                def _(r=r):
                    r[...] = jnp.zeros_like(r)
                r[...] += o.astype(r.dtype)
            else:
                r[...] = o.astype(r.dtype)

    return _pcall(body, grid=grid, in_specs=[s for _, s in ins], out_specs=[s for _, s, _ in outs],
                  out_shape=[s for s, _, _ in outs], name=name,
                  compiler_params=_params(len(grid)))(*[a for a, _ in ins])


def _bwd_call(name, fn, grid, ins, cts, gouts, acc_axes=(), addends=()):
    nin, nct, nadd = len(ins), len(cts), len(addends)

    def body(*refs):
        pids = tuple(pl.program_id(a) for a in range(len(grid)))
        first = _first(acc_axes) if acc_axes else None
        vals = [r[...].astype(F32) for r in refs[:nin]]
        ctv = tuple(r[...].astype(F32) for r in refs[nin:nin + nct])
        addv = {pos: refs[nin + nct + n][...].astype(F32) for n, (pos, _, _) in enumerate(addends)}
        _, vjp = jax.vjp(lambda *v: tuple(fn(pids, *v)), *vals)
        grads = vjp(ctv)
        for pos, (r, (idx, _, _, acc)) in enumerate(zip(refs[nin + nct + nadd:], gouts)):
            gval = grads[idx]
            if pos in addv:
                gval = gval + addv[pos]
            if acc:
                @pl.when(first)
                def _(r=r):
                    r[...] = jnp.zeros_like(r)
                r[...] += gval.astype(r.dtype)
            else:
                r[...] = gval.astype(r.dtype)

    args = [a for a, _ in ins] + [a for a, _ in cts] + [a for _, a, _ in addends]
    specs = [s for _, s in ins] + [s for _, s in cts] + [s for _, _, s in addends]
    return _pcall(body, grid=grid, in_specs=specs, out_specs=[s for _, _, s, _ in gouts],
                  out_shape=[s for _, s, _, _ in gouts], name=name,
                  compiler_params=_params(len(grid)))(*args)


def _mm(name, a, b, *, ta=False, tb=False, res=None, tm=512, tn=None, tk=None):
    if ta:
        K, M = a.shape
    else:
        M, K = a.shape
    N = b.shape[0] if tb else b.shape[1]
    tm, tn, tk = _tile(M, tm, 8), N if tn is None else tn, K if tk is None else min(tk, K)
    nk = K // tk
    assert res is None or nk == 1
    a_spec = pl.BlockSpec((tk, tm), lambda i, j, k: (k, i)) if ta else pl.BlockSpec((tm, tk), lambda i, j, k: (i, k))
    b_spec = pl.BlockSpec((tn, tk), lambda i, j, k: (j, k)) if tb else pl.BlockSpec((tk, tn), lambda i, j, k: (k, j))
    o_spec = pl.BlockSpec((tm, tn), lambda i, j, k: (i, j))
    ca, cb = (0 if ta else 1), (1 if tb else 0)

    def body(*refs):
        a_ref, b_ref, o_ref = refs[0], refs[1], refs[-1]
        part = _dg(a_ref[...], b_ref[...], ca, cb)
        if nk == 1:
            o_ref[...] = part if res is None else refs[2][...] + part
        else:
            @pl.when(pl.program_id(2) == 0)
            def _():
                o_ref[...] = jnp.zeros_like(o_ref)

            o_ref[...] += part

    args, specs = [a, b], [a_spec, b_spec]
    if res is not None:
        args.append(res)
        specs.append(o_spec)
    return _pcall(body, grid=(M // tm, N // tn, nk), in_specs=specs, out_specs=o_spec, out_shape=_sds((M, N)),
                  name=name, compiler_params=_params(3))(*args)


FFN_PARTS = 2


def _ffn_fwd(name, h, g, wg, wu, wd, l, tm):
    T, D = h.shape
    ns, _, _, Fs = wg.shape

    def body(h_ref, g_ref, wg_ref, wu_ref, wd_ref, ho_ref, gate_ref, up_ref, hn_s, acc_s):
        s = pl.program_id(1)

        @pl.when(s == 0)
        def _():
            hn_s[...] = _rms(h_ref[...], g_ref[...]).astype(BF16)
            acc_s[...] = jnp.zeros_like(acc_s)

        parts = [pl.ds(k * (tm // FFN_PARTS), tm // FFN_PARTS) for k in range(FFN_PARTS)]
        gu = [(_dg(hn_s[r, :], wg_ref[...], 1, 0), _dg(hn_s[r, :], wu_ref[...], 1, 0)) for r in parts]
        for r, (gate, up) in zip(parts, gu):
            gate_ref[r, :] = gate.astype(BF16)
            up_ref[r, :] = up.astype(BF16)
            act = (gate * _sigmoid_fast(gate) * up).astype(BF16)
            acc_s[r, :] += _dg(act, wd_ref[...], 1, 0)

        @pl.when(s == ns - 1)
        def _():
            ho_ref[...] = h_ref[...] + 0.5 * acc_s[...]

    row = pl.BlockSpec((tm, D), lambda i, s: (i, 0))
    wcol = pl.BlockSpec((None, None, D, Fs), lambda i, s: (s, l, 0, 0))
    wrow = pl.BlockSpec((None, None, Fs, D), lambda i, s: (s, l, 0, 0))
    slot = pl.BlockSpec((None, tm, Fs), lambda i, s: (s, i, 0))
    return _pcall(body, grid=(T // tm, ns), in_specs=[row, pl.BlockSpec((1, D), lambda i, s: (0, 0)), wcol, wcol, wrow],
                  out_specs=[row, slot, slot],
                  out_shape=[_sds((T, D)), _sds((ns, T, Fs), BF16), _sds((ns, T, Fs), BF16)],
                  scratch_shapes=[pltpu.VMEM((tm, D), BF16), pltpu.VMEM((tm, D), F32)], name=name,
                  compiler_params=_params(2))(h, g, wg, wu, wd)


def _ffn_bwd_x(name, dh, h, g, gate, up, wg, wu, wd, l, tm):
    T, D = h.shape
    ns, _, _, Fs = wg.shape

    def body(dh_ref, h_ref, g_ref, gate_ref, up_ref, wg_ref, wu_ref, wd_ref,
             dho_ref, dgate_ref, dup_ref, dgain_ref, do_s, acc_s):
        i, s = pl.program_id(0), pl.program_id(1)

        @pl.when(s == 0)
        def _():
            do_s[...] = (0.5 * dh_ref[...]).astype(BF16)
            acc_s[...] = jnp.zeros_like(acc_s)

        @pl.when(jnp.logical_and(i == 0, s == 0))
        def _():
            dgain_ref[...] = jnp.zeros_like(dgain_ref)

        parts = [pl.ds(k * (tm // FFN_PARTS), tm // FFN_PARTS) for k in range(FFN_PARTS)]
        dact = [_dg(do_s[r, :], wd_ref[...], 1, 1) for r in parts]
        for r, da in zip(parts, dact):
            gt, u = gate_ref[r, :].astype(F32), up_ref[r, :].astype(F32)
            sg = _sigmoid_fast(gt)
            dgt = (da * u * (sg * (1.0 + gt * (1.0 - sg)))).astype(BF16)
            du = (da * (gt * sg)).astype(BF16)
            dgate_ref[r, :] = dgt
            dup_ref[r, :] = du
            acc_s[r, :] += _dg(dgt, wg_ref[...], 1, 1) + _dg(du, wu_ref[...], 1, 1)

        @pl.when(s == ns - 1)
        def _():
            _, vjp = jax.vjp(_rms, h_ref[...], g_ref[...])
            dx, dgain = vjp(acc_s[...])
            dho_ref[...] = dh_ref[...] + dx
            dgain_ref[...] += dgain

    row = pl.BlockSpec((tm, D), lambda i, s: (i, 0))
    gain = pl.BlockSpec((1, D), lambda i, s: (0, 0))
    wcol = pl.BlockSpec((None, None, D, Fs), lambda i, s: (s, l, 0, 0))
    wrow = pl.BlockSpec((None, None, Fs, D), lambda i, s: (s, l, 0, 0))
    slot = pl.BlockSpec((None, tm, Fs), lambda i, s: (s, i, 0))
    return _pcall(body, grid=(T // tm, ns), in_specs=[row, row, gain, slot, slot, wcol, wcol, wrow],
                  out_specs=[row, slot, slot, gain],
                  out_shape=[_sds((T, D)), _sds((ns, T, Fs), BF16), _sds((ns, T, Fs), BF16), _sds((1, D))],
                  scratch_shapes=[pltpu.VMEM((tm, D), BF16), pltpu.VMEM((tm, D), F32)], name=name,
                  compiler_params=_params(2))(dh, h, g, gate, up, wg, wu, wd)


def _ffn_bwd_w(name, dh, h, g, gate, up, dgate, dup, bufs, l, nl, tm):
    T, D = h.shape
    ns, _, Fs = gate.shape

    def body(dh_ref, h_ref, g_ref, gate_ref, up_ref, dgate_ref, dup_ref, dwg_ref, dwu_ref, dwd_ref):
        @pl.when(pl.program_id(1) == 0)
        def _():
            dwg_ref[...] = jnp.zeros_like(dwg_ref)
            dwu_ref[...] = jnp.zeros_like(dwu_ref)
            dwd_ref[...] = jnp.zeros_like(dwd_ref)

        def prep(r):
            hn = _rms(h_ref[r, :], g_ref[...]).astype(BF16)
            gt = gate_ref[r, :].astype(F32)
            act = (gt * _sigmoid_fast(gt) * up_ref[r, :].astype(F32)).astype(BF16)
            return hn, act, (0.5 * dh_ref[r, :]).astype(BF16)

        parts = [pl.ds(k * (tm // FFN_PARTS), tm // FFN_PARTS) for k in range(FFN_PARTS)]
        ready = [prep(r) for r in parts]
        add = lambda xs: functools.reduce(lambda p, q: p + q, xs)
        dwg_ref[...] += add([_dg(hn, dgate_ref[r, :], 0, 0) for r, (hn, _, _) in zip(parts, ready)])
        dwu_ref[...] += add([_dg(hn, dup_ref[r, :], 0, 0) for r, (hn, _, _) in zip(parts, ready)])
        dwd_ref[...] += add([_dg(act, do, 0, 0) for _, act, do in ready])

    row = pl.BlockSpec((tm, D), lambda s, j: (j, 0))
    gain = pl.BlockSpec((1, D), lambda s, j: (0, 0))
    slot = pl.BlockSpec((None, tm, Fs), lambda s, j: (s, j, 0))
    wcol = pl.BlockSpec((None, None, D, Fs), lambda s, j: (l, s, 0, 0))
    wrow = pl.BlockSpec((None, None, Fs, D), lambda s, j: (l, s, 0, 0))
    args, specs, alias = [dh, h, g, gate, up, dgate, dup], [row, row, gain, slot, slot, slot, slot], {}
    if bufs is not None:
        alias = {len(args) + k: k for k in range(3)}
        args, specs = args + list(bufs), specs + [pl.BlockSpec(memory_space=pl.ANY)] * 3
    return _pcall(lambda *refs: body(*refs[:7], *refs[-3:]), grid=(ns, T // tm), in_specs=specs,
                  out_specs=[wcol, wcol, wrow], input_output_aliases=alias,
                  out_shape=[_sds((nl, ns, D, Fs)), _sds((nl, ns, D, Fs)), _sds((nl, ns, Fs, D))], name=name,
                  compiler_params=_params(2))(*args)


def _rms_fn(pids, x, g):
    return (_rms(x, g),)


def _pool_fn(pids, a, w, scale):
    rows = lax.broadcasted_iota(jnp.int32, (a.shape[0], LANES), 0)
    outs = []
    for gi, win in enumerate(POOL_WINDOWS):
        ag = a[:, gi * LANES:(gi + 1) * LANES]
        s, k = ag, 1
        while k < win:
            s = s + _shift(s, k)
            k *= 2
        cnt = jnp.minimum(rows + 1, win).astype(F32)
        pooled = s / cnt - ag
        outs.append(_bdot(pooled, w[gi]) * scale[:, gi * LANES:(gi + 1) * LANES])
    return (jnp.concatenate(outs, axis=-1),)


def _conv_taps(x, cw):
    K = cw.shape[0]
    y = cw[K - 1] * x
    for j in range(K - 1):
        y = y + cw[j] * _shift(x, K - 1 - j)
    return y


def _prep_fn(pids, q, k, v, bg, cwq, cwk, cwv, alog, dtb):
    hd = pids[0]
    q, k, v = _silu(_conv_taps(q, cwq)), _silu(_conv_taps(k, cwk)), _silu(_conv_taps(v, cwv))
    q = q * lax.rsqrt(jnp.sum(q * q, axis=-1, keepdims=True) + EPS) * (LANES ** -0.5)
    k = k * lax.rsqrt(jnp.sum(k * k, axis=-1, keepdims=True) + EPS)
    beta = _sigmoid(_col(bg, hd))
    g = -jnp.exp(_col(alog, hd)) * _softplus(_col(bg, hd + 4) + _col(dtb, hd))
    return q, k, v, jnp.broadcast_to(beta, q.shape), jnp.broadcast_to(g, q.shape)


def _post_fn(pids, o, z, onorm):
    return (_rms(o, onorm) * _silu(z),)


def _sgu_fn(pids, u, v, g, b, ws, bias_t):
    u, v = _gelu(u), _gelu(v)
    mu = jnp.mean(v, axis=-1, keepdims=True)
    xc = v - mu
    vn = xc * lax.rsqrt(jnp.mean(xc * xc, axis=-1, keepdims=True) + EPS) * g + b
    r = lax.broadcasted_iota(jnp.int32, (LANES, LANES), 0)
    c = lax.broadcasted_iota(jnp.int32, (LANES, LANES), 1)
    rows = []
    for n in range(u.shape[0] // LANES):
        heads = []
        for hd in range(4):
            wm = jnp.where(r >= c, ws[hd], 0.0)
            blk = vn[n * LANES:(n + 1) * LANES, hd * LANES:(hd + 1) * LANES]
            heads.append(_bdot(wm, blk) + _col(bias_t, hd))
        rows.append(jnp.concatenate(heads, axis=-1))
    mixed = jnp.concatenate(rows, axis=0) if len(rows) > 1 else rows[0]
    return (u * mixed,)


def _sconv_fn(pids, xd, bgate, cg, cw):
    return (bgate * _conv_taps(cg * xd, cw),)


def _loss_fn(pids, h, g, tgt):
    err = _rms(h, g) - tgt
    tot = 0.5 * jnp.sum(jnp.mean(err * err, axis=-1, keepdims=True), axis=0, keepdims=True)
    return (jnp.broadcast_to(tot, (1, LANES)),)


DELTA_PAIRS = 2
DELTA_ROWS = DELTA_PAIRS * PAIR


def _each(fn, *lists):
    return [fn(*args) for args in zip(*lists)]


def _chunk_cumsum(g):
    pos = jnp.bitwise_and(lax.broadcasted_iota(jnp.int32, g.shape, 0), CH - 1)
    s, k = g, 1
    while k < CH:
        s = s + jnp.where(pos >= k, _shift(s, k), 0.0)
        k *= 2
    tot = [jnp.broadcast_to(jnp.sum(g[j * CH:(j + 1) * CH], axis=0, keepdims=True), (CH, g.shape[1])) for j in (0, 1)]
    return s, jnp.concatenate(tot, axis=0)


def _stage_a(blocks):
    q, k, v, beta, g = (list(x) for x in zip(*blocks))
    r = lax.broadcasted_iota(jnp.int32, (PAIR, PAIR), 0)
    c = lax.broadcasted_iota(jnp.int32, (PAIR, PAIR), 1)
    same = jnp.right_shift(r, 6) == jnp.right_shift(c, 6)
    tri = jnp.logical_and(same, r >= c)
    strict = jnp.logical_and(same, r > c)
    eye = (r == c).astype(F32)
    sums = _each(_chunk_cumsum, g)
    gc, gt = [a for a, _ in sums], [b for _, b in sums]
    gamma = _each(lambda x: jnp.exp(jnp.where(tri, x - x.T, -jnp.inf)), gc)
    kb = _each(lambda a, b: a * b, k, beta)
    kk = _each(_bdot_nt, kb, k)
    p = _each(lambda a, gm: -jnp.where(strict, a * gm, 0.0), kk, gamma)
    tinv = _each(lambda x: eye + x, p)
    for _ in range(5):
        p = _each(_bdot, p, p)
        tinv = _each(lambda t, x: t + x, tinv, _each(_bdot, tinv, p))
    egc = _each(jnp.exp, gc)
    u = _each(_bdot, tinv, _each(lambda a, b: a * b, v, beta))
    w = _each(_bdot, tinv, _each(lambda a, e: a * e, kb, egc))
    a = _each(lambda x, gm: x * gm, _each(_bdot_nt, q, k), gamma)
    qd = _each(lambda x, e: x * e, q, egc)
    kd = _each(lambda x, t, s: x * jnp.exp(t - s), k, gt, gc)
    return list(zip(u, w, qd, a, kd, _each(jnp.exp, gt)))


def _nn(a, b):
    return _dg(a, b, 1, 0)


def _nt(a, b):
    return _dg(a, b, 1, 1)


def _tn(a, b):
    return _dg(a, b, 0, 0)


def _scan_pair_fwd(s0, u, w, qd, a, kd, l0, l1):
    lo = lambda xs: [x[:CH] for x in xs]
    hi = lambda xs: [x[CH:] for x in xs]
    vn0 = _each(lambda x, y: x - y, lo(u), _each(_nn, lo(w), s0))
    s1 = _each(lambda s, l, x: s * l + x, s0, l0, _each(_tn, lo(kd), vn0))
    vn1 = _each(lambda x, y: x - y, hi(u), _each(_nn, hi(w), s1))
    s2 = _each(lambda s, l, x: s * l + x, s1, l1, _each(_tn, hi(kd), vn1))
    inter = _each(lambda x, y: jnp.concatenate([x, y], axis=0), _each(_nn, lo(qd), s0), _each(_nn, hi(qd), s1))
    intra = _each(_nn, a, _each(lambda x, y: jnp.concatenate([x, y], axis=0), vn0, vn1))
    return _each(lambda x, y: x + y, inter, intra), s1, s2


def _scan_chunk_bwd(s, ds_next, do, dvn_o, u, w, qd, kd, lrow):
    vn = _each(lambda x, y: x - y, u, _each(_nn, w, s))
    dvn = _each(lambda x, y: x + y, dvn_o, _each(_nn, kd, ds_next))
    dkd = _each(_nt, vn, ds_next)
    dl = _each(lambda a, b: jnp.sum(a * b, axis=0, keepdims=True), ds_next, s)
    dw = _each(lambda x: -x, _each(_nt, dvn, s))
    dqd = _each(_nt, do, s)
    ds = _each(lambda x, l, d, y: x + l * d - y, _each(_tn, qd, do), lrow, ds_next, _each(_tn, w, dvn))
    return ds, vn, dvn, dw, dqd, dkd, dl


def _delta_blocks(refs):
    return [tuple(r[p * PAIR:(p + 1) * PAIR, h * LANES:(h + 1) * LANES] for r in refs)
            for p in range(DELTA_PAIRS) for h in range(4)]


def _delta_specs(B, S, reverse):
    nstep = S // DELTA_ROWS
    at = (lambda i: nstep - 1 - i) if reverse else (lambda i: i)
    blk = pl.BlockSpec((DELTA_ROWS, 4 * LANES), lambda b, i: (b * nstep + at(i), 0))
    st = pl.BlockSpec((None, 4, 2 * DELTA_PAIRS, LANES, LANES), lambda b, i: (b, 0, at(i), 0, 0))
    scratch = [pltpu.VMEM((4, LANES, LANES), F32), pltpu.VMEM((4 * DELTA_PAIRS, PAIR, LANES), F32)]
    return nstep, blk, st, scratch


def _delta_fwd(name, qn, kn, vn, beta, g, B, S):
    T = qn.shape[0]
    nstep, blk, st, scratch = _delta_specs(B, S, False)

    def body(q_ref, k_ref, v_ref, b_ref, g_ref, o_ref, st_ref, s_s, l_s):
        @pl.when(pl.program_id(1) == 0)
        def _():
            s_s[...] = jnp.zeros_like(s_s)

        outs = _stage_a(_delta_blocks((q_ref, k_ref, v_ref, b_ref, g_ref)))
        for j, blk_out in enumerate(outs):
            l_s[j] = blk_out[5]
        state = [s_s[h] for h in range(4)]
        for p in range(DELTA_PAIRS):
            u, w, qd, a, kd, _ = (list(x) for x in zip(*outs[4 * p:4 * p + 4]))
            l0 = [l_s[4 * p + h, 0:1, :] for h in range(4)]
            l1 = [l_s[4 * p + h, CH:CH + 1, :] for h in range(4)]
            o, mid, end = _scan_pair_fwd(state, u, w, qd, a, kd, l0, l1)
            for h in range(4):
                o_ref[p * PAIR:(p + 1) * PAIR, h * LANES:(h + 1) * LANES] = o[h]
                st_ref[h, 2 * p] = state[h]
                st_ref[h, 2 * p + 1] = mid[h]
            state = end
        for h in range(4):
            s_s[h] = state[h]

    return _pcall(body, grid=(B, nstep), in_specs=[blk] * 5, out_specs=[blk, st],
                  out_shape=[_sds((T, 4 * LANES)), _sds((B, 4, S // CH, LANES, LANES))],
                  scratch_shapes=scratch, name=name, compiler_params=_params(2))(qn, kn, vn, beta, g)


def _delta_bwd(name, qn, kn, vn, beta, g, states, do, B, S):
    T = qn.shape[0]
    nstep, blk, st, scratch = _delta_specs(B, S, True)

    def body(q_ref, k_ref, v_ref, b_ref, g_ref, st_ref, do_ref, dq_ref, dk_ref, dv_ref, db_ref, dg_ref, ds_s, l_s):
        @pl.when(pl.program_id(1) == 0)
        def _():
            ds_s[...] = jnp.zeros_like(ds_s)

        rowid = lax.broadcasted_iota(jnp.int32, (PAIR, LANES), 0)
        lo = lambda xs: [x[:CH] for x in xs]
        hi = lambda xs: [x[CH:] for x in xs]
        cat = lambda xs, ys: _each(lambda x, y: jnp.concatenate([x, y], axis=0), xs, ys)
        outs, vjp = jax.vjp(_stage_a, _delta_blocks((q_ref, k_ref, v_ref, b_ref, g_ref)))
        for j, blk_out in enumerate(outs):
            l_s[j] = blk_out[5]
        ds = [ds_s[h] for h in range(4)]
        cts = [None] * (4 * DELTA_PAIRS)
        for p in reversed(range(DELTA_PAIRS)):
            u, w, qd, a, kd, _ = (list(x) for x in zip(*outs[4 * p:4 * p + 4]))
            l0 = [l_s[4 * p + h, 0:1, :] for h in range(4)]
            l1 = [l_s[4 * p + h, CH:CH + 1, :] for h in range(4)]
            s0 = [st_ref[h, 2 * p] for h in range(4)]
            s1 = [st_ref[h, 2 * p + 1] for h in range(4)]
            dout = [do_ref[p * PAIR:(p + 1) * PAIR, h * LANES:(h + 1) * LANES] for h in range(4)]
            dvn_o = _each(_tn, a, dout)
            ds1, vn1, dvn1, dw1, dqd1, dkd1, dl1 = _scan_chunk_bwd(s1, ds, hi(dout), hi(dvn_o), hi(u), hi(w), hi(qd),
                                                                   hi(kd), l1)
            ds, vn0, dvn0, dw0, dqd0, dkd0, dl0 = _scan_chunk_bwd(s0, ds1, lo(dout), lo(dvn_o), lo(u), lo(w), lo(qd),
                                                                  lo(kd), l0)
            da = _each(_nt, dout, cat(vn0, vn1))
            dl = _each(lambda x, y: jnp.where(rowid == 0, x, jnp.where(rowid == CH, y, 0.0)), dl0, dl1)
            for h, ct in enumerate(zip(cat(dvn0, dvn1), cat(dw0, dw1), cat(dqd0, dqd1), da, cat(dkd0, dkd1), dl)):
                cts[4 * p + h] = ct
        for h in range(4):
            ds_s[h] = ds[h]
        (grads,) = vjp(cts)
        for j, blk_grads in enumerate(grads):
            p, h = divmod(j, 4)
            for ref, val in zip((dq_ref, dk_ref, dv_ref, db_ref, dg_ref), blk_grads):
                ref[p * PAIR:(p + 1) * PAIR, h * LANES:(h + 1) * LANES] = val

    return _pcall(body, grid=(B, nstep), in_specs=[blk] * 5 + [st, blk], out_specs=[blk] * 5,
                  out_shape=[_sds((T, 4 * LANES))] * 5, scratch_shapes=scratch, name=name,
                  compiler_params=_params(2))(qn, kn, vn, beta, g, states, do)


ROW_TILE = 512
FFN_TILE = 1024
FFN_TILE_BX = 1024
SGU_ROWS = 256


def _rows(r, c, off=0):
    return pl.BlockSpec((r, c), lambda i: (i, off))


def _whole(shape, nd):
    zeros = (0,) * len(shape)
    if nd == 1:
        return pl.BlockSpec(shape, lambda i: zeros)
    return pl.BlockSpec(shape, lambda i, j: zeros)


def _norm_fwd(name, h, g):
    T, D = h.shape
    tr = _tile(T, ROW_TILE, 8)
    return _fwd_call(name, _rms_fn, (T // tr,), [(h, _rows(tr, D)), (g, _whole((1, D), 1))],
                     [(_sds((T, D), BF16), _rows(tr, D), False)])[0]


def _norm_bwd(name, h, g, dhn, dh_res):
    T, D = h.shape
    tr = _tile(T, ROW_TILE, 8)
    return _bwd_call(name, _rms_fn, (T // tr,), [(h, _rows(tr, D)), (g, _whole((1, D), 1))], [(dhn, _rows(tr, D))],
                     [(0, _sds((T, D)), _rows(tr, D), False), (1, _sds((1, D)), _whole((1, D), 1), True)],
                     acc_axes=(0,), addends=[(0, dh_res, _rows(tr, D))])


def _ab_specs(B, S, P):
    W4 = 4 * LANES
    seq4 = pl.BlockSpec((S, W4), lambda b: (b, 0))
    pool_ins = lambda proj: [(proj, seq4), (P["pool_w"], _whole((4, LANES, LANES), 1)), (P["pool_scale"], _whole((1, W4), 1))]
    hb = lambda off: pl.BlockSpec((S, LANES), lambda b, h: (b, off + h))
    cw = lambda off: pl.BlockSpec((4, 1, LANES), lambda b, h: (0, 0, off + h))
    small = _whole((1, LANES), 2)
    prep_ins = lambda proj: [(proj, hb(4)), (proj, hb(8)), (proj, hb(12)),
                             (proj, pl.BlockSpec((S, LANES), lambda b, h: (b, 20))),
                             (P["conv_w"], cw(0)), (P["conv_w"], cw(4)), (P["conv_w"], cw(8)),
                             (P["a_log"], small), (P["dt_bias"], small)]
    post_ins = lambda o, proj: [(o, hb(0)), (proj, hb(16)), (P["out_norm"], small)]
    return seq4, pool_ins, hb, prep_ins, post_ins, small


def _ab_fwd(tag, h, P, B, S):
    T, D = h.shape
    W4 = 4 * LANES
    seq4, pool_ins, hb, prep_ins, post_ins, small = _ab_specs(B, S, P)
    hn = _norm_fwd(tag + "_norm", h, P["mix_norm"])
    proj = _mm(tag + "_in", hn, P["w_in"], tm=256)
    ya = _fwd_call(tag + "_pool", _pool_fn, (B,), pool_ins(proj), [(_sds((T, W4), BF16), seq4, False)])[0]
    qn, kn, vn, beta, g = _fwd_call(tag + "_prep", lambda pids, *v: _prep_fn((pids[1],), *v), (B, 4), prep_ins(proj),
                                    [(_sds((T, W4)), hb(0), False)] * 5)
    o, states = _delta_fwd(tag + "_delta", qn, kn, vn, beta, g, B, S)
    yb = _fwd_call(tag + "_post", _post_fn, (B, 4), post_ins(o, proj), [(_sds((T, W4), BF16), hb(0), False)])[0]
    y = jnp.concatenate([ya, yb], axis=-1)
    h_new = _mm(tag + "_out", y, P["w_out"], res=h)
    return h_new, (h, hn, proj, qn, kn, vn, beta, g, states, o, y)


def _ab_bwd(tag, dh, saved, P, B, S):
    h, hn, proj, qn, kn, vn, beta, g, states, o, y = saved
    T, D = h.shape
    W4 = 4 * LANES
    seq4, pool_ins, hb, prep_ins, post_ins, small = _ab_specs(B, S, P)
    dy = _mm(tag + "_out_dx", dh, P["w_out"], tb=True)
    dw_out = _mm(tag + "_out_dw", y, dh, ta=True, tm=D, tk=1024)
    do, dz, d_onorm = _bwd_call(tag + "_post_b", _post_fn, (B, 4), post_ins(o, proj), [(dy, hb(4))],
                                [(0, _sds((T, W4)), hb(0), False), (1, _sds((T, W4)), hb(0), False),
                                 (2, _sds((1, LANES)), small, True)], acc_axes=(0, 1))
    dqn, dkn, dvn, dbeta, dg = _delta_bwd(tag + "_delta_b", qn, kn, vn, beta, g, states, do, B, S)
    per_b = pl.BlockSpec((None, 1, LANES), lambda b, h: (b, 0, 0))
    dcw = pl.BlockSpec((None, 4, 1, LANES), lambda b, h: (b, 0, 0, h))
    dq, dk, dv, dbg, dcq, dck, dcv, dalog, ddt = _bwd_call(
        tag + "_prep_b", lambda pids, *v: _prep_fn((pids[1],), *v), (B, 4), prep_ins(proj),
        [(dqn, hb(0)), (dkn, hb(0)), (dvn, hb(0)), (dbeta, hb(0)), (dg, hb(0))],
        [(0, _sds((T, W4)), hb(0), False), (1, _sds((T, W4)), hb(0), False), (2, _sds((T, W4)), hb(0), False),
         (3, _sds((T, LANES)), pl.BlockSpec((S, LANES), lambda b, h: (b, 0)), True),
         (4, _sds((B, 4, 1, W4)), dcw, False), (5, _sds((B, 4, 1, W4)), dcw, False),
         (6, _sds((B, 4, 1, W4)), dcw, False),
         (7, _sds((B, 1, LANES)), per_b, True), (8, _sds((B, 1, LANES)), per_b, True)], acc_axes=(1,))
    d_conv = jnp.concatenate([jnp.sum(d, axis=0)[:, 0, :] for d in (dcq, dck, dcv)], axis=-1)
    da, dpool_w, dpool_scale = _bwd_call(
        tag + "_pool_b", _pool_fn, (B,), pool_ins(proj), [(dy, seq4)],
        [(0, _sds((T, W4)), seq4, False), (1, _sds((4, LANES, LANES)), _whole((4, LANES, LANES), 1), True),
         (2, _sds((1, W4)), _whole((1, W4), 1), True)], acc_axes=(0,))
    dproj = jnp.concatenate([da, dq, dk, dv, dz, dbg], axis=-1)
    dhn = _mm(tag + "_in_dx", dproj, P["w_in"], tb=True, tm=256)
    dw_in = _mm(tag + "_in_dw", hn, dproj, ta=True, tm=D, tk=512)
    dh_new, d_mix = _norm_bwd(tag + "_norm_b", h, P["mix_norm"], dhn, dh)
    grads = dict(w_in=dw_in, w_out=dw_out, mix_norm=d_mix[0], pool_w=dpool_w, pool_scale=dpool_scale[0],
                 conv_w=d_conv, a_log=jnp.sum(dalog, axis=0)[0, :4], dt_bias=jnp.sum(ddt, axis=0)[0, :4],
                 out_norm=d_onorm[0])
    return dh_new, grads


def _cd_specs(B, S, T, P):
    W4 = 4 * LANES
    R = _tile(T, SGU_ROWS, LANES)
    sgu_ins = lambda proj: [(proj, _rows(R, W4, 0)), (proj, _rows(R, W4, 1)), (P["sgu_g"], _whole((1, W4), 1)),
                            (P["sgu_b"], _whole((1, W4), 1)), (P["sgu_w"], _whole((4, LANES, LANES), 1)),
                            (P["bias_t"], _whole((LANES, LANES), 1))]
    jb = lambda off: pl.BlockSpec((S, LANES), lambda j, b: (b, off + j))
    sc_ins = lambda proj: [(proj, jb(8)), (proj, jb(12)), (proj, jb(16)),
                           (P["sc_w"], pl.BlockSpec((3, 1, LANES), lambda j, b: (0, 0, j)))]
    return R, sgu_ins, jb, sc_ins


def _cd_fwd(tag, h, P, B, S):
    T, D = h.shape
    W4 = 4 * LANES
    R, sgu_ins, jb, sc_ins = _cd_specs(B, S, T, P)
    hn = _norm_fwd(tag + "_norm", h, P["mix_norm"])
    proj = _mm(tag + "_in", hn, P["w_in"], tm=256)
    yc = _fwd_call(tag + "_sgu", _sgu_fn, (T // R,), sgu_ins(proj), [(_sds((T, W4), BF16), _rows(R, W4), False)])[0]
    yd = _fwd_call(tag + "_sconv", _sconv_fn, (4, B), sc_ins(proj), [(_sds((T, W4), BF16), jb(0), False)])[0]
    y = jnp.concatenate([yc, yd], axis=-1)
    h_new = _mm(tag + "_out", y, P["w_out"], res=h)
    return h_new, (h, hn, proj, y)


def _cd_bwd(tag, dh, saved, P, B, S):
    h, hn, proj, y = saved
    T, D = h.shape
    W4 = 4 * LANES
    R, sgu_ins, jb, sc_ins = _cd_specs(B, S, T, P)
    dy = _mm(tag + "_out_dx", dh, P["w_out"], tb=True)
    dw_out = _mm(tag + "_out_dw", y, dh, ta=True, tm=D, tk=1024)
    du, dv, dsg, dsb, dsw, dbias_t = _bwd_call(
        tag + "_sgu_b", _sgu_fn, (T // R,), sgu_ins(proj), [(dy, _rows(R, W4, 0))],
        [(0, _sds((T, W4)), _rows(R, W4), False), (1, _sds((T, W4)), _rows(R, W4), False),
         (2, _sds((1, W4)), _whole((1, W4), 1), True), (3, _sds((1, W4)), _whole((1, W4), 1), True),
         (4, _sds((4, LANES, LANES)), _whole((4, LANES, LANES), 1), True),
         (5, _sds((LANES, LANES)), _whole((LANES, LANES), 1), True)], acc_axes=(0,))
    dxd, dbgate, dcg, dsc = _bwd_call(
        tag + "_sconv_b", _sconv_fn, (4, B), sc_ins(proj), [(dy, jb(4))],
        [(0, _sds((T, W4)), jb(0), False), (1, _sds((T, W4)), jb(0), False), (2, _sds((T, W4)), jb(0), False),
         (3, _sds((3, 1, W4)), pl.BlockSpec((3, 1, LANES), lambda j, b: (0, 0, j)), True)], acc_axes=(1,))
    dproj = jnp.concatenate([du, dv, dxd, dbgate, dcg], axis=-1)
    dhn = _mm(tag + "_in_dx", dproj, P["w_in"], tb=True, tm=256)
    dw_in = _mm(tag + "_in_dw", hn, dproj, ta=True, tm=D, tk=512)
    dh_new, d_mix = _norm_bwd(tag + "_norm_b", h, P["mix_norm"], dhn, dh)
    grads = dict(w_in=dw_in, w_out=dw_out, mix_norm=d_mix[0], sgu_g=dsg[0], sgu_b=dsb[0], sgu_w=dsw,
                 sgu_bias=dbias_t[:, :4].T, sc_w=dsc[:, 0, :])
    return dh_new, grads


def _loss_fwd_bwd(h, g, tgt):
    T, D = h.shape
    tr = _tile(T, ROW_TILE, 8)
    ins = [(h, _rows(tr, D)), (g, _whole((1, D), 1)), (tgt, _rows(tr, D))]
    vec = _whole((1, LANES), 1)
    loss = _fwd_call("loss", _loss_fn, (T // tr,), ins, [(_sds((1, LANES)), vec, True)], acc_axes=(0,))[0]
    one = jnp.zeros((1, LANES), F32).at[0, 0].set(1.0)
    dh, dg = _bwd_call("loss_b", _loss_fn, (T // tr,), ins, [(one, vec)],
                       [(0, _sds((T, D)), _rows(tr, D), False), (1, _sds((1, D)), _whole((1, D), 1), True)],
                       acc_axes=(0,))
    return loss[0, 0], dh, dg[0]


def _local_step(x2, tgt2, W, B, S):
    L = W["ffn1_norm"].shape[0]
    tm = _tile(x2.shape[0], FFN_TILE, 8)
    tmx = _tile(x2.shape[0], FFN_TILE_BX, 8)
    h = x2
    saved = []
    for l in range(L):
        e = l // 2
        f1 = (W["ffn1_norm"][l][None], W["ffn1_g"], W["ffn1_u"], W["ffn1_d"])
        f2 = (W["ffn2_norm"][l][None], W["ffn2_g"], W["ffn2_u"], W["ffn2_d"])
        h0 = h
        h, gate1, up1 = _ffn_fwd(f"l{l}_ffn1", h0, *f1, l, tm)
        if l % 2 == 0:
            P = dict(mix_norm=W["mix_norm"][l][None], w_in=W["ab_in"][e], w_out=W["ab_out"][e], pool_w=W["pool_w"][e],
                     pool_scale=W["pool_scale"][e][None], conv_w=W["dn_conv_w"][e][:, None, :],
                     a_log=jnp.pad(W["dn_a_log"][e][None], ((0, 0), (0, LANES - 4))),
                     dt_bias=jnp.pad(W["dn_dt_bias"][e][None], ((0, 0), (0, LANES - 4))),
                     out_norm=W["dn_out_norm"][e][None])
            h1 = h
            h, msave = _ab_fwd(f"l{l}_ab", h1, P, B, S)
        else:
            P = dict(mix_norm=W["mix_norm"][l][None], w_in=W["cd_in"][e], w_out=W["cd_out"][e],
                     sgu_g=W["sgu_norm_g"][e][None], sgu_b=W["sgu_norm_b"][e][None], sgu_w=W["sgu_w"][e],
                     bias_t=jnp.pad(W["sgu_bias"][e].T, ((0, 0), (0, LANES - 4))),
                     sc_w=W["sc_conv_w"][e][:, None, :])
            h1 = h
            h, msave = _cd_fwd(f"l{l}_cd", h1, P, B, S)
        h2 = h
        h, gate2, up2 = _ffn_fwd(f"l{l}_ffn2", h2, *f2, l, tm)
        saved.append((f1, f2, P, h0, gate1, up1, msave, h2, gate2, up2))

    loss, dh, d_final = _loss_fwd_bwd(h, W["final_norm"][None], tgt2)

    G = {k: [None] * L for k in ("ffn1_norm", "ffn2_norm", "mix_norm")}
    bufs1 = bufs2 = None
    GA, GC = [None] * ((L + 1) // 2), [None] * (L // 2)
    for l in reversed(range(L)):
        f1, f2, P, h0, gate1, up1, msave, h2, gate2, up2 = saved[l]
        dh_in = dh
        dh, dgate, dup, dn2 = _ffn_bwd_x(f"l{l}_ffn2_bx", dh_in, h2, f2[0], gate2, up2, f2[1], f2[2], f2[3], l, tmx)
        bufs2 = _ffn_bwd_w(f"l{l}_ffn2_bw", dh_in, h2, f2[0], gate2, up2, dgate, dup, bufs2, l, L, tm)
        G["ffn2_norm"][l] = dn2[0]
        if l % 2 == 0:
            dh, mg = _ab_bwd(f"l{l}_ab", dh, msave, P, B, S)
            GA[l // 2] = mg
        else:
            dh, mg = _cd_bwd(f"l{l}_cd", dh, msave, P, B, S)
            GC[l // 2] = mg
        G["mix_norm"][l] = mg["mix_norm"]
        dh_in = dh
        dh, dgate, dup, dn1 = _ffn_bwd_x(f"l{l}_ffn1_bx", dh_in, h0, f1[0], gate1, up1, f1[1], f1[2], f1[3], l, tmx)
        bufs1 = _ffn_bwd_w(f"l{l}_ffn1_bw", dh_in, h0, f1[0], gate1, up1, dgate, dup, bufs1, l, L, tm)
        G["ffn1_norm"][l] = dn1[0]

    st = lambda xs: jnp.stack(xs, axis=0)
    big = dict(ffn1_w_gate=bufs1[0], ffn1_w_up=bufs1[1], ffn1_w_down=bufs1[2],
               ffn2_w_gate=bufs2[0], ffn2_w_up=bufs2[1], ffn2_w_down=bufs2[2],
               ab_w_in=st([m["w_in"] for m in GA]), ab_w_out=st([m["w_out"] for m in GA]),
               cd_w_in=st([m["w_in"] for m in GC]), cd_w_out=st([m["w_out"] for m in GC]))
    small = dict(ffn1_norm=st(G["ffn1_norm"]), mix_norm=st(G["mix_norm"]), ffn2_norm=st(G["ffn2_norm"]),
                 pool_w=st([m["pool_w"] for m in GA]), pool_scale=st([m["pool_scale"] for m in GA]),
                 dn_conv_w=st([m["conv_w"] for m in GA]), dn_a_log=st([m["a_log"] for m in GA]),
                 dn_dt_bias=st([m["dt_bias"] for m in GA]), dn_out_norm=st([m["out_norm"] for m in GA]),
                 sgu_norm_g=st([m["sgu_g"] for m in GC]), sgu_norm_b=st([m["sgu_b"] for m in GC]),
                 sgu_w=st([m["sgu_w"] for m in GC]), sgu_bias=st([m["sgu_bias"] for m in GC]),
                 sc_conv_w=st([m["sc_w"] for m in GC]), final_norm=d_final)
    return loss, dh, big, small


ANY = pl.BlockSpec(memory_space=pl.ANY)


def _place():
    return lax.axis_index("x"), lax.axis_index("y"), lax.axis_index("c")


def _exchange(name, srcs, out_shapes, plan, n_copies, bases=None):
    n, m = len(srcs), len(out_shapes)
    nb = m if bases is not None else 0

    def body(*refs):
        ins, outs = refs[:n], refs[n + nb:n + nb + m]
        send, recv = refs[n + nb + m:]
        remote = plan(_place(), ins, outs)
        assert len(remote) == n_copies
        sends = []
        for k, (s, d, peer, _) in enumerate(remote):
            cp = pltpu.make_async_remote_copy(src_ref=s, dst_ref=d, send_sem=send.at[k], recv_sem=recv.at[k],
                                              device_id=peer, device_id_type=MESH)
            cp.start()
            sends.append(cp)
        for k, (s, _, peer, land) in enumerate(remote):
            pltpu.make_async_remote_copy(src_ref=s, dst_ref=land, send_sem=send.at[k], recv_sem=recv.at[k],
                                         device_id=peer, device_id_type=MESH).wait_recv()
        for cp in sends:
            cp.wait_send()

    return _pcall(body, in_specs=[ANY] * (n + nb), out_specs=[ANY] * m, out_shape=out_shapes,
                  input_output_aliases={n + k: k for k in range(nb)},
                  scratch_shapes=[pltpu.SemaphoreType.DMA((n_copies,)), pltpu.SemaphoreType.DMA((n_copies,))],
                  name=name)(*srcs, *(bases or ()))


def _other_chips(x, y):
    return [(1 - x, y), (x, 1 - y), (1 - x, 1 - y)]


def _gather_chips(name, xs, bases):
    n = len(xs)
    split = [a.ndim >= 3 and a.shape[0] % 2 == 0 for a in xs]
    n_fwd = sum(split)
    fwd_of = {t: j for j, t in enumerate(t for t in range(n) if split[t])}

    def body(*refs):
        ins, outs = refs[:n], refs[2 * n:3 * n]
        send1, recv1, send2, recv2 = refs[3 * n:]
        x, y, c = _place()
        me = 2 * x + y
        chips = _other_chips(x, y)

        def part(t, cc):
            half = xs[t].shape[0] // 2
            return pl.ds(cc * half, half) if split[t] else pl.ds(0, xs[t].shape[0])

        def first(r, t, started):
            px, py = chips[r]
            dst = outs[t].at[me, part(t, c)] if started else outs[t].at[2 * px + py, part(t, c)]
            return pltpu.make_async_remote_copy(src_ref=ins[t].at[part(t, c)], dst_ref=dst, send_sem=send1.at[r, t],
                                                recv_sem=recv1.at[r, t], device_id=(px, py, c), device_id_type=MESH)

        def second(r, t, started):
            px, py = chips[r]
            rows = part(t, c) if started else part(t, 1 - c)
            blk = outs[t].at[2 * px + py, rows]
            return pltpu.make_async_remote_copy(src_ref=blk, dst_ref=blk, send_sem=send2.at[r, fwd_of[t]],
                                                recv_sem=recv2.at[r, fwd_of[t]], device_id=(x, y, 1 - c),
                                                device_id_type=MESH)

        sends = [first(r, t, True) for r in range(3) for t in range(n)]
        for cp in sends:
            cp.start()
        for r in range(3):
            for t in range(n):
                first(r, t, False).wait_recv()
                if split[t]:
                    cp = second(r, t, True)
                    cp.start()
                    sends.append(cp)
        for r in range(3):
            for t in range(n):
                if split[t]:
                    second(r, t, False).wait_recv()
        for cp in sends:
            cp.wait_send()

    return _pcall(body, in_specs=[ANY] * (2 * n), out_specs=[ANY] * n,
                  out_shape=[_sds((N_CHIPS,) + a.shape, a.dtype) for a in xs],
                  input_output_aliases={n + k: k for k in range(n)},
                  scratch_shapes=[pltpu.SemaphoreType.DMA((3, n)), pltpu.SemaphoreType.DMA((3, n)),
                                  pltpu.SemaphoreType.DMA((3, max(n_fwd, 1))), pltpu.SemaphoreType.DMA((3, max(n_fwd, 1)))],
                  name=name)(*xs, *bases)


def _pair_split(name, gs):
    n = len(gs)

    def plan(place, ins, outs):
        x, y, c = place
        return [(ins[t].at[:, :, 1 - c], outs[t], (x, y, 1 - c), outs[t]) for t in range(n)]

    return _exchange(name, gs, [_sds(a.shape[:2] + a.shape[3:], a.dtype) for a in gs], plan, n)


def _chip_scatter(name, ps, bases):
    n = len(ps)

    def plan(place, ins, outs):
        x, y, c = place
        me = 2 * x + y
        return [(ins[t].at[:, 2 * px + py], outs[t].at[me], (px, py, c), outs[t].at[2 * px + py])
                for (px, py) in _other_chips(x, y) for t in range(n)]

    return _exchange(name, ps, [_sds((N_CHIPS, a.shape[0]) + a.shape[2:], a.dtype) for a in ps], plan, 3 * n, bases)


def _pair_share(name, ts):
    n = len(ts)

    def plan(place, ins, outs):
        x, y, c = place
        return [(ins[t], outs[t], (x, y, 1 - c), outs[t]) for t in range(n)]

    return _exchange(name, ts, [_sds(a.shape, a.dtype) for a in ts], plan, n)


def _gather_devices(name, v, base):
    flips = [(fx, fy, fc) for fx in (0, 1) for fy in (0, 1) for fc in (0, 1)][1:]

    def plan(place, ins, outs):
        x, y, c = place
        me = 4 * x + 2 * y + c
        remote = []
        for fx, fy, fc in flips:
            px, py, pc = (1 - x if fx else x), (1 - y if fy else y), (1 - c if fc else c)
            remote.append((ins[0], outs[0].at[me], (px, py, pc), outs[0].at[4 * px + 2 * py + pc]))
        return remote

    return _exchange(name, [v], [_sds((8,) + v.shape, v.dtype)], plan, len(flips), [base])[0]


def _sum_slots(name, a):
    k, rows, cols = a.shape
    tr = _tile(rows, ROW_TILE, 8)
    ins = [(a, pl.BlockSpec((None, tr, cols), lambda i, j=j: (j, i, 0))) for j in range(k)]
    return _fwd_call(name, lambda pids, *v: (functools.reduce(lambda p, q: p + q, v),), (rows // tr,), ins,
                     [(_sds((rows, cols)), _rows(tr, cols), False)])[0]


def _half_tile(ah):
    return _tile(ah, 256, 8)


def _pair_sum(name, g5, theirs, cf, out_dtype):
    L, P4, _, Ah, Bt = g5.shape
    ta = _half_tile(Ah)
    half = lambda hh: pl.BlockSpec((None, None, None, ta, Bt), lambda l, p, i: (l, p, hh, i, 0))
    blk = pl.BlockSpec((None, None, ta, Bt), lambda l, p, i: (l, p, i, 0))
    fn = lambda pids, g0, g1, r, c: (jnp.where(jnp.max(c) > 0.5, g1, g0) + r,)
    return _fwd_call(name, fn, (L, P4, Ah // ta), [(g5, half(0)), (g5, half(1)), (theirs, blk),
                                                   (cf, pl.BlockSpec((1, LANES), lambda l, p, i: (0, 0)))],
                     [(_sds((L, P4, Ah, Bt), out_dtype), blk, False)])[0]


def _adam_terms(w, g, m, v):
    m2 = ADAM_B1 * m + (1.0 - ADAM_B1) * g
    v2 = ADAM_B2 * v + (1.0 - ADAM_B2) * (g * g)
    m_hat = m2 / (1.0 - ADAM_B1 ** ADAM_STEP)
    v_hat = v2 / (1.0 - ADAM_B2 ** ADAM_STEP)
    return -ADAM_LR * (m_hat / (jnp.sqrt(v_hat) + ADAM_EPS) + ADAM_WD * w), m2, v2


def _adam_halves(name, w, m, v, mine, theirs, cf):
    L, Aa, Bt = w.shape
    Ah = Aa // 2
    ta = _half_tile(Ah)
    full = pl.BlockSpec((None, None, ta, Bt), lambda l, hh, i: (l, hh, i, 0))
    part = pl.BlockSpec((None, ta, Bt), lambda l, hh, i: (l, i, 0))

    def fn(pids, w_, m_, v_, a, b, c):
        g = jnp.where(jnp.max(c) == pids[1].astype(F32), a, b)
        return (g,) + _adam_terms(w_, g, m_, v_)

    ins = [(a.reshape(L, 2, Ah, Bt), full) for a in (w, m, v)] + [(mine, part), (theirs, part),
                                                                 (cf, pl.BlockSpec((1, LANES), lambda l, hh, i: (0, 0)))]
    outs = _fwd_call(name, fn, (L, 2, Ah // ta), ins, [(_sds((L, 2, Ah, Bt)), full, False)] * 4)
    return [o.reshape(w.shape) for o in outs]


def _adam_rows(name, w, g, m, v):
    rows, cols = w.shape
    tr = _tile(rows, ROW_TILE, 8)
    ins = [(a, _rows(tr, cols)) for a in (w, g, m, v)]
    return _fwd_call(name, lambda pids, *a: _adam_terms(*a), (rows // tr,), ins,
                     [(_sds((rows, cols)), _rows(tr, cols), False)] * 3)


def _pack(xs):
    flat = jnp.concatenate([a.reshape(-1).astype(F32) for a in xs])
    pad = (-flat.size) % (8 * LANES)
    return jnp.pad(flat, (0, pad)).reshape(-1, LANES)


def _unpack(buf, shapes):
    flat, out, off = buf.reshape(-1), [], 0
    for s in shapes:
        n = math.prod(s)
        out.append(flat[off:off + n].reshape(s))
        off += n
    return out


_WEIGHTS = ("ffn1_norm", "ffn1_w_gate", "ffn1_w_up", "ffn1_w_down", "mix_norm", "ffn2_norm", "ffn2_w_gate", "ffn2_w_up",
            "ffn2_w_down", "ab_w_in", "pool_w", "pool_scale", "dn_conv_w", "dn_a_log", "dn_dt_bias", "dn_out_norm",
            "ab_w_out", "cd_w_in", "sgu_norm_g", "sgu_norm_b", "sgu_w", "sgu_bias", "sc_conv_w", "cd_w_out", "final_norm")
_BIG = ("ffn1_w_gate", "ffn1_w_up", "ffn1_w_down", "ffn2_w_gate", "ffn2_w_up", "ffn2_w_down", "ab_w_in", "ab_w_out",
        "cd_w_in", "cd_w_out")
_ROW_SHARDED = ("ffn1_w_down", "ffn2_w_down", "ab_w_out", "cd_w_out")
_SMALL_SHARDED = ("dn_conv_w", "sgu_norm_g", "sgu_norm_b", "sc_conv_w")
_SMALL = tuple(n for n in _WEIGHTS if n not in _BIG)


def _to_slots(name, g):
    L, A, Bt = g.shape
    if name in _ROW_SHARDED:
        return g.reshape(L, N_CHIPS, A // N_CHIPS, Bt)
    return g.reshape(L, A, N_CHIPS, Bt // N_CHIPS).transpose(0, 2, 1, 3)


def kernel(x, ffn1_norm, ffn1_w_gate, ffn1_w_up, ffn1_w_down, mix_norm, ffn2_norm, ffn2_w_gate, ffn2_w_up, ffn2_w_down,
           ab_w_in, pool_w, pool_scale, dn_conv_w, dn_a_log, dn_dt_bias, dn_out_norm, ab_w_out, cd_w_in, sgu_norm_g,
           sgu_norm_b, sgu_w, sgu_bias, sc_conv_w, cd_w_out, final_norm, loss_target,
           m_ffn1_norm, m_ffn1_w_gate, m_ffn1_w_up, m_ffn1_w_down, m_mix_norm, m_ffn2_norm, m_ffn2_w_gate, m_ffn2_w_up,
           m_ffn2_w_down, m_ab_w_in, m_pool_w, m_pool_scale, m_dn_conv_w, m_dn_a_log, m_dn_dt_bias, m_dn_out_norm,
           m_ab_w_out, m_cd_w_in, m_sgu_norm_g, m_sgu_norm_b, m_sgu_w, m_sgu_bias, m_sc_conv_w, m_cd_w_out, m_final_norm,
           v_ffn1_norm, v_ffn1_w_gate, v_ffn1_w_up, v_ffn1_w_down, v_mix_norm, v_ffn2_norm, v_ffn2_w_gate, v_ffn2_w_up,
           v_ffn2_w_down, v_ab_w_in, v_pool_w, v_pool_scale, v_dn_conv_w, v_dn_a_log, v_dn_dt_bias, v_dn_out_norm,
           v_ab_w_out, v_cd_w_in, v_sgu_norm_g, v_sgu_norm_b, v_sgu_w, v_sgu_bias, v_sc_conv_w, v_cd_w_out, v_final_norm):
    A = dict(locals())
    B, S, D = x.shape
    T = B * S
    xi, yi, ci = _place()
    chip = 2 * xi + yi
    cf = jnp.broadcast_to(ci.astype(F32), (1, LANES))
    own = lambda n, mine, slot: lax.dynamic_update_index_in_dim(lax.empty((n,) + mine.shape, mine.dtype), mine, slot, 0)

    sh_shapes = [A[n].shape for n in _SMALL_SHARDED]
    local = [A[n].astype(BF16) for n in _BIG] + [_pack([A[n] for n in _SMALL_SHARDED])]
    gathered = _gather_chips("gather_weights", local, [own(N_CHIPS, a, chip) for a in local])
    gw = dict(zip(_BIG, gathered[:-1]))
    per_chip = [_unpack(gathered[-1][p], sh_shapes) for p in range(N_CHIPS)]
    W = {n: A[n] for n in _SMALL if n not in _SMALL_SHARDED}
    for j, n in enumerate(_SMALL_SHARDED):
        W[n] = jnp.concatenate([per_chip[p][j] for p in range(N_CHIPS)], axis=-1)
    for f in ("ffn1", "ffn2"):
        W[f + "_g"], W[f + "_u"], W[f + "_d"] = gw[f + "_w_gate"], gw[f + "_w_up"], gw[f + "_w_down"]
    cols = lambda g: jnp.concatenate([g[p] for p in range(N_CHIPS)], axis=-1)
    rows = lambda g: jnp.concatenate([g[p] for p in range(N_CHIPS)], axis=1)
    ab_in = cols(gw["ab_w_in"])
    ab_cols = ab_in.shape[-1]
    ab_pad = (-ab_cols) % LANES
    W["ab_in"] = jnp.pad(ab_in, ((0, 0), (0, 0), (0, ab_pad)))
    W["cd_in"], W["ab_out"], W["cd_out"] = cols(gw["cd_w_in"]), rows(gw["ab_w_out"]), rows(gw["cd_w_out"])

    loss, dx, big, small = _local_step(x.reshape(T, D), loss_target.reshape(T, D), W, B, S)
    loss = lax.psum(loss, ("x", "y", "c"))

    big["ab_w_in"] = big["ab_w_in"][:, :, :ab_cols]
    slots = []
    for n in _BIG:
        g = big[n] if big[n].ndim == 4 else _to_slots(n, big[n])
        L, P4, Aa, Bt = g.shape
        slots.append(g.reshape(L, P4, 2, Aa // 2, Bt))
    theirs = _pair_split("grads_pair_split", slots)
    pair = [_pair_sum(f"grads_pair_sum_{n}", g5, r, cf, BF16) for n, g5, r in zip(_BIG, slots, theirs)]
    parts = _chip_scatter("grads_chip_scatter", pair,
                          [own(N_CHIPS, lax.dynamic_index_in_dim(p, chip, 1, keepdims=False), chip) for p in pair])
    tot = [_sum_slots(f"grads_chip_sum_{n}", p.reshape(N_CHIPS, -1, p.shape[-1])).reshape(p.shape[1:])
           for n, p in zip(_BIG, parts)]
    other = _pair_share("grads_pair_share", tot)

    sm_shapes = [small[n].shape for n in _SMALL]
    sm_local = _pack([small[n] for n in _SMALL])
    every = _gather_devices("gather_small_grads", sm_local, own(8, sm_local, 2 * chip + ci))
    grads = {}
    for n, g in zip(_SMALL, _unpack(_sum_slots("small_grads_sum", every), sm_shapes)):
        if n in _SMALL_SHARDED:
            w_loc = A[n].shape[-1]
            g = lax.dynamic_slice_in_dim(g, chip * w_loc, w_loc, axis=g.ndim - 1)
        grads[n] = g.reshape(A[n].shape)

    delta, new_m, new_v = {}, {}, {}
    for n, a, b in zip(_BIG, tot, other):
        grads[n], delta[n], new_m[n], new_v[n] = _adam_halves(f"adam_{n}", A[n], A["m_" + n], A["v_" + n], a, b, cf)
    packed = [_pack([d[n] for n in _SMALL]) for d in (A, grads, {n: A["m_" + n] for n in _SMALL},
                                                       {n: A["v_" + n] for n in _SMALL})]
    loc_shapes = [A[n].shape for n in _SMALL]
    for d, buf in zip((delta, new_m, new_v), _adam_rows("adam_small", *packed)):
        d.update(zip(_SMALL, _unpack(buf, loc_shapes)))

    return (loss, dx.reshape(B, S, D), *[grads[n] for n in _WEIGHTS], *[delta[n] for n in _WEIGHTS],
            *[new_m[n] for n in _WEIGHTS], *[new_v[n] for n in _WEIGHTS])
```

```python
import functools
import math

import jax
import jax.numpy as jnp
from jax import lax
from jax.experimental import pallas as pl
from jax.experimental.pallas import tpu as pltpu

F32, BF16 = jnp.float32, jnp.bfloat16
HIGHEST = lax.Precision.HIGHEST
MESH = pl.DeviceIdType.MESH

EPS = 1e-6
LANES = 128
CH = 64
PAIR = 2 * CH
POOL_WINDOWS = (2, 4, 8, 16)
N_CHIPS = 4
ADAM_LR, ADAM_B1, ADAM_B2, ADAM_EPS, ADAM_WD, ADAM_STEP = 0.001, 0.9, 0.999, 1e-08, 0.01, 10
VMEM_LIMIT = 60 * 1024 * 1024


def _pcall(body, **kw):
    return pl.pallas_call(body, **kw)


def _params(nd):
    return pltpu.CompilerParams(dimension_semantics=("arbitrary",) * nd, vmem_limit_bytes=VMEM_LIMIT)


def _sds(shape, dtype=F32):
    return jax.ShapeDtypeStruct(tuple(shape), dtype)


def _tile(n, cap, mult=LANES):
    if n <= cap:
        return n
    best = None
    for t in range(mult, cap + 1, mult):
        if n % t == 0:
            best = t
    assert best is not None, (n, cap)
    return best


def _dg(a, b, ca, cb):
    return lax.dot_general(a.astype(BF16), b.astype(BF16), (((ca,), (cb,)), ((), ())),
                           preferred_element_type=F32)


@jax.custom_vjp
def _bdot(a, b):
    return _dg(a, b, 1, 0)


def _bdot_fwd(a, b):
    return _dg(a, b, 1, 0), (a, b)


def _bdot_bwd(res, g):
    a, b = res
    return _dg(g, b, 1, 1), _dg(a, g, 0, 0)


_bdot.defvjp(_bdot_fwd, _bdot_bwd)


@jax.custom_vjp
def _bdot_nt(a, b):
    return _dg(a, b, 1, 1)


def _bdot_nt_fwd(a, b):
    return _dg(a, b, 1, 1), (a, b)


def _bdot_nt_bwd(res, g):
    a, b = res
    return _dg(g, b, 1, 0), _dg(g, a, 0, 0)


_bdot_nt.defvjp(_bdot_nt_fwd, _bdot_nt_bwd)


def _hdot(a, b):
    return jnp.dot(a, b, precision=HIGHEST, preferred_element_type=F32)


def _shift_raw(x, k):
    n = x.shape[0]
    rows = lax.broadcasted_iota(jnp.int32, x.shape, 0)
    r = pltpu.roll(x, k % n, 0)
    if k > 0:
        return jnp.where(rows >= k, r, 0.0)
    return jnp.where(rows < n + k, r, 0.0)


@functools.partial(jax.custom_vjp, nondiff_argnums=(1,))
def _shift(x, k):
    return _shift_raw(x, k)


def _shift_fwd(x, k):
    return _shift_raw(x, k), None


def _shift_bwd(k, _, g):
    return (_shift(g, -k),)


_shift.defvjp(_shift_fwd, _shift_bwd)


def _col(x, j):
    lanes = lax.broadcasted_iota(jnp.int32, x.shape, x.ndim - 1)
    return jnp.sum(jnp.where(lanes == j, x, 0.0), axis=-1, keepdims=True)


def _rms(x, g):
    return x * lax.rsqrt(jnp.mean(x * x, axis=-1, keepdims=True) + EPS) * g


def _sigmoid(x):
    return 1.0 / (1.0 + jnp.exp(-x))


def _silu(x):
    return x * _sigmoid(x)


def _sigmoid_fast(x):
    return pl.reciprocal(1.0 + jnp.exp(-x), approx=True)


def _softplus(x):
    return jnp.maximum(x, 0.0) + jnp.log(1.0 + jnp.exp(-jnp.abs(x)))


def _gelu(x):
    c = math.sqrt(2.0 / math.pi)
    return 0.5 * x * (1.0 + jnp.tanh(c * (x + 0.044715 * (x * x * x))))


def _first(axes):
    c = None
    for a in axes:
        t = pl.program_id(a) == 0
        c = t if c is None else jnp.logical_and(c, t)
    return c


def _fwd_call(name, fn, grid, ins, outs, acc_axes=()):
    nin = len(ins)

    def body(*refs):
        pids = tuple(pl.program_id(a) for a in range(len(grid)))
        first = _first(acc_axes) if acc_axes else None
        vals = [r[...].astype(F32) for r in refs[:nin]]
        res = fn(pids, *vals)
        for r, o, (_, _, acc) in zip(refs[nin:], res, outs):
            if acc:
                @pl.when(first)
                def _(r=r):
                    r[...] = jnp.zeros_like(r)
                r[...] += o.astype(r.dtype)
            else:
                r[...] = o.astype(r.dtype)

    return _pcall(body, grid=grid, in_specs=[s for _, s in ins], out_specs=[s for _, s, _ in outs],
                  out_shape=[s for s, _, _ in outs], name=name,
                  compiler_params=_params(len(grid)))(*[a for a, _ in ins])


def _bwd_call(name, fn, grid, ins, cts, gouts, acc_axes=(), addends=()):
    nin, nct, nadd = len(ins), len(cts), len(addends)

    def body(*refs):
        pids = tuple(pl.program_id(a) for a in range(len(grid)))
        first = _first(acc_axes) if acc_axes else None
        vals = [r[...].astype(F32) for r in refs[:nin]]
        ctv = tuple(r[...].astype(F32) for r in refs[nin:nin + nct])
        addv = {pos: refs[nin + nct + n][...].astype(F32) for n, (pos, _, _) in enumerate(addends)}
        _, vjp = jax.vjp(lambda *v: tuple(fn(pids, *v)), *vals)
        grads = vjp(ctv)
        for pos, (r, (idx, _, _, acc)) in enumerate(zip(refs[nin + nct + nadd:], gouts)):
            gval = grads[idx]
            if pos in addv:
                gval = gval + addv[pos]
            if acc:
                @pl.when(first)
                def _(r=r):
                    r[...] = jnp.zeros_like(r)
                r[...] += gval.astype(r.dtype)
            else:
                r[...] = gval.astype(r.dtype)

    args = [a for a, _ in ins] + [a for a, _ in cts] + [a for _, a, _ in addends]
    specs = [s for _, s in ins] + [s for _, s in cts] + [s for _, _, s in addends]
    return _pcall(body, grid=grid, in_specs=specs, out_specs=[s for _, _, s, _ in gouts],
                  out_shape=[s for _, s, _, _ in gouts], name=name,
                  compiler_params=_params(len(grid)))(*args)


def _mm(name, a, b, *, ta=False, tb=False, res=None, tm=512, tn=None, tk=None):
    if ta:
        K, M = a.shape
    else:
        M, K = a.shape
    N = b.shape[0] if tb else b.shape[1]
    tm, tn, tk = _tile(M, tm, 8), N if tn is None else tn, K if tk is None else min(tk, K)
    nk = K // tk
    assert res is None or nk == 1
    a_spec = pl.BlockSpec((tk, tm), lambda i, j, k: (k, i)) if ta else pl.BlockSpec((tm, tk), lambda i, j, k: (i, k))
    b_spec = pl.BlockSpec((tn, tk), lambda i, j, k: (j, k)) if tb else pl.BlockSpec((tk, tn), lambda i, j, k: (k, j))
    o_spec = pl.BlockSpec((tm, tn), lambda i, j, k: (i, j))
    ca, cb = (0 if ta else 1), (1 if tb else 0)

    def body(*refs):
        a_ref, b_ref, o_ref = refs[0], refs[1], refs[-1]
        part = _dg(a_ref[...], b_ref[...], ca, cb)
        if nk == 1:
            o_ref[...] = part if res is None else refs[2][...] + part
        else:
            @pl.when(pl.program_id(2) == 0)
            def _():
                o_ref[...] = jnp.zeros_like(o_ref)

            o_ref[...] += part

    args, specs = [a, b], [a_spec, b_spec]
    if res is not None:
        args.append(res)
        specs.append(o_spec)
    return _pcall(body, grid=(M // tm, N // tn, nk), in_specs=specs, out_specs=o_spec, out_shape=_sds((M, N)),
                  name=name, compiler_params=_params(3))(*args)


FFN_PARTS = 2


def _ffn_fwd(name, h, g, wg, wu, wd, l, tm):
    T, D = h.shape
    ns, _, _, Fs = wg.shape

    def body(h_ref, g_ref, wg_ref, wu_ref, wd_ref, ho_ref, gate_ref, up_ref, hn_s, acc_s):
        s = pl.program_id(1)

        @pl.when(s == 0)
        def _():
            hn_s[...] = _rms(h_ref[...], g_ref[...]).astype(BF16)
            acc_s[...] = jnp.zeros_like(acc_s)

        parts = [pl.ds(k * (tm // FFN_PARTS), tm // FFN_PARTS) for k in range(FFN_PARTS)]
        gu = [(_dg(hn_s[r, :], wg_ref[...], 1, 0), _dg(hn_s[r, :], wu_ref[...], 1, 0)) for r in parts]
        for r, (gate, up) in zip(parts, gu):
            gate_ref[r, :] = gate.astype(BF16)
            up_ref[r, :] = up.astype(BF16)
            act = (gate * _sigmoid_fast(gate) * up).astype(BF16)
            acc_s[r, :] += _dg(act, wd_ref[...], 1, 0)

        @pl.when(s == ns - 1)
        def _():
            ho_ref[...] = h_ref[...] + 0.5 * acc_s[...]

    row = pl.BlockSpec((tm, D), lambda i, s: (i, 0))
    wcol = pl.BlockSpec((None, None, D, Fs), lambda i, s: (s, l, 0, 0))
    wrow = pl.BlockSpec((None, None, Fs, D), lambda i, s: (s, l, 0, 0))
    slot = pl.BlockSpec((None, tm, Fs), lambda i, s: (s, i, 0))
    return _pcall(body, grid=(T // tm, ns), in_specs=[row, pl.BlockSpec((1, D), lambda i, s: (0, 0)), wcol, wcol, wrow],
                  out_specs=[row, slot, slot],
                  out_shape=[_sds((T, D)), _sds((ns, T, Fs), BF16), _sds((ns, T, Fs), BF16)],
                  scratch_shapes=[pltpu.VMEM((tm, D), BF16), pltpu.VMEM((tm, D), F32)], name=name,
                  compiler_params=_params(2))(h, g, wg, wu, wd)


def _ffn_bwd_x(name, dh, h, g, gate, up, wg, wu, wd, l, tm):
    T, D = h.shape
    ns, _, _, Fs = wg.shape

    def body(dh_ref, h_ref, g_ref, gate_ref, up_ref, wg_ref, wu_ref, wd_ref,
             dho_ref, dgate_ref, dup_ref, dgain_ref, do_s, acc_s):
        i, s = pl.program_id(0), pl.program_id(1)

        @pl.when(s == 0)
        def _():
            do_s[...] = (0.5 * dh_ref[...]).astype(BF16)
            acc_s[...] = jnp.zeros_like(acc_s)

        @pl.when(jnp.logical_and(i == 0, s == 0))
        def _():
            dgain_ref[...] = jnp.zeros_like(dgain_ref)

        parts = [pl.ds(k * (tm // FFN_PARTS), tm // FFN_PARTS) for k in range(FFN_PARTS)]
        dact = [_dg(do_s[r, :], wd_ref[...], 1, 1) for r in parts]
        for r, da in zip(parts, dact):
            gt, u = gate_ref[r, :].astype(F32), up_ref[r, :].astype(F32)
            sg = _sigmoid_fast(gt)
            dgt = (da * u * (sg * (1.0 + gt * (1.0 - sg)))).astype(BF16)
            du = (da * (gt * sg)).astype(BF16)
            dgate_ref[r, :] = dgt
            dup_ref[r, :] = du
            acc_s[r, :] += _dg(dgt, wg_ref[...], 1, 1) + _dg(du, wu_ref[...], 1, 1)

        @pl.when(s == ns - 1)
        def _():
            _, vjp = jax.vjp(_rms, h_ref[...], g_ref[...])
            dx, dgain = vjp(acc_s[...])
            dho_ref[...] = dh_ref[...] + dx
            dgain_ref[...] += dgain

    row = pl.BlockSpec((tm, D), lambda i, s: (i, 0))
    gain = pl.BlockSpec((1, D), lambda i, s: (0, 0))
    wcol = pl.BlockSpec((None, None, D, Fs), lambda i, s: (s, l, 0, 0))
    wrow = pl.BlockSpec((None, None, Fs, D), lambda i, s: (s, l, 0, 0))
    slot = pl.BlockSpec((None, tm, Fs), lambda i, s: (s, i, 0))
    return _pcall(body, grid=(T // tm, ns), in_specs=[row, row, gain, slot, slot, wcol, wcol, wrow],
                  out_specs=[row, slot, slot, gain],
                  out_shape=[_sds((T, D)), _sds((ns, T, Fs), BF16), _sds((ns, T, Fs), BF16), _sds((1, D))],
                  scratch_shapes=[pltpu.VMEM((tm, D), BF16), pltpu.VMEM((tm, D), F32)], name=name,
                  compiler_params=_params(2))(dh, h, g, gate, up, wg, wu, wd)


def _ffn_bwd_w(name, dh, h, g, gate, up, dgate, dup, bufs, l, nl, tm):
    T, D = h.shape
    ns, _, Fs = gate.shape

    def body(dh_ref, h_ref, g_ref, gate_ref, up_ref, dgate_ref, dup_ref, dwg_ref, dwu_ref, dwd_ref):
        @pl.when(pl.program_id(1) == 0)
        def _():
            dwg_ref[...] = jnp.zeros_like(dwg_ref)
            dwu_ref[...] = jnp.zeros_like(dwu_ref)
            dwd_ref[...] = jnp.zeros_like(dwd_ref)

        def prep(r):
            hn = _rms(h_ref[r, :], g_ref[...]).astype(BF16)
            gt = gate_ref[r, :].astype(F32)
            act = (gt * _sigmoid_fast(gt) * up_ref[r, :].astype(F32)).astype(BF16)
            return hn, act, (0.5 * dh_ref[r, :]).astype(BF16)

        parts = [pl.ds(k * (tm // FFN_PARTS), tm // FFN_PARTS) for k in range(FFN_PARTS)]
        ready = [prep(r) for r in parts]
        add = lambda xs: functools.reduce(lambda p, q: p + q, xs)
        dwg_ref[...] += add([_dg(hn, dgate_ref[r, :], 0, 0) for r, (hn, _, _) in zip(parts, ready)])
        dwu_ref[...] += add([_dg(hn, dup_ref[r, :], 0, 0) for r, (hn, _, _) in zip(parts, ready)])
        dwd_ref[...] += add([_dg(act, do, 0, 0) for _, act, do in ready])

    row = pl.BlockSpec((tm, D), lambda s, j: (j, 0))
    gain = pl.BlockSpec((1, D), lambda s, j: (0, 0))
    slot = pl.BlockSpec((None, tm, Fs), lambda s, j: (s, j, 0))
    wcol = pl.BlockSpec((None, None, D, Fs), lambda s, j: (l, s, 0, 0))
    wrow = pl.BlockSpec((None, None, Fs, D), lambda s, j: (l, s, 0, 0))
    args, specs, alias = [dh, h, g, gate, up, dgate, dup], [row, row, gain, slot, slot, slot, slot], {}
    if bufs is not None:
        alias = {len(args) + k: k for k in range(3)}
        args, specs = args + list(bufs), specs + [pl.BlockSpec(memory_space=pl.ANY)] * 3
    return _pcall(lambda *refs: body(*refs[:7], *refs[-3:]), grid=(ns, T // tm), in_specs=specs,
                  out_specs=[wcol, wcol, wrow], input_output_aliases=alias,
                  out_shape=[_sds((nl, ns, D, Fs)), _sds((nl, ns, D, Fs)), _sds((nl, ns, Fs, D))], name=name,
                  compiler_params=_params(2))(*args)


def _rms_fn(pids, x, g):
    return (_rms(x, g),)


def _pool_fn(pids, a, w, scale):
    rows = lax.broadcasted_iota(jnp.int32, (a.shape[0], LANES), 0)
    outs = []
    for gi, win in enumerate(POOL_WINDOWS):
        ag = a[:, gi * LANES:(gi + 1) * LANES]
        s, k = ag, 1
        while k < win:
            s = s + _shift(s, k)
            k *= 2
        cnt = jnp.minimum(rows + 1, win).astype(F32)
        pooled = s / cnt - ag
        outs.append(_bdot(pooled, w[gi]) * scale[:, gi * LANES:(gi + 1) * LANES])
    return (jnp.concatenate(outs, axis=-1),)


def _conv_taps(x, cw):
    K = cw.shape[0]
    y = cw[K - 1] * x
    for j in range(K - 1):
        y = y + cw[j] * _shift(x, K - 1 - j)
    return y


def _prep_fn(pids, q, k, v, bg, cwq, cwk, cwv, alog, dtb):
    hd = pids[0]
    q, k, v = _silu(_conv_taps(q, cwq)), _silu(_conv_taps(k, cwk)), _silu(_conv_taps(v, cwv))
    q = q * lax.rsqrt(jnp.sum(q * q, axis=-1, keepdims=True) + EPS) * (LANES ** -0.5)
    k = k * lax.rsqrt(jnp.sum(k * k, axis=-1, keepdims=True) + EPS)
    beta = _sigmoid(_col(bg, hd))
    g = -jnp.exp(_col(alog, hd)) * _softplus(_col(bg, hd + 4) + _col(dtb, hd))
    return q, k, v, jnp.broadcast_to(beta, q.shape), jnp.broadcast_to(g, q.shape)


def _post_fn(pids, o, z, onorm):
    return (_rms(o, onorm) * _silu(z),)


def _sgu_fn(pids, u, v, g, b, ws, bias_t):
    u, v = _gelu(u), _gelu(v)
    mu = jnp.mean(v, axis=-1, keepdims=True)
    xc = v - mu
    vn = xc * lax.rsqrt(jnp.mean(xc * xc, axis=-1, keepdims=True) + EPS) * g + b
    r = lax.broadcasted_iota(jnp.int32, (LANES, LANES), 0)
    c = lax.broadcasted_iota(jnp.int32, (LANES, LANES), 1)
    rows = []
    for n in range(u.shape[0] // LANES):
        heads = []
        for hd in range(4):
            wm = jnp.where(r >= c, ws[hd], 0.0)
            blk = vn[n * LANES:(n + 1) * LANES, hd * LANES:(hd + 1) * LANES]
            heads.append(_bdot(wm, blk) + _col(bias_t, hd))
        rows.append(jnp.concatenate(heads, axis=-1))
    mixed = jnp.concatenate(rows, axis=0) if len(rows) > 1 else rows[0]
    return (u * mixed,)


def _sconv_fn(pids, xd, bgate, cg, cw):
    return (bgate * _conv_taps(cg * xd, cw),)


def _loss_fn(pids, h, g, tgt):
    err = _rms(h, g) - tgt
    tot = 0.5 * jnp.sum(jnp.mean(err * err, axis=-1, keepdims=True), axis=0, keepdims=True)
    return (jnp.broadcast_to(tot, (1, LANES)),)


DELTA_PAIRS = 2
DELTA_ROWS = DELTA_PAIRS * PAIR


def _each(fn, *lists):
    return [fn(*args) for args in zip(*lists)]


def _chunk_cumsum(g):
    pos = jnp.bitwise_and(lax.broadcasted_iota(jnp.int32, g.shape, 0), CH - 1)
    s, k = g, 1
    while k < CH:
        s = s + jnp.where(pos >= k, _shift(s, k), 0.0)
        k *= 2
    tot = [jnp.broadcast_to(jnp.sum(g[j * CH:(j + 1) * CH], axis=0, keepdims=True), (CH, g.shape[1])) for j in (0, 1)]
    return s, jnp.concatenate(tot, axis=0)


def _stage_a(blocks):
    q, k, v, beta, g = (list(x) for x in zip(*blocks))
    r = lax.broadcasted_iota(jnp.int32, (PAIR, PAIR), 0)
    c = lax.broadcasted_iota(jnp.int32, (PAIR, PAIR), 1)
    same = jnp.right_shift(r, 6) == jnp.right_shift(c, 6)
    tri = jnp.logical_and(same, r >= c)
    strict = jnp.logical_and(same, r > c)
    eye = (r == c).astype(F32)
    sums = _each(_chunk_cumsum, g)
    gc, gt = [a for a, _ in sums], [b for _, b in sums]
    gamma = _each(lambda x: jnp.exp(jnp.where(tri, x - x.T, -jnp.inf)), gc)
    kb = _each(lambda a, b: a * b, k, beta)
    kk = _each(_bdot_nt, kb, k)
    p = _each(lambda a, gm: -jnp.where(strict, a * gm, 0.0), kk, gamma)
    tinv = _each(lambda x: eye + x, p)
    for _ in range(5):
        p = _each(_bdot, p, p)
        tinv = _each(lambda t, x: t + x, tinv, _each(_bdot, tinv, p))
    egc = _each(jnp.exp, gc)
    u = _each(_bdot, tinv, _each(lambda a, b: a * b, v, beta))
    w = _each(_bdot, tinv, _each(lambda a, e: a * e, kb, egc))
    a = _each(lambda x, gm: x * gm, _each(_bdot_nt, q, k), gamma)
    qd = _each(lambda x, e: x * e, q, egc)
    kd = _each(lambda x, t, s: x * jnp.exp(t - s), k, gt, gc)
    return list(zip(u, w, qd, a, kd, _each(jnp.exp, gt)))


def _nn(a, b):
    return _dg(a, b, 1, 0)


def _nt(a, b):
    return _dg(a, b, 1, 1)


def _tn(a, b):
    return _dg(a, b, 0, 0)


def _scan_pair_fwd(s0, u, w, qd, a, kd, l0, l1):
    lo = lambda xs: [x[:CH] for x in xs]
    hi = lambda xs: [x[CH:] for x in xs]
    vn0 = _each(lambda x, y: x - y, lo(u), _each(_nn, lo(w), s0))
    s1 = _each(lambda s, l, x: s * l + x, s0, l0, _each(_tn, lo(kd), vn0))
    vn1 = _each(lambda x, y: x - y, hi(u), _each(_nn, hi(w), s1))
    s2 = _each(lambda s, l, x: s * l + x, s1, l1, _each(_tn, hi(kd), vn1))
    inter = _each(lambda x, y: jnp.concatenate([x, y], axis=0), _each(_nn, lo(qd), s0), _each(_nn, hi(qd), s1))
    intra = _each(_nn, a, _each(lambda x, y: jnp.concatenate([x, y], axis=0), vn0, vn1))
    return _each(lambda x, y: x + y, inter, intra), s1, s2


def _scan_chunk_bwd(s, ds_next, do, dvn_o, u, w, qd, kd, lrow):
    vn = _each(lambda x, y: x - y, u, _each(_nn, w, s))
    dvn = _each(lambda x, y: x + y, dvn_o, _each(_nn, kd, ds_next))
    dkd = _each(_nt, vn, ds_next)
    dl = _each(lambda a, b: jnp.sum(a * b, axis=0, keepdims=True), ds_next, s)
    dw = _each(lambda x: -x, _each(_nt, dvn, s))
    dqd = _each(_nt, do, s)
    ds = _each(lambda x, l, d, y: x + l * d - y, _each(_tn, qd, do), lrow, ds_next, _each(_tn, w, dvn))
    return ds, vn, dvn, dw, dqd, dkd, dl


def _delta_blocks(refs):
    return [tuple(r[p * PAIR:(p + 1) * PAIR, h * LANES:(h + 1) * LANES] for r in refs)
            for p in range(DELTA_PAIRS) for h in range(4)]


def _delta_specs(B, S, reverse):
    assert S % DELTA_ROWS == 0, (S, DELTA_ROWS)
    nstep = S // DELTA_ROWS
    at =(lambda i: nstep - 1 - i) if reverse else (lambda i: i)
    blk = pl.BlockSpec((DELTA_ROWS, 4 * LANES), lambda b, i: (b * nstep + at(i), 0))
    st = pl.BlockSpec((None, 4, 2 * DELTA_PAIRS, LANES, LANES), lambda b, i: (b, 0, at(i), 0, 0))
    scratch = [pltpu.VMEM((4, LANES, LANES), F32), pltpu.VMEM((4 * DELTA_PAIRS, PAIR, LANES), F32)]
    return nstep, blk, st, scratch


def _delta_fwd(name, qn, kn, vn, beta, g, B, S):
    T = qn.shape[0]
    nstep, blk, st, scratch = _delta_specs(B, S, False)

    def body(q_ref, k_ref, v_ref, b_ref, g_ref, o_ref, st_ref, s_s, l_s):
        @pl.when(pl.program_id(1) == 0)
        def _():
            s_s[...] = jnp.zeros_like(s_s)

        outs = _stage_a(_delta_blocks((q_ref, k_ref, v_ref, b_ref, g_ref)))
        for j, blk_out in enumerate(outs):
            l_s[j] = blk_out[5]
        state = [s_s[h] for h in range(4)]
        for p in range(DELTA_PAIRS):
            u, w, qd, a, kd, _ = (list(x) for x in zip(*outs[4 * p:4 * p + 4]))
            l0 = [l_s[4 * p + h, 0:1, :] for h in range(4)]
            l1 = [l_s[4 * p + h, CH:CH + 1, :] for h in range(4)]
            o, mid, end = _scan_pair_fwd(state, u, w, qd, a, kd, l0, l1)
            for h in range(4):
                o_ref[p * PAIR:(p + 1) * PAIR, h * LANES:(h + 1) * LANES] = o[h]
                st_ref[h, 2 * p] = state[h]
                st_ref[h, 2 * p + 1] = mid[h]
            state = end
        for h in range(4):
            s_s[h] = state[h]

    return _pcall(body, grid=(B, nstep), in_specs=[blk] * 5, out_specs=[blk, st],
                  out_shape=[_sds((T, 4 * LANES)), _sds((B, 4, S // CH, LANES, LANES))],
                  scratch_shapes=scratch, name=name, compiler_params=_params(2))(qn, kn, vn, beta, g)


def _delta_bwd(name, qn, kn, vn, beta, g, states, do, B, S):
    T = qn.shape[0]
    nstep, blk, st, scratch = _delta_specs(B, S, True)

    def body(q_ref, k_ref, v_ref, b_ref, g_ref, st_ref, do_ref, dq_ref, dk_ref, dv_ref, db_ref, dg_ref, ds_s, l_s):
        @pl.when(pl.program_id(1) == 0)
        def _():
            ds_s[...] = jnp.zeros_like(ds_s)

        rowid = lax.broadcasted_iota(jnp.int32, (PAIR, LANES), 0)
        lo = lambda xs: [x[:CH] for x in xs]
        hi = lambda xs: [x[CH:] for x in xs]
        cat = lambda xs, ys: _each(lambda x, y: jnp.concatenate([x, y], axis=0), xs, ys)
        outs, vjp = jax.vjp(_stage_a, _delta_blocks((q_ref, k_ref, v_ref, b_ref, g_ref)))
        for j, blk_out in enumerate(outs):
            l_s[j] = blk_out[5]
        ds = [ds_s[h] for h in range(4)]
        cts = [None] * (4 * DELTA_PAIRS)
        for p in reversed(range(DELTA_PAIRS)):
            u, w, qd, a, kd, _ = (list(x) for x in zip(*outs[4 * p:4 * p + 4]))
            l0 = [l_s[4 * p + h, 0:1, :] for h in range(4)]
            l1 = [l_s[4 * p + h, CH:CH + 1, :] for h in range(4)]
            s0 = [st_ref[h, 2 * p] for h in range(4)]
            s1 = [st_ref[h, 2 * p + 1] for h in range(4)]
            dout = [do_ref[p * PAIR:(p + 1) * PAIR, h * LANES:(h + 1) * LANES] for h in range(4)]
            dvn_o = _each(_tn, a, dout)
            ds1, vn1, dvn1, dw1, dqd1, dkd1, dl1 = _scan_chunk_bwd(s1, ds, hi(dout), hi(dvn_o), hi(u), hi(w), hi(qd),
                                                                   hi(kd), l1)
            ds, vn0, dvn0, dw0, dqd0, dkd0, dl0 = _scan_chunk_bwd(s0, ds1, lo(dout), lo(dvn_o), lo(u), lo(w), lo(qd),
                                                                  lo(kd), l0)
            da = _each(_nt, dout, cat(vn0, vn1))
            dl = _each(lambda x, y: jnp.where(rowid == 0, x, jnp.where(rowid == CH, y, 0.0)), dl0, dl1)
            for h, ct in enumerate(zip(cat(dvn0, dvn1), cat(dw0, dw1), cat(dqd0, dqd1), da, cat(dkd0, dkd1), dl)):
                cts[4 * p + h] = ct
        for h in range(4):
            ds_s[h] = ds[h]
        (grads,) = vjp(cts)
        for j, blk_grads in enumerate(grads):
            p, h = divmod(j, 4)
            for ref, val in zip((dq_ref, dk_ref, dv_ref, db_ref, dg_ref), blk_grads):
                ref[p * PAIR:(p + 1) * PAIR, h * LANES:(h + 1) * LANES] = val

    return _pcall(body, grid=(B, nstep), in_specs=[blk] * 5 + [st, blk], out_specs=[blk] * 5,
                  out_shape=[_sds((T, 4 * LANES))] * 5, scratch_shapes=scratch, name=name,
                  compiler_params=_params(2))(qn, kn, vn, beta, g, states, do)


ROW_TILE = 512
FFN_TILE = 1024
FFN_TILE_BX = 1024
SGU_ROWS = 256


def _rows(r, c, off=0):
    return pl.BlockSpec((r, c), lambda i: (i, off))


def _whole(shape, nd):
    zeros = (0,) * len(shape)
    if nd == 1:
        return pl.BlockSpec(shape, lambda i: zeros)
    return pl.BlockSpec(shape, lambda i, j: zeros)


def _norm_fwd(name, h, g):
    T, D = h.shape
    tr = _tile(T, ROW_TILE, 8)
    return _fwd_call(name, _rms_fn, (T // tr,), [(h, _rows(tr, D)), (g, _whole((1, D), 1))],
                     [(_sds((T, D), BF16), _rows(tr, D), False)])[0]


def _norm_bwd(name, h, g, dhn, dh_res):
    T, D = h.shape
    tr = _tile(T, ROW_TILE, 8)
    return _bwd_call(name, _rms_fn, (T // tr,), [(h, _rows(tr, D)), (g, _whole((1, D), 1))], [(dhn, _rows(tr, D))],
                     [(0, _sds((T, D)), _rows(tr, D), False), (1, _sds((1, D)), _whole((1, D), 1), True)],
                     acc_axes=(0,), addends=[(0, dh_res, _rows(tr, D))])


def _ab_specs(B, S, P):
    W4 = 4 * LANES
    seq4 = pl.BlockSpec((S, W4), lambda b: (b, 0))
    pool_ins = lambda proj: [(proj, seq4), (P["pool_w"], _whole((4, LANES, LANES), 1)), (P["pool_scale"], _whole((1, W4), 1))]
    hb = lambda off: pl.BlockSpec((S, LANES), lambda b, h: (b, off + h))
    cw = lambda off: pl.BlockSpec((4, 1, LANES), lambda b, h: (0, 0, off + h))
    small = _whole((1, LANES), 2)
    prep_ins = lambda proj: [(proj, hb(4)), (proj, hb(8)), (proj, hb(12)),
                             (proj, pl.BlockSpec((S, LANES), lambda b, h: (b, 20))),
                             (P["conv_w"], cw(0)), (P["conv_w"], cw(4)), (P["conv_w"], cw(8)),
                             (P["a_log"], small), (P["dt_bias"], small)]
    post_ins = lambda o, proj: [(o, hb(0)), (proj, hb(16)), (P["out_norm"], small)]
    return seq4, pool_ins, hb, prep_ins, post_ins, small


def _ab_fwd(tag, h, P, B, S):
    T, D = h.shape
    W4 = 4 * LANES
    seq4, pool_ins, hb, prep_ins, post_ins, small = _ab_specs(B, S, P)
    hn = _norm_fwd(tag + "_norm", h, P["mix_norm"])
    proj = _mm(tag + "_in", hn, P["w_in"], tm=256)
    ya = _fwd_call(tag + "_pool", _pool_fn, (B,), pool_ins(proj), [(_sds((T, W4), BF16), seq4, False)])[0]
    qn, kn, vn, beta, g = _fwd_call(tag + "_prep", lambda pids, *v: _prep_fn((pids[1],), *v), (B, 4), prep_ins(proj),
                                    [(_sds((T, W4)), hb(0), False)] * 5)
    o, states = _delta_fwd(tag + "_delta", qn, kn, vn, beta, g, B, S)
    yb = _fwd_call(tag + "_post", _post_fn, (B, 4), post_ins(o, proj), [(_sds((T, W4), BF16), hb(0), False)])[0]
    y = jnp.concatenate([ya, yb], axis=-1)
    h_new = _mm(tag + "_out", y, P["w_out"], res=h)
    return h_new, (h, hn, proj, qn, kn, vn, beta, g, states, o, y)


def _ab_bwd(tag, dh, saved, P, B, S):
    h, hn, proj, qn, kn, vn, beta, g, states, o, y = saved
    T, D = h.shape
    W4 = 4 * LANES
    seq4, pool_ins, hb, prep_ins, post_ins, small = _ab_specs(B, S, P)
    dy = _mm(tag + "_out_dx", dh, P["w_out"], tb=True)
    dw_out = _mm(tag + "_out_dw", y, dh, ta=True, tm=D, tk=1024)
    do, dz, d_onorm = _bwd_call(tag + "_post_b", _post_fn, (B, 4), post_ins(o, proj), [(dy, hb(4))],
                                [(0, _sds((T, W4)), hb(0), False), (1, _sds((T, W4)), hb(0), False),
                                 (2, _sds((1, LANES)), small, True)], acc_axes=(0, 1))
    dqn, dkn, dvn, dbeta, dg = _delta_bwd(tag + "_delta_b", qn, kn, vn, beta, g, states, do, B, S)
    per_b = pl.BlockSpec((None, 1, LANES), lambda b, h: (b, 0, 0))
    dcw = pl.BlockSpec((None, 4, 1, LANES), lambda b, h: (b, 0, 0, h))
    dq, dk, dv, dbg, dcq, dck, dcv, dalog, ddt = _bwd_call(
        tag + "_prep_b", lambda pids, *v: _prep_fn((pids[1],), *v), (B, 4), prep_ins(proj),
        [(dqn, hb(0)), (dkn, hb(0)), (dvn, hb(0)), (dbeta, hb(0)), (dg, hb(0))],
        [(0, _sds((T, W4)), hb(0), False), (1, _sds((T, W4)), hb(0), False), (2, _sds((T, W4)), hb(0), False),
         (3, _sds((T, LANES)), pl.BlockSpec((S, LANES), lambda b, h: (b, 0)), True),
         (4, _sds((B, 4, 1, W4)), dcw, False), (5, _sds((B, 4, 1, W4)), dcw, False),
         (6, _sds((B, 4, 1, W4)), dcw, False),
         (7, _sds((B, 1, LANES)), per_b, True), (8, _sds((B, 1, LANES)), per_b, True)], acc_axes=(1,))
    d_conv = jnp.concatenate([jnp.sum(d, axis=0)[:, 0, :] for d in (dcq, dck, dcv)], axis=-1)
    da, dpool_w, dpool_scale = _bwd_call(
        tag + "_pool_b", _pool_fn, (B,), pool_ins(proj), [(dy, seq4)],
        [(0, _sds((T, W4)), seq4, False), (1, _sds((4, LANES, LANES)), _whole((4, LANES, LANES), 1), True),
         (2, _sds((1, W4)), _whole((1, W4), 1), True)], acc_axes=(0,))
    dproj = jnp.concatenate([da, dq, dk, dv, dz, dbg], axis=-1)
    dhn = _mm(tag + "_in_dx", dproj, P["w_in"], tb=True, tm=256)
    dw_in = _mm(tag + "_in_dw", hn, dproj, ta=True, tm=D, tk=512)
    dh_new, d_mix = _norm_bwd(tag + "_norm_b", h, P["mix_norm"], dhn, dh)
    grads = dict(w_in=dw_in, w_out=dw_out, mix_norm=d_mix[0], pool_w=dpool_w, pool_scale=dpool_scale[0],
                 conv_w=d_conv, a_log=jnp.sum(dalog, axis=0)[0, :4], dt_bias=jnp.sum(ddt, axis=0)[0, :4],
                 out_norm=d_onorm[0])
    return dh_new, grads


def _cd_specs(B, S, T, P):
    W4 = 4 * LANES
    R = _tile(T, SGU_ROWS, LANES)
    sgu_ins = lambda proj: [(proj, _rows(R, W4, 0)), (proj, _rows(R, W4, 1)), (P["sgu_g"], _whole((1, W4), 1)),
                            (P["sgu_b"], _whole((1, W4), 1)), (P["sgu_w"], _whole((4, LANES, LANES), 1)),
                            (P["bias_t"], _whole((LANES, LANES), 1))]
    jb = lambda off: pl.BlockSpec((S, LANES), lambda j, b: (b, off + j))
    sc_ins = lambda proj: [(proj, jb(8)), (proj, jb(12)), (proj, jb(16)),
                           (P["sc_w"], pl.BlockSpec((3, 1, LANES), lambda j, b: (0, 0, j)))]
    return R, sgu_ins, jb, sc_ins


def _cd_fwd(tag, h, P, B, S):
    T, D = h.shape
    W4 = 4 * LANES
    R, sgu_ins, jb, sc_ins = _cd_specs(B, S, T, P)
    hn = _norm_fwd(tag + "_norm", h, P["mix_norm"])
    proj = _mm(tag + "_in", hn, P["w_in"], tm=256)
    yc = _fwd_call(tag + "_sgu", _sgu_fn, (T // R,), sgu_ins(proj), [(_sds((T, W4), BF16), _rows(R, W4), False)])[0]
    yd = _fwd_call(tag + "_sconv", _sconv_fn, (4, B), sc_ins(proj), [(_sds((T, W4), BF16), jb(0), False)])[0]
    y = jnp.concatenate([yc, yd], axis=-1)
    h_new = _mm(tag + "_out", y, P["w_out"], res=h)
    return h_new, (h, hn, proj, y)


def _cd_bwd(tag, dh, saved, P, B, S):
    h, hn, proj, y = saved
    T, D = h.shape
    W4 = 4 * LANES
    R, sgu_ins, jb, sc_ins = _cd_specs(B, S, T, P)
    dy = _mm(tag + "_out_dx", dh, P["w_out"], tb=True)
    dw_out = _mm(tag + "_out_dw", y, dh, ta=True, tm=D, tk=1024)
    du, dv, dsg, dsb, dsw, dbias_t = _bwd_call(
        tag + "_sgu_b", _sgu_fn, (T // R,), sgu_ins(proj), [(dy, _rows(R, W4, 0))],
        [(0, _sds((T, W4)), _rows(R, W4), False), (1, _sds((T, W4)), _rows(R, W4), False),
         (2, _sds((1, W4)), _whole((1, W4), 1), True), (3, _sds((1, W4)), _whole((1, W4), 1), True),
         (4, _sds((4, LANES, LANES)), _whole((4, LANES, LANES), 1), True),
         (5, _sds((LANES, LANES)), _whole((LANES, LANES), 1), True)], acc_axes=(0,))
    dxd, dbgate, dcg, dsc = _bwd_call(
        tag + "_sconv_b", _sconv_fn, (4, B), sc_ins(proj), [(dy, jb(4))],
        [(0, _sds((T, W4)), jb(0), False), (1, _sds((T, W4)), jb(0), False), (2, _sds((T, W4)), jb(0), False),
         (3, _sds((3, 1, W4)), pl.BlockSpec((3, 1, LANES), lambda j, b: (0, 0, j)), True)], acc_axes=(1,))
    dproj = jnp.concatenate([du, dv, dxd, dbgate, dcg], axis=-1)
    dhn = _mm(tag + "_in_dx", dproj, P["w_in"], tb=True, tm=256)
    dw_in = _mm(tag + "_in_dw", hn, dproj, ta=True, tm=D, tk=512)
    dh_new, d_mix = _norm_bwd(tag + "_norm_b", h, P["mix_norm"], dhn, dh)
    grads = dict(w_in=dw_in, w_out=dw_out, mix_norm=d_mix[0], sgu_g=dsg[0], sgu_b=dsb[0], sgu_w=dsw,
                 sgu_bias=dbias_t[:, :4].T, sc_w=dsc[:, 0, :])
    return dh_new, grads


def _loss_fwd_bwd(h, g, tgt):
    T, D = h.shape
    tr = _tile(T, ROW_TILE, 8)
    ins = [(h, _rows(tr, D)), (g, _whole((1, D), 1)), (tgt, _rows(tr, D))]
    vec = _whole((1, LANES), 1)
    loss = _fwd_call("loss", _loss_fn, (T // tr,), ins, [(_sds((1, LANES)), vec, True)], acc_axes=(0,))[0]
    one = jnp.zeros((1, LANES), F32).at[0, 0].set(1.0)
    dh, dg = _bwd_call("loss_b", _loss_fn, (T // tr,), ins, [(one, vec)],
                       [(0, _sds((T, D)), _rows(tr, D), False), (1, _sds((1, D)), _whole((1, D), 1), True)],
                       acc_axes=(0,))
    return loss[0, 0], dh, dg[0]


def _local_step(x2, tgt2, W, B, S):
    L = W["ffn1_norm"].shape[0]
    tm = _tile(x2.shape[0], FFN_TILE, 8)
    tmx = _tile(x2.shape[0], FFN_TILE_BX, 8)
    h = x2
    saved = []
    for l in range(L):
        e = l // 2
        f1 = (W["ffn1_norm"][l][None], W["ffn1_g"], W["ffn1_u"], W["ffn1_d"])
        f2 = (W["ffn2_norm"][l][None], W["ffn2_g"], W["ffn2_u"], W["ffn2_d"])
        h0 = h
        h, gate1, up1 = _ffn_fwd(f"l{l}_ffn1", h0, *f1, l, tm)
        if l % 2 == 0:
            P = dict(mix_norm=W["mix_norm"][l][None], w_in=W["ab_in"][e], w_out=W["ab_out"][e], pool_w=W["pool_w"][e],
                     pool_scale=W["pool_scale"][e][None], conv_w=W["dn_conv_w"][e][:, None, :],
                     a_log=jnp.pad(W["dn_a_log"][e][None], ((0, 0), (0, LANES - 4))),
                     dt_bias=jnp.pad(W["dn_dt_bias"][e][None], ((0, 0), (0, LANES - 4))),
                     out_norm=W["dn_out_norm"][e][None])
            h1 = h
            h, msave = _ab_fwd(f"l{l}_ab", h1, P, B, S)
        else:
            P = dict(mix_norm=W["mix_norm"][l][None], w_in=W["cd_in"][e], w_out=W["cd_out"][e],
                     sgu_g=W["sgu_norm_g"][e][None], sgu_b=W["sgu_norm_b"][e][None], sgu_w=W["sgu_w"][e],
                     bias_t=jnp.pad(W["sgu_bias"][e].T, ((0, 0), (0, LANES - 4))),
                     sc_w=W["sc_conv_w"][e][:, None, :])
            h1 = h
            h, msave = _cd_fwd(f"l{l}_cd", h1, P, B, S)
        h2 = h
        h, gate2, up2 = _ffn_fwd(f"l{l}_ffn2", h2, *f2, l, tm)
        saved.append((f1, f2, P, h0, gate1, up1, msave, h2, gate2, up2))

    loss, dh, d_final = _loss_fwd_bwd(h, W["final_norm"][None], tgt2)

    G = {k: [None] * L for k in ("ffn1_norm", "ffn2_norm", "mix_norm")}
    bufs1 = bufs2 = None
    GA, GC = [None] * ((L + 1) // 2), [None] * (L // 2)
    for l in reversed(range(L)):
        f1, f2, P, h0, gate1, up1, msave, h2, gate2, up2 = saved[l]
        dh_in = dh
        dh, dgate, dup, dn2 = _ffn_bwd_x(f"l{l}_ffn2_bx", dh_in, h2, f2[0], gate2, up2, f2[1], f2[2], f2[3], l, tmx)
        bufs2 = _ffn_bwd_w(f"l{l}_ffn2_bw", dh_in, h2, f2[0], gate2, up2, dgate, dup, bufs2, l, L, tm)
        G["ffn2_norm"][l] = dn2[0]
        if l % 2 == 0:
            dh, mg = _ab_bwd(f"l{l}_ab", dh, msave, P, B, S)
            GA[l // 2] = mg
        else:
            dh, mg = _cd_bwd(f"l{l}_cd", dh, msave, P, B, S)
            GC[l // 2] = mg
        G["mix_norm"][l] = mg["mix_norm"]
        dh_in = dh
        dh, dgate, dup, dn1 = _ffn_bwd_x(f"l{l}_ffn1_bx", dh_in, h0, f1[0], gate1, up1, f1[1], f1[2], f1[3], l, tmx)
        bufs1 = _ffn_bwd_w(f"l{l}_ffn1_bw", dh_in, h0, f1[0], gate1, up1, dgate, dup, bufs1, l, L, tm)
        G["ffn1_norm"][l] = dn1[0]

    st = lambda xs: jnp.stack(xs, axis=0)
    big = dict(ffn1_w_gate=bufs1[0], ffn1_w_up=bufs1[1], ffn1_w_down=bufs1[2],
               ffn2_w_gate=bufs2[0], ffn2_w_up=bufs2[1], ffn2_w_down=bufs2[2],
               ab_w_in=st([m["w_in"] for m in GA]), ab_w_out=st([m["w_out"] for m in GA]),
               cd_w_in=st([m["w_in"] for m in GC]), cd_w_out=st([m["w_out"] for m in GC]))
    small = dict(ffn1_norm=st(G["ffn1_norm"]), mix_norm=st(G["mix_norm"]), ffn2_norm=st(G["ffn2_norm"]),
                 pool_w=st([m["pool_w"] for m in GA]), pool_scale=st([m["pool_scale"] for m in GA]),
                 dn_conv_w=st([m["conv_w"] for m in GA]), dn_a_log=st([m["a_log"] for m in GA]),
                 dn_dt_bias=st([m["dt_bias"] for m in GA]), dn_out_norm=st([m["out_norm"] for m in GA]),
                 sgu_norm_g=st([m["sgu_g"] for m in GC]), sgu_norm_b=st([m["sgu_b"] for m in GC]),
                 sgu_w=st([m["sgu_w"] for m in GC]), sgu_bias=st([m["sgu_bias"] for m in GC]),
                 sc_conv_w=st([m["sc_w"] for m in GC]), final_norm=d_final)
    return loss, dh, big, small


ANY = pl.BlockSpec(memory_space=pl.ANY)


def _place():
    return lax.axis_index("x"), lax.axis_index("y"), lax.axis_index("c")


def _exchange(name, srcs, out_shapes, plan, n_copies, bases=None):
    n, m = len(srcs), len(out_shapes)
    nb = m if bases is not None else 0

    def body(*refs):
        ins, outs = refs[:n], refs[n + nb:n + nb + m]
        send, recv = refs[n + nb + m:]
        remote = plan(_place(), ins, outs)
        assert len(remote) == n_copies
        sends = []
        for k, (s, d, peer, _) in enumerate(remote):
            cp = pltpu.make_async_remote_copy(src_ref=s, dst_ref=d, send_sem=send.at[k], recv_sem=recv.at[k],
                                              device_id=peer, device_id_type=MESH)
            cp.start()
            sends.append(cp)
        for k, (s, _, peer, land) in enumerate(remote):
            pltpu.make_async_remote_copy(src_ref=s, dst_ref=land, send_sem=send.at[k], recv_sem=recv.at[k],
                                         device_id=peer, device_id_type=MESH).wait_recv()
        for cp in sends:
            cp.wait_send()

    return _pcall(body, in_specs=[ANY] * (n + nb), out_specs=[ANY] * m, out_shape=out_shapes,
                  input_output_aliases={n + k: k for k in range(nb)},
                  scratch_shapes=[pltpu.SemaphoreType.DMA((n_copies,)), pltpu.SemaphoreType.DMA((n_copies,))],
                  name=name)(*srcs, *(bases or ()))


def _other_chips(x, y):
    return [(1 - x, y), (x, 1 - y), (1 - x, 1 - y)]


def _gather_chips(name, xs):
    n = len(xs)
    split = [a.ndim >= 3 and a.shape[0] % 2 == 0 for a in xs]
    n_fwd = sum(split)
    fwd_of = {t: j for j, t in enumerate(t for t in range(n) if split[t])}

    def body(*refs):
        ins, outs = refs[:n], refs[n:2 * n]
        send0, recv0, send1, recv1, send2, recv2 = refs[2 * n:]
        x, y, c = _place()
        me = 2 * x + y
        chips = _other_chips(x, y)

        def mine(t):
            return pltpu.make_async_remote_copy(src_ref=ins[t], dst_ref=outs[t].at[me], send_sem=send0.at[t],
                                                recv_sem=recv0.at[t], device_id=(x, y, 1 - c), device_id_type=MESH)

        def part(t, cc):
            half = xs[t].shape[0] // 2
            return pl.ds(cc * half, half) if split[t] else pl.ds(0, xs[t].shape[0])

        def first(r, t, started):
            px, py = chips[r]
            dst = outs[t].at[me, part(t, c)] if started else outs[t].at[2 * px + py, part(t, c)]
            return pltpu.make_async_remote_copy(src_ref=ins[t].at[part(t, c)], dst_ref=dst, send_sem=send1.at[r, t],
                                                recv_sem=recv1.at[r, t], device_id=(px, py, c), device_id_type=MESH)

        def second(r, t, started):
            px, py = chips[r]
            rows = part(t, c) if started else part(t, 1 - c)
            blk = outs[t].at[2 * px + py, rows]
            return pltpu.make_async_remote_copy(src_ref=blk, dst_ref=blk, send_sem=send2.at[r, fwd_of[t]],
                                                recv_sem=recv2.at[r, fwd_of[t]], device_id=(x, y, 1 - c),
                                                device_id_type=MESH)

        sends = [first(r, t, True) for r in range(3) for t in range(n)] + [mine(t) for t in range(n)]
        for cp in sends:
            cp.start()
        for r in range(3):
            for t in range(n):
                first(r, t, False).wait_recv()
                if split[t]:
                    cp = second(r, t, True)
                    cp.start()
                    sends.append(cp)
        for r in range(3):
            for t in range(n):
                if split[t]:
                    second(r, t, False).wait_recv()
        for t in range(n):
            mine(t).wait_recv()
        for cp in sends:
            cp.wait_send()

    return _pcall(body, in_specs=[ANY] * n, out_specs=[ANY] * n,
                  out_shape=[_sds((N_CHIPS,) + a.shape, a.dtype) for a in xs],
                  scratch_shapes=[pltpu.SemaphoreType.DMA((n,)), pltpu.SemaphoreType.DMA((n,)),
                                  pltpu.SemaphoreType.DMA((3, n)), pltpu.SemaphoreType.DMA((3, n)),
                                  pltpu.SemaphoreType.DMA((3, max(n_fwd, 1))), pltpu.SemaphoreType.DMA((3, max(n_fwd, 1)))],
                  name=name)(*xs)


def _pair_split(name, gs):
    n = len(gs)

    def plan(place, ins, outs):
        x, y, c = place
        return [(ins[t].at[:, :, 1 - c], outs[t], (x, y, 1 - c), outs[t]) for t in range(n)]

    return _exchange(name, gs, [_sds(a.shape[:2] + a.shape[3:], a.dtype) for a in gs], plan, n)


def _chip_scatter(name, ps, bases):
    n = len(ps)

    def plan(place, ins, outs):
        x, y, c = place
        me = 2 * x + y
        return [(ins[t].at[:, 2 * px + py], outs[t].at[:, me], (px, py, c), outs[t].at[:, 2 * px + py])
                for (px, py) in _other_chips(x, y) for t in range(n)]

    return _exchange(name, ps, [_sds(a.shape, a.dtype) for a in ps], plan, 3 * n, bases)


def _pair_share(name, ts):
    n = len(ts)

    def plan(place, ins, outs):
        x, y, c = place
        return [(ins[t], outs[t], (x, y, 1 - c), outs[t]) for t in range(n)]

    return _exchange(name, ts, [_sds(a.shape, a.dtype) for a in ts], plan, n)


def _gather_devices(name, v, base):
    flips = [(fx, fy, fc) for fx in (0, 1) for fy in (0, 1) for fc in (0, 1)][1:]

    def plan(place, ins, outs):
        x, y, c = place
        me = 4 * x + 2 * y + c
        remote = []
        for fx, fy, fc in flips:
            px, py, pc = (1 - x if fx else x), (1 - y if fy else y), (1 - c if fc else c)
            remote.append((ins[0], outs[0].at[me], (px, py, pc), outs[0].at[4 * px + 2 * py + pc]))
        return remote

    return _exchange(name, [v], [_sds((8,) + v.shape, v.dtype)], plan, len(flips), [base])[0]


def _sum_slots(name, a):
    k, rows, cols = a.shape
    tr = _tile(rows, ROW_TILE, 8)
    ins = [(a, pl.BlockSpec((None, tr, cols), lambda i, j=j: (j, i, 0))) for j in range(k)]
    return _fwd_call(name, lambda pids, *v: (functools.reduce(lambda p, q: p + q, v),), (rows // tr,), ins,
                     [(_sds((rows, cols)), _rows(tr, cols), False)])[0]


def _half_tile(ah):
    return _tile(ah, 512, 8)


def _pair_sum(name, g5, theirs, cf, out_dtype):
    L, P4, _, Ah, Bt = g5.shape
    ta = _half_tile(Ah)
    half = lambda hh: pl.BlockSpec((None, None, None, ta, Bt), lambda l, p, i: (l, p, hh, i, 0))
    blk = pl.BlockSpec((None, None, ta, Bt), lambda l, p, i: (l, p, i, 0))

    def fn(pids, g0, g1, r, c):
        tot = jnp.where(jnp.max(c) > 0.5, g1, g0) + r
        return tot, tot

    return _fwd_call(name, fn, (L, P4, Ah // ta), [(g5, half(0)), (g5, half(1)), (theirs, blk),
                                                   (cf, pl.BlockSpec((1, LANES), lambda l, p, i: (0, 0)))],
                     [(_sds((L, P4, Ah, Bt), out_dtype), blk, False)] * 2)


def _sum_chips(name, parts):
    L, P4, Ah, Bt = parts.shape
    ta = _half_tile(Ah)
    ins = [(parts, pl.BlockSpec((None, None, ta, Bt), lambda l, i, q=q: (l, q, i, 0))) for q in range(P4)]
    return _fwd_call(name, lambda pids, *v: (functools.reduce(lambda p, q: p + q, v),), (L, Ah // ta), ins,
                     [(_sds((L, Ah, Bt)), pl.BlockSpec((None, ta, Bt), lambda l, i: (l, i, 0)), False)])[0]


def _adam_terms(w, g, m, v):
    m2 = ADAM_B1 * m + (1.0 - ADAM_B1) * g
    v2 = ADAM_B2 * v + (1.0 - ADAM_B2) * (g * g)
    m_hat = m2 / (1.0 - ADAM_B1 ** ADAM_STEP)
    v_hat = v2 / (1.0 - ADAM_B2 ** ADAM_STEP)
    return -ADAM_LR * (m_hat / (jnp.sqrt(v_hat) + ADAM_EPS) + ADAM_WD * w), m2, v2


def _adam_halves(name, w, m, v, mine, theirs, cf):
    L, Aa, Bt = w.shape
    Ah = Aa // 2
    ta = _half_tile(Ah)
    full = pl.BlockSpec((None, None, ta, Bt), lambda l, hh, i: (l, hh, i, 0))
    part = pl.BlockSpec((None, ta, Bt), lambda l, hh, i: (l, i, 0))

    def fn(pids, w_, m_, v_, a, b, c):
        g = jnp.where(jnp.max(c) == pids[1].astype(F32), a, b)
        return (g,) + _adam_terms(w_, g, m_, v_)

    ins = [(a.reshape(L, 2, Ah, Bt), full) for a in (w, m, v)] + [(mine, part), (theirs, part),
                                                                 (cf, pl.BlockSpec((1, LANES), lambda l, hh, i: (0, 0)))]
    outs = _fwd_call(name, fn, (L, 2, Ah // ta), ins, [(_sds((L, 2, Ah, Bt)), full, False)] * 4)
    return [o.reshape(w.shape) for o in outs]


def _adam_rows(name, w, g, m, v):
    rows, cols = w.shape
    tr = _tile(rows, ROW_TILE, 8)
    ins = [(a, _rows(tr, cols)) for a in (w, g, m, v)]
    return _fwd_call(name, lambda pids, *a: _adam_terms(*a), (rows // tr,), ins,
                     [(_sds((rows, cols)), _rows(tr, cols), False)] * 3)


def _pack(xs):
    rows = []
    for a in xs:
        flat = a.reshape(-1).astype(F32)
        rows.append(jnp.pad(flat, (0, (-flat.size) % LANES)).reshape(-1, LANES))
    buf = jnp.concatenate(rows, axis=0)
    return jnp.pad(buf, ((0, (-buf.shape[0]) % 8), (0, 0)))


def _unpack(buf, shapes):
    out, row = [], 0
    for s in shapes:
        n = math.prod(s)
        nr = -(-n // LANES)
        out.append(buf[row:row + nr].reshape(-1)[:n].reshape(s))
        row += nr
    return out


_WEIGHTS = ("ffn1_norm", "ffn1_w_gate", "ffn1_w_up", "ffn1_w_down", "mix_norm", "ffn2_norm", "ffn2_w_gate", "ffn2_w_up",
            "ffn2_w_down", "ab_w_in", "pool_w", "pool_scale", "dn_conv_w", "dn_a_log", "dn_dt_bias", "dn_out_norm",
            "ab_w_out", "cd_w_in", "sgu_norm_g", "sgu_norm_b", "sgu_w", "sgu_bias", "sc_conv_w", "cd_w_out", "final_norm")
_BIG = ("ffn1_w_gate", "ffn1_w_up", "ffn1_w_down", "ffn2_w_gate", "ffn2_w_up", "ffn2_w_down", "ab_w_in", "ab_w_out",
        "cd_w_in", "cd_w_out")
_ROW_SHARDED = ("ffn1_w_down", "ffn2_w_down", "ab_w_out", "cd_w_out")
_SMALL_SHARDED = ("dn_conv_w", "sgu_norm_g", "sgu_norm_b", "sc_conv_w")
_SMALL = tuple(n for n in _WEIGHTS if n not in _BIG)


def _to_slots(name, g):
    L, A, Bt = g.shape
    if name in _ROW_SHARDED:
        return g.reshape(L, N_CHIPS, A // N_CHIPS, Bt)
    return g.reshape(L, A, N_CHIPS, Bt // N_CHIPS).transpose(0, 2, 1, 3)


def kernel(x, ffn1_norm, ffn1_w_gate, ffn1_w_up, ffn1_w_down, mix_norm, ffn2_norm, ffn2_w_gate, ffn2_w_up, ffn2_w_down,
           ab_w_in, pool_w, pool_scale, dn_conv_w, dn_a_log, dn_dt_bias, dn_out_norm, ab_w_out, cd_w_in, sgu_norm_g,
           sgu_norm_b, sgu_w, sgu_bias, sc_conv_w, cd_w_out, final_norm, loss_target,
           m_ffn1_norm, m_ffn1_w_gate, m_ffn1_w_up, m_ffn1_w_down, m_mix_norm, m_ffn2_norm, m_ffn2_w_gate, m_ffn2_w_up,
           m_ffn2_w_down, m_ab_w_in, m_pool_w, m_pool_scale, m_dn_conv_w, m_dn_a_log, m_dn_dt_bias, m_dn_out_norm,
           m_ab_w_out, m_cd_w_in, m_sgu_norm_g, m_sgu_norm_b, m_sgu_w, m_sgu_bias, m_sc_conv_w, m_cd_w_out, m_final_norm,
           v_ffn1_norm, v_ffn1_w_gate, v_ffn1_w_up, v_ffn1_w_down, v_mix_norm, v_ffn2_norm, v_ffn2_w_gate, v_ffn2_w_up,
           v_ffn2_w_down, v_ab_w_in, v_pool_w, v_pool_scale, v_dn_conv_w, v_dn_a_log, v_dn_dt_bias, v_dn_out_norm,
           v_ab_w_out, v_cd_w_in, v_sgu_norm_g, v_sgu_norm_b, v_sgu_w, v_sgu_bias, v_sc_conv_w, v_cd_w_out, v_final_norm):
    A = dict(locals())
    B, S, D = x.shape
    T = B * S
    xi, yi, ci = _place()
    chip = 2 * xi + yi
    cf = jnp.broadcast_to(ci.astype(F32), (1, LANES))
    own = lambda n, mine, slot: lax.dynamic_update_index_in_dim(lax.empty((n,) + mine.shape, mine.dtype), mine, slot, 0)

    sh_shapes = [A[n].shape for n in _SMALL_SHARDED]
    local = [A[n].astype(BF16) for n in _BIG] + [_pack([A[n] for n in _SMALL_SHARDED])]
    gathered = _gather_chips("gather_weights", local)
    gw = dict(zip(_BIG, gathered[:-1]))
    per_chip = [_unpack(gathered[-1][p], sh_shapes) for p in range(N_CHIPS)]
    W = {n: A[n] for n in _SMALL if n not in _SMALL_SHARDED}
    for j, n in enumerate(_SMALL_SHARDED):
        W[n] = jnp.concatenate([per_chip[p][j] for p in range(N_CHIPS)], axis=-1)
    for f in ("ffn1", "ffn2"):
        W[f + "_g"], W[f + "_u"], W[f + "_d"] = gw[f + "_w_gate"], gw[f + "_w_up"], gw[f + "_w_down"]
    cols = lambda g: jnp.concatenate([g[p] for p in range(N_CHIPS)], axis=-1)
    rows = lambda g: jnp.concatenate([g[p] for p in range(N_CHIPS)], axis=1)
    ab_in = cols(gw["ab_w_in"])
    ab_cols = ab_in.shape[-1]
    ab_pad = (-ab_cols) % LANES
    W["ab_in"] = jnp.pad(ab_in, ((0, 0), (0, 0), (0, ab_pad)))
    W["cd_in"], W["ab_out"], W["cd_out"] = cols(gw["cd_w_in"]), rows(gw["ab_w_out"]), rows(gw["cd_w_out"])

    loss, dx, big, small = _local_step(x.reshape(T, D), loss_target.reshape(T, D), W, B, S)
    loss = lax.psum(loss, ("x", "y", "c"))

    big["ab_w_in"] = big["ab_w_in"][:, :, :ab_cols]
    slots = []
    for n in _BIG:
        g = big[n] if big[n].ndim == 4 else _to_slots(n, big[n])
        L, P4, Aa, Bt = g.shape
        slots.append(g.reshape(L, P4, 2, Aa // 2, Bt))
    theirs = _pair_split("grads_pair_split", slots)
    pair = [_pair_sum(f"grads_pair_sum_{n}", g5, r, cf, BF16) for n, g5, r in zip(_BIG, slots, theirs)]
    parts = _chip_scatter("grads_chip_scatter", [p for p, _ in pair], [p for _, p in pair])
    tot = [_sum_chips(f"grads_chip_sum_{n}", p) for n, p in zip(_BIG, parts)]
    other = _pair_share("grads_pair_share", tot)

    sm_shapes = [small[n].shape for n in _SMALL]
    sm_local = _pack([small[n] for n in _SMALL])
    every = _gather_devices("gather_small_grads", sm_local, own(8, sm_local, 2 * chip + ci))
    grads = {}
    for n, g in zip(_SMALL, _unpack(_sum_slots("small_grads_sum", every), sm_shapes)):
        if n in _SMALL_SHARDED:
            w_loc = A[n].shape[-1]
            g = lax.dynamic_slice_in_dim(g, chip * w_loc, w_loc, axis=g.ndim - 1)
        grads[n] = g.reshape(A[n].shape)

    delta, new_m, new_v = {}, {}, {}
    for n, a, b in zip(_BIG, tot, other):
        grads[n], delta[n], new_m[n], new_v[n] = _adam_halves(f"adam_{n}", A[n], A["m_" + n], A["v_" + n], a, b, cf)
    packed = [_pack([d[n] for n in _SMALL]) for d in (A, grads, {n: A["m_" + n] for n in _SMALL},
                                                       {n: A["v_" + n] for n in _SMALL})]
    loc_shapes = [A[n].shape for n in _SMALL]
    for d, buf in zip((delta, new_m, new_v), _adam_rows("adam_small", *packed)):
        d.update(zip(_SMALL, _unpack(buf, loc_shapes)))

    return (loss, dx.reshape(B, S, D), *[grads[n] for n in _WEIGHTS], *[delta[n] for n in _WEIGHTS],
            *[new_m[n] for n in _WEIGHTS], *[new_v[n] for n in _WEIGHTS])
```

```python
import functools
import math

import jax
import jax.numpy as jnp
from jax import lax
from jax.experimental import pallas as pl
from jax.experimental.pallas import tpu as pltpu

F32, BF16 = jnp.float32, jnp.bfloat16
HIGHEST = lax.Precision.HIGHEST
MESH = pl.DeviceIdType.MESH

EPS = 1e-6
LANES = 128
CH = 64
PAIR = 2 * CH
POOL_WINDOWS = (2, 4, 8, 16)
N_CHIPS = 4
ADAM_LR, ADAM_B1, ADAM_B2, ADAM_EPS, ADAM_WD, ADAM_STEP = 0.001, 0.9, 0.999, 1e-08, 0.01, 10
VMEM_LIMIT = 63 * 1024 * 1024


def _pcall(body, **kw):
    return pl.pallas_call(body, **kw)


def _params(nd):
    return pltpu.CompilerParams(dimension_semantics=("arbitrary",) * nd, vmem_limit_bytes=VMEM_LIMIT)


def _sds(shape, dtype=F32):
    return jax.ShapeDtypeStruct(tuple(shape), dtype)


def _tile(n, cap, mult=LANES):
    if n <= cap:
        return n
    best = None
    for t in range(mult, cap + 1, mult):
        if n % t == 0:
            best = t
    assert best is not None, (n, cap)
    return best


def _dg(a, b, ca, cb):
    return lax.dot_general(a.astype(BF16), b.astype(BF16), (((ca,), (cb,)), ((), ())),
                           preferred_element_type=F32)


@jax.custom_vjp
def _bdot(a, b):
    return _dg(a, b, 1, 0)


def _bdot_fwd(a, b):
    return _dg(a, b, 1, 0), (a, b)


def _bdot_bwd(res, g):
    a, b = res
    return _dg(g, b, 1, 1), _dg(a, g, 0, 0)


_bdot.defvjp(_bdot_fwd, _bdot_bwd)


@jax.custom_vjp
def _bdot_nt(a, b):
    return _dg(a, b, 1, 1)


def _bdot_nt_fwd(a, b):
    return _dg(a, b, 1, 1), (a, b)


def _bdot_nt_bwd(res, g):
    a, b = res
    return _dg(g, b, 1, 0), _dg(g, a, 0, 0)


_bdot_nt.defvjp(_bdot_nt_fwd, _bdot_nt_bwd)


def _hdot(a, b):
    return jnp.dot(a, b, precision=HIGHEST, preferred_element_type=F32)


def _shift_raw(x, k):
    n = x.shape[0]
    rows = lax.broadcasted_iota(jnp.int32, x.shape, 0)
    r = pltpu.roll(x, k % n, 0)
    if k > 0:
        return jnp.where(rows >= k, r, 0.0)
    return jnp.where(rows < n + k, r, 0.0)


@functools.partial(jax.custom_vjp, nondiff_argnums=(1,))
def _shift(x, k):
    return _shift_raw(x, k)


def _shift_fwd(x, k):
    return _shift_raw(x, k), None


def _shift_bwd(k, _, g):
    return (_shift(g, -k),)


_shift.defvjp(_shift_fwd, _shift_bwd)


def _col(x, j):
    lanes = lax.broadcasted_iota(jnp.int32, x.shape, x.ndim - 1)
    return jnp.sum(jnp.where(lanes == j, x, 0.0), axis=-1, keepdims=True)


def _rms(x, g):
    return x * lax.rsqrt(jnp.mean(x * x, axis=-1, keepdims=True) + EPS) * g


def _sigmoid(x):
    return 1.0 / (1.0 + jnp.exp(-x))


def _silu(x):
    return x * _sigmoid(x)


def _sigmoid_fast(x):
    return pl.reciprocal(1.0 + jnp.exp(-x), approx=True)


def _softplus(x):
    return jnp.maximum(x, 0.0) + jnp.log(1.0 + jnp.exp(-jnp.abs(x)))


def _gelu(x):
    c = math.sqrt(2.0 / math.pi)
    return 0.5 * x * (1.0 + jnp.tanh(c * (x + 0.044715 * (x * x * x))))


def _first(axes):
    c = None
    for a in axes:
        t = pl.program_id(a) == 0
        c = t if c is None else jnp.logical_and(c, t)
    return c


def _fwd_call(name, fn, grid, ins, outs, acc_axes=()):
    nin = len(ins)

    def body(*refs):
        pids = tuple(pl.program_id(a) for a in range(len(grid)))
        first = _first(acc_axes) if acc_axes else None
        vals = [r[...].astype(F32) for r in refs[:nin]]
        res = fn(pids, *vals)
        for r, o, (_, _, acc) in zip(refs[nin:], res, outs):
            if acc:
                @pl.when(first)
                def _(r=r):
                    r[...] = jnp.zeros_like(r)
                r[...] += o.astype(r.dtype)
            else:
                r[...] = o.astype(r.dtype)

    return _pcall(body, grid=grid, in_specs=[s for _, s in ins], out_specs=[s for _, s, _ in outs],
                  out_shape=[s for s, _, _ in outs], name=name,
                  compiler_params=_params(len(grid)))(*[a for a, _ in ins])


def _bwd_call(name, fn, grid, ins, cts, gouts, acc_axes=(), addends=()):
    nin, nct, nadd = len(ins), len(cts), len(addends)

    def body(*refs):
        pids = tuple(pl.program_id(a) for a in range(len(grid)))
        first = _first(acc_axes) if acc_axes else None
        vals = [r[...].astype(F32) for r in refs[:nin]]
        ctv = tuple(r[...].astype(F32) for r in refs[nin:nin + nct])
        addv = {pos: refs[nin + nct + n][...].astype(F32) for n, (pos, _, _) in enumerate(addends)}
        _, vjp = jax.vjp(lambda *v: tuple(fn(pids, *v)), *vals)
        grads = vjp(ctv)
        for pos, (r, (idx, _, _, acc)) in enumerate(zip(refs[nin + nct + nadd:], gouts)):
            gval = grads[idx]
            if pos in addv:
                gval = gval + addv[pos]
            if acc:
                @pl.when(first)
                def _(r=r):
                    r[...] = jnp.zeros_like(r)
                r[...] += gval.astype(r.dtype)
            else:
                r[...] = gval.astype(r.dtype)

    args = [a for a, _ in ins] + [a for a, _ in cts] + [a for _, a, _ in addends]
    specs = [s for _, s in ins] + [s for _, s in cts] + [s for _, _, s in addends]
    return _pcall(body, grid=grid, in_specs=specs, out_specs=[s for _, _, s, _ in gouts],
                  out_shape=[s for _, s, _, _ in gouts], name=name,
                  compiler_params=_params(len(grid)))(*args)


def _mm(name, a, b, *, ta=False, tb=False, res=None, tm=512, tn=None, tk=None):
    if ta:
        K, M = a.shape
    else:
        M, K = a.shape
    N = b.shape[0] if tb else b.shape[1]
    tm, tn, tk = _tile(M, tm, 8), N if tn is None else tn, K if tk is None else min(tk, K)
    nk = K // tk
    assert res is None or nk == 1
    a_spec = pl.BlockSpec((tk, tm), lambda i, j, k: (k, i)) if ta else pl.BlockSpec((tm, tk), lambda i, j, k: (i, k))
    b_spec = pl.BlockSpec((tn, tk), lambda i, j, k: (j, k)) if tb else pl.BlockSpec((tk, tn), lambda i, j, k: (k, j))
    o_spec = pl.BlockSpec((tm, tn), lambda i, j, k: (i, j))
    ca, cb = (0 if ta else 1), (1 if tb else 0)

    def body(*refs):
        a_ref, b_ref, o_ref = refs[0], refs[1], refs[-1]
        part = _dg(a_ref[...], b_ref[...], ca, cb)
        if nk == 1:
            o_ref[...] = part if res is None else refs[2][...] + part
        else:
            @pl.when(pl.program_id(2) == 0)
            def _():
                o_ref[...] = jnp.zeros_like(o_ref)

            o_ref[...] += part

    args, specs = [a, b], [a_spec, b_spec]
    if res is not None:
        args.append(res)
        specs.append(o_spec)
    return _pcall(body, grid=(M // tm, N // tn, nk), in_specs=specs, out_specs=o_spec, out_shape=_sds((M, N)),
                  name=name, compiler_params=_params(3))(*args)


FFN_PARTS = 2


def _ffn_fwd(name, h, g, wg, wu, wd, l, tm):
    T, D = h.shape
    ns, _, _, Fs = wg.shape

    def body(h_ref, g_ref, wg_ref, wu_ref, wd_ref, ho_ref, gate_ref, up_ref, hn_s, acc_s):
        s = pl.program_id(1)

        @pl.when(s == 0)
        def _():
            hn_s[...] = _rms(h_ref[...], g_ref[...]).astype(BF16)
            acc_s[...] = jnp.zeros_like(acc_s)

        parts = [pl.ds(k * (tm // FFN_PARTS), tm // FFN_PARTS) for k in range(FFN_PARTS)]
        gu = [(_dg(hn_s[r, :], wg_ref[...], 1, 0), _dg(hn_s[r, :], wu_ref[...], 1, 0)) for r in parts]
        for r, (gate, up) in zip(parts, gu):
            gate_ref[r, :] = gate.astype(BF16)
            up_ref[r, :] = up.astype(BF16)
            act = (gate * _sigmoid_fast(gate) * up).astype(BF16)
            acc_s[r, :] += _dg(act, wd_ref[...], 1, 0)

        @pl.when(s == ns - 1)
        def _():
            ho_ref[...] = h_ref[...] + 0.5 * acc_s[...]

    row = pl.BlockSpec((tm, D), lambda i, s: (i, 0))
    wcol = pl.BlockSpec((None, None, D, Fs), lambda i, s: (s, l, 0, 0))
    wrow = pl.BlockSpec((None, None, Fs, D), lambda i, s: (s, l, 0, 0))
    slot = pl.BlockSpec((None, tm, Fs), lambda i, s: (s, i, 0))
    return _pcall(body, grid=(T // tm, ns), in_specs=[row, pl.BlockSpec((1, D), lambda i, s: (0, 0)), wcol, wcol, wrow],
                  out_specs=[row, slot, slot, row],
                  out_shape=[_sds((T, D)), _sds((ns, T, Fs), BF16), _sds((ns, T, Fs), BF16), _sds((T, D), BF16)],
                  scratch_shapes=[pltpu.VMEM((tm, D), F32)], name=name,
                  compiler_params=_params(2))(h, g, wg, wu, wd)


def _ffn_bwd_x(name, dh, h, g, gate, up, wg, wu, wd, l, tm):
    T, D = h.shape
    ns, _, _, Fs = wg.shape

    def body(dh_ref, h_ref, g_ref, gate_ref, up_ref, wg_ref, wu_ref, wd_ref,
             dho_ref, dgate_ref, dup_ref, dgain_ref, do_s, acc_s):
        i, s = pl.program_id(0), pl.program_id(1)

        @pl.when(s == 0)
        def _():
            do_s[...] = (0.5 * dh_ref[...]).astype(BF16)
            acc_s[...] = jnp.zeros_like(acc_s)

        @pl.when(jnp.logical_and(i == 0, s == 0))
        def _():
            dgain_ref[...] = jnp.zeros_like(dgain_ref)

        parts = [pl.ds(k * (tm // FFN_PARTS), tm // FFN_PARTS) for k in range(FFN_PARTS)]
        dact = [_dg(do_s[r, :], wd_ref[...], 1, 1) for r in parts]
        for r, da in zip(parts, dact):
            gt, u = gate_ref[r, :].astype(F32), up_ref[r, :].astype(F32)
            sg = _sigmoid_fast(gt)
            dgt = (da * u * (sg * (1.0 + gt * (1.0 - sg)))).astype(BF16)
            du = (da * (gt * sg)).astype(BF16)
            dgate_ref[r, :] = dgt
            dup_ref[r, :] = du
            acc_s[r, :] += _dg(dgt, wg_ref[...], 1, 1) + _dg(du, wu_ref[...], 1, 1)

        @pl.when(s == ns - 1)
        def _():
            _, vjp = jax.vjp(_rms, h_ref[...], g_ref[...])
            dx, dgain = vjp(acc_s[...])
            dho_ref[...] = dh_ref[...] + dx
            dgain_ref[...] += dgain

    row = pl.BlockSpec((tm, D), lambda i, s: (i, 0))
    gain = pl.BlockSpec((1, D), lambda i, s: (0, 0))
    wcol = pl.BlockSpec((None, None, D, Fs), lambda i, s: (s, l, 0, 0))
    wrow = pl.BlockSpec((None, None, Fs, D), lambda i, s: (s, l, 0, 0))
    slot = pl.BlockSpec((None, tm, Fs), lambda i, s: (s, i, 0))
    return _pcall(body, grid=(T // tm, ns), in_specs=[row, row, gain, slot, slot, wcol, wcol, wrow],
                  out_specs=[row, slot, slot, gain, row],
                  out_shape=[_sds((T, D)), _sds((ns, T, Fs), BF16), _sds((ns, T, Fs), BF16), _sds((1, D)),
                             _sds((T, D), BF16)],
                  scratch_shapes=[pltpu.VMEM((tm, D), F32)], name=name,
                  compiler_params=_params(2))(dh, h, g, gate, up, wg, wu, wd)


def _ffn_bwd_w(name, hn, do, gate, up, dgate, dup, bufs, l, nl, tm):
    T, D = hn.shape
    ns, _, Fs = gate.shape

    def body(hn_ref, do_ref, gate_ref, up_ref, dgate_ref, dup_ref, dwg_ref, dwu_ref, dwd_ref):
        @pl.when(pl.program_id(1) == 0)
        def _():
            dwg_ref[...] = jnp.zeros_like(dwg_ref)
            dwu_ref[...] = jnp.zeros_like(dwu_ref)
            dwd_ref[...] = jnp.zeros_like(dwd_ref)

        def prep(r):
            gt = gate_ref[r, :].astype(F32)
            act = (gt * _sigmoid_fast(gt) * up_ref[r, :].astype(F32)).astype(BF16)
            return hn_ref[r, :], act, do_ref[r, :]

        parts = [pl.ds(k * (tm // FFN_PARTS), tm // FFN_PARTS) for k in range(FFN_PARTS)]
        ready = [prep(r) for r in parts]
        add = lambda xs: functools.reduce(lambda p, q: p + q, xs)
        dwg_ref[...] += add([_dg(hn, dgate_ref[r, :], 0, 0) for r, (hn, _, _) in zip(parts, ready)])
        dwu_ref[...] += add([_dg(hn, dup_ref[r, :], 0, 0) for r, (hn, _, _) in zip(parts, ready)])
        dwd_ref[...] += add([_dg(act, do, 0, 0) for _, act, do in ready])

    row = pl.BlockSpec((tm, D), lambda s, j: (j, 0))
    gain = pl.BlockSpec((1, D), lambda s, j: (0, 0))
    slot = pl.BlockSpec((None, tm, Fs), lambda s, j: (s, j, 0))
    wcol = pl.BlockSpec((None, None, D, Fs), lambda s, j: (l, s, 0, 0))
    wrow = pl.BlockSpec((None, None, Fs, D), lambda s, j: (l, s, 0, 0))
    args, specs, alias = [hn, do, gate, up, dgate, dup], [row, row, slot, slot, slot, slot], {}
    if bufs is not None:
        alias = {len(args) + k: k for k in range(3)}
        args, specs = args + list(bufs), specs + [pl.BlockSpec(memory_space=pl.ANY)] * 3
    return _pcall(lambda *refs: body(*refs[:6], *refs[-3:]), grid=(ns, T // tm), in_specs=specs,
                  out_specs=[wcol, wcol, wrow], input_output_aliases=alias,
                  out_shape=[_sds((nl, ns, D, Fs)), _sds((nl, ns, D, Fs)), _sds((nl, ns, Fs, D))], name=name,
                  compiler_params=_params(2))(*args)


def _rms_fn(pids, x, g):
    return (_rms(x, g),)


def _pool_fn(pids, a, w, scale):
    rows = lax.broadcasted_iota(jnp.int32, (a.shape[0], LANES), 0)
    outs = []
    for gi, win in enumerate(POOL_WINDOWS):
        ag = a[:, gi * LANES:(gi + 1) * LANES]
        s, k = ag, 1
        while k < win:
            s = s + _shift(s, k)
            k *= 2
        cnt = jnp.minimum(rows + 1, win).astype(F32)
        pooled = s / cnt - ag
        outs.append(_bdot(pooled, w[gi]) * scale[:, gi * LANES:(gi + 1) * LANES])
    return (jnp.concatenate(outs, axis=-1),)


def _conv_taps(x, cw):
    K = cw.shape[0]
    y = cw[K - 1] * x
    for j in range(K - 1):
        y = y + cw[j] * _shift(x, K - 1 - j)
    return y


def _prep_fn(pids, q, k, v, bg, cwq, cwk, cwv, alog, dtb):
    hd = pids[0]
    q, k, v = _silu(_conv_taps(q, cwq)), _silu(_conv_taps(k, cwk)), _silu(_conv_taps(v, cwv))
    q = q * lax.rsqrt(jnp.sum(q * q, axis=-1, keepdims=True) + EPS) * (LANES ** -0.5)
    k = k * lax.rsqrt(jnp.sum(k * k, axis=-1, keepdims=True) + EPS)
    beta = _sigmoid(_col(bg, hd))
    g = -jnp.exp(_col(alog, hd)) * _softplus(_col(bg, hd + 4) + _col(dtb, hd))
    return q, k, v, jnp.broadcast_to(beta, q.shape), jnp.broadcast_to(g, q.shape)


def _post_fn(pids, o, z, onorm):
    return (_rms(o, onorm) * _silu(z),)


def _sgu_fn(pids, u, v, g, b, ws, bias_t):
    u, v = _gelu(u), _gelu(v)
    mu = jnp.mean(v, axis=-1, keepdims=True)
    xc = v - mu
    vn = xc * lax.rsqrt(jnp.mean(xc * xc, axis=-1, keepdims=True) + EPS) * g + b
    r = lax.broadcasted_iota(jnp.int32, (LANES, LANES), 0)
    c = lax.broadcasted_iota(jnp.int32, (LANES, LANES), 1)
    rows = []
    for n in range(u.shape[0] // LANES):
        heads = []
        for hd in range(4):
            wm = jnp.where(r >= c, ws[hd], 0.0)
            blk = vn[n * LANES:(n + 1) * LANES, hd * LANES:(hd + 1) * LANES]
            heads.append(_bdot(wm, blk) + _col(bias_t, hd))
        rows.append(jnp.concatenate(heads, axis=-1))
    mixed = jnp.concatenate(rows, axis=0) if len(rows) > 1 else rows[0]
    return (u * mixed,)


def _sconv_fn(pids, xd, bgate, cg, cw):
    return (bgate * _conv_taps(cg * xd, cw),)


def _loss_fn(pids, h, g, tgt):
    err = _rms(h, g) - tgt
    tot = 0.5 * jnp.sum(jnp.mean(err * err, axis=-1, keepdims=True), axis=0, keepdims=True)
    return (jnp.broadcast_to(tot, (1, LANES)),)


DELTA_PAIRS = 4
DELTA_ROWS = DELTA_PAIRS * PAIR


def _each(fn, *lists):
    return [fn(*args) for args in zip(*lists)]


def _chunk_cumsum(g):
    pos = jnp.bitwise_and(lax.broadcasted_iota(jnp.int32, g.shape, 0), CH - 1)
    s, k = g, 1
    while k < CH:
        s = s + jnp.where(pos >= k, _shift(s, k), 0.0)
        k *= 2
    tot = [jnp.broadcast_to(jnp.sum(g[j * CH:(j + 1) * CH], axis=0, keepdims=True), (CH, g.shape[1])) for j in (0, 1)]
    return s, jnp.concatenate(tot, axis=0)


def _stage_a(blocks):
    q, k, v, beta, g = (list(x) for x in zip(*blocks))
    r = lax.broadcasted_iota(jnp.int32, (PAIR, PAIR), 0)
    c = lax.broadcasted_iota(jnp.int32, (PAIR, PAIR), 1)
    same = jnp.right_shift(r, 6) == jnp.right_shift(c, 6)
    tri = jnp.logical_and(same, r >= c)
    strict = jnp.logical_and(same, r > c)
    eye = (r == c).astype(F32)
    sums = _each(_chunk_cumsum, g)
    gc, gt = [a for a, _ in sums], [b for _, b in sums]
    gamma = _each(lambda x: jnp.exp(jnp.where(tri, x - x.T, -jnp.inf)), gc)
    kb = _each(lambda a, b: a * b, k, beta)
    kk = _each(_bdot_nt, kb, k)
    p = _each(lambda a, gm: -jnp.where(strict, a * gm, 0.0), kk, gamma)
    tinv = _each(lambda x: eye + x, p)
    for _ in range(5):
        p = _each(_bdot, p, p)
        tinv = _each(lambda t, x: t + x, tinv, _each(_bdot, tinv, p))
    egc = _each(jnp.exp, gc)
    u = _each(_bdot, tinv, _each(lambda a, b: a * b, v, beta))
    w = _each(_bdot, tinv, _each(lambda a, e: a * e, kb, egc))
    a = _each(lambda x, gm: x * gm, _each(_bdot_nt, q, k), gamma)
    qd = _each(lambda x, e: x * e, q, egc)
    kd = _each(lambda x, t, s: x * jnp.exp(t - s), k, gt, gc)
    return list(zip(u, w, qd, a, kd, _each(jnp.exp, gt)))


def _nn(a, b):
    return _dg(a, b, 1, 0)


def _nt(a, b):
    return _dg(a, b, 1, 1)


def _tn(a, b):
    return _dg(a, b, 0, 0)


def _scan_pair_fwd(s0, u, w, qd, a, kd, l0, l1):
    lo = lambda xs: [x[:CH] for x in xs]
    hi = lambda xs: [x[CH:] for x in xs]
    vn0 = _each(lambda x, y: x - y, lo(u), _each(_nn, lo(w), s0))
    s1 = _each(lambda s, l, x: s * l + x, s0, l0, _each(_tn, lo(kd), vn0))
    vn1 = _each(lambda x, y: x - y, hi(u), _each(_nn, hi(w), s1))
    s2 = _each(lambda s, l, x: s * l + x, s1, l1, _each(_tn, hi(kd), vn1))
    inter = _each(lambda x, y: jnp.concatenate([x, y], axis=0), _each(_nn, lo(qd), s0), _each(_nn, hi(qd), s1))
    intra = _each(_nn, a, _each(lambda x, y: jnp.concatenate([x, y], axis=0), vn0, vn1))
    return _each(lambda x, y: x + y, inter, intra), s1, s2


def _scan_chunk_bwd(s, ds_next, do, dvn_o, u, w, qd, kd, lrow):
    vn = _each(lambda x, y: x - y, u, _each(_nn, w, s))
    dvn = _each(lambda x, y: x + y, dvn_o, _each(_nn, kd, ds_next))
    dkd = _each(_nt, vn, ds_next)
    dl = _each(lambda a, b: jnp.sum(a * b, axis=0, keepdims=True), ds_next, s)
    dw = _each(lambda x: -x, _each(_nt, dvn, s))
    dqd = _each(_nt, do, s)
    ds = _each(lambda x, l, d, y: x + l * d - y, _each(_tn, qd, do), lrow, ds_next, _each(_tn, w, dvn))
    return ds, vn, dvn, dw, dqd, dkd, dl


def _delta_blocks(refs):
    return [tuple(r[p * PAIR:(p + 1) * PAIR, h * LANES:(h + 1) * LANES] for r in refs)
            for p in range(DELTA_PAIRS) for h in range(4)]


def _delta_specs(B, S, reverse):
    assert S % DELTA_ROWS == 0, (S, DELTA_ROWS)
    nstep = S // DELTA_ROWS
    at =(lambda i: nstep - 1 - i) if reverse else (lambda i: i)
    blk = pl.BlockSpec((DELTA_ROWS, 4 * LANES), lambda b, i: (b * nstep + at(i), 0))
    st = pl.BlockSpec((None, 4, 2 * DELTA_PAIRS, LANES, LANES), lambda b, i: (b, 0, at(i), 0, 0))
    scratch = [pltpu.VMEM((4, LANES, LANES), F32), pltpu.VMEM((4 * DELTA_PAIRS, PAIR, LANES), F32)]
    return nstep, blk, st, scratch


def _delta_fwd(name, qn, kn, vn, beta, g, B, S):
    T = qn.shape[0]
    nstep, blk, st, scratch = _delta_specs(B, S, False)

    def body(q_ref, k_ref, v_ref, b_ref, g_ref, o_ref, st_ref, s_s, l_s):
        @pl.when(pl.program_id(1) == 0)
        def _():
            s_s[...] = jnp.zeros_like(s_s)

        outs = _stage_a(_delta_blocks((q_ref, k_ref, v_ref, b_ref, g_ref)))
        for j, blk_out in enumerate(outs):
            l_s[j] = blk_out[5]
        state = [s_s[h] for h in range(4)]
        for p in range(DELTA_PAIRS):
            u, w, qd, a, kd, _ = (list(x) for x in zip(*outs[4 * p:4 * p + 4]))
            l0 = [l_s[4 * p + h, 0:1, :] for h in range(4)]
            l1 = [l_s[4 * p + h, CH:CH + 1, :] for h in range(4)]
            o, mid, end = _scan_pair_fwd(state, u, w, qd, a, kd, l0, l1)
            for h in range(4):
                o_ref[p * PAIR:(p + 1) * PAIR, h * LANES:(h + 1) * LANES] = o[h]
                st_ref[h, 2 * p] = state[h]
                st_ref[h, 2 * p + 1] = mid[h]
            state = end
        for h in range(4):
            s_s[h] = state[h]

    return _pcall(body, grid=(B, nstep), in_specs=[blk] * 5, out_specs=[blk, st],
                  out_shape=[_sds((T, 4 * LANES)), _sds((B, 4, S // CH, LANES, LANES))],
                  scratch_shapes=scratch, name=name, compiler_params=_params(2))(qn, kn, vn, beta, g)


def _delta_bwd(name, qn, kn, vn, beta, g, states, do, B, S):
    T = qn.shape[0]
    nstep, blk, st, scratch = _delta_specs(B, S, True)

    def body(q_ref, k_ref, v_ref, b_ref, g_ref, st_ref, do_ref, dq_ref, dk_ref, dv_ref, db_ref, dg_ref, ds_s, l_s):
        @pl.when(pl.program_id(1) == 0)
        def _():
            ds_s[...] = jnp.zeros_like(ds_s)

        rowid = lax.broadcasted_iota(jnp.int32, (PAIR, LANES), 0)
        lo = lambda xs: [x[:CH] for x in xs]
        hi = lambda xs: [x[CH:] for x in xs]
        cat = lambda xs, ys: _each(lambda x, y: jnp.concatenate([x, y], axis=0), xs, ys)
        outs, vjp = jax.vjp(_stage_a, _delta_blocks((q_ref, k_ref, v_ref, b_ref, g_ref)))
        for j, blk_out in enumerate(outs):
            l_s[j] = blk_out[5]
        ds = [ds_s[h] for h in range(4)]
        cts = [None] * (4 * DELTA_PAIRS)
        for p in reversed(range(DELTA_PAIRS)):
            u, w, qd, a, kd, _ = (list(x) for x in zip(*outs[4 * p:4 * p + 4]))
            l0 = [l_s[4 * p + h, 0:1, :] for h in range(4)]
            l1 = [l_s[4 * p + h, CH:CH + 1, :] for h in range(4)]
            s0 = [st_ref[h, 2 * p] for h in range(4)]
            s1 = [st_ref[h, 2 * p + 1] for h in range(4)]
            dout = [do_ref[p * PAIR:(p + 1) * PAIR, h * LANES:(h + 1) * LANES] for h in range(4)]
            dvn_o = _each(_tn, a, dout)
            ds1, vn1, dvn1, dw1, dqd1, dkd1, dl1 = _scan_chunk_bwd(s1, ds, hi(dout), hi(dvn_o), hi(u), hi(w), hi(qd),
                                                                   hi(kd), l1)
            ds, vn0, dvn0, dw0, dqd0, dkd0, dl0 = _scan_chunk_bwd(s0, ds1, lo(dout), lo(dvn_o), lo(u), lo(w), lo(qd),
                                                                  lo(kd), l0)
            da = _each(_nt, dout, cat(vn0, vn1))
            dl = _each(lambda x, y: jnp.where(rowid == 0, x, jnp.where(rowid == CH, y, 0.0)), dl0, dl1)
            for h, ct in enumerate(zip(cat(dvn0, dvn1), cat(dw0, dw1), cat(dqd0, dqd1), da, cat(dkd0, dkd1), dl)):
                cts[4 * p + h] = ct
        for h in range(4):
            ds_s[h] = ds[h]
        (grads,) = vjp(cts)
        for j, blk_grads in enumerate(grads):
            p, h = divmod(j, 4)
            for ref, val in zip((dq_ref, dk_ref, dv_ref, db_ref, dg_ref), blk_grads):
                ref[p * PAIR:(p + 1) * PAIR, h * LANES:(h + 1) * LANES] = val

    return _pcall(body, grid=(B, nstep), in_specs=[blk] * 5 + [st, blk], out_specs=[blk] * 5,
                  out_shape=[_sds((T, 4 * LANES))] * 5, scratch_shapes=scratch, name=name,
                  compiler_params=_params(2))(qn, kn, vn, beta, g, states, do)


ROW_TILE = 512
FFN_TILE = 1024
FFN_TILE_BX = 1024
SGU_ROWS = 256


def _rows(r, c, off=0):
    return pl.BlockSpec((r, c), lambda i: (i, off))


def _whole(shape, nd):
    zeros = (0,) * len(shape)
    if nd == 1:
        return pl.BlockSpec(shape, lambda i: zeros)
    return pl.BlockSpec(shape, lambda i, j: zeros)


def _norm_fwd(name, h, g):
    T, D = h.shape
    tr = _tile(T, ROW_TILE, 8)
    return _fwd_call(name, _rms_fn, (T // tr,), [(h, _rows(tr, D)), (g, _whole((1, D), 1))],
                     [(_sds((T, D), BF16), _rows(tr, D), False)])[0]


def _norm_bwd(name, h, g, dhn, dh_res):
    T, D = h.shape
    tr = _tile(T, ROW_TILE, 8)
    return _bwd_call(name, _rms_fn, (T // tr,), [(h, _rows(tr, D)), (g, _whole((1, D), 1))], [(dhn, _rows(tr, D))],
                     [(0, _sds((T, D)), _rows(tr, D), False), (1, _sds((1, D)), _whole((1, D), 1), True)],
                     acc_axes=(0,), addends=[(0, dh_res, _rows(tr, D))])


def _ab_specs(B, S, P):
    W4 = 4 * LANES
    seq4 = pl.BlockSpec((S, W4), lambda b: (b, 0))
    pool_ins = lambda proj: [(proj, seq4), (P["pool_w"], _whole((4, LANES, LANES), 1)), (P["pool_scale"], _whole((1, W4), 1))]
    hb = lambda off: pl.BlockSpec((S, LANES), lambda b, h: (b, off + h))
    cw = lambda off: pl.BlockSpec((4, 1, LANES), lambda b, h: (0, 0, off + h))
    small = _whole((1, LANES), 2)
    prep_ins = lambda proj: [(proj, hb(4)), (proj, hb(8)), (proj, hb(12)),
                             (proj, pl.BlockSpec((S, LANES), lambda b, h: (b, 20))),
                             (P["conv_w"], cw(0)), (P["conv_w"], cw(4)), (P["conv_w"], cw(8)),
                             (P["a_log"], small), (P["dt_bias"], small)]
    post_ins = lambda o, proj: [(o, hb(0)), (proj, hb(16)), (P["out_norm"], small)]
    return seq4, pool_ins, hb, prep_ins, post_ins, small


def _ab_fwd(tag, h, P, B, S):
    T, D = h.shape
    W4 = 4 * LANES
    seq4, pool_ins, hb, prep_ins, post_ins, small = _ab_specs(B, S, P)
    hn = _norm_fwd(tag + "_norm", h, P["mix_norm"])
    proj = _mm(tag + "_in", hn, P["w_in"], tm=256)
    ya = _fwd_call(tag + "_pool", _pool_fn, (B,), pool_ins(proj), [(_sds((T, W4), BF16), seq4, False)])[0]
    qn, kn, vn, beta, g = _fwd_call(tag + "_prep", lambda pids, *v: _prep_fn((pids[1],), *v), (B, 4), prep_ins(proj),
                                    [(_sds((T, W4)), hb(0), False)] * 5)
    o, states = _delta_fwd(tag + "_delta", qn, kn, vn, beta, g, B, S)
    yb = _fwd_call(tag + "_post", _post_fn, (B, 4), post_ins(o, proj), [(_sds((T, W4), BF16), hb(0), False)])[0]
    y = jnp.concatenate([ya, yb], axis=-1)
    h_new = _mm(tag + "_out", y, P["w_out"], res=h)
    return h_new, (h, hn, proj, qn, kn, vn, beta, g, states, o, y)


def _ab_bwd(tag, dh, saved, P, B, S):
    h, hn, proj, qn, kn, vn, beta, g, states, o, y = saved
    T, D = h.shape
    W4 = 4 * LANES
    seq4, pool_ins, hb, prep_ins, post_ins, small = _ab_specs(B, S, P)
    dy = _mm(tag + "_out_dx", dh, P["w_out"], tb=True)
    dw_out = _mm(tag + "_out_dw", y, dh, ta=True, tm=D, tk=1024)
    do, dz, d_onorm = _bwd_call(tag + "_post_b", _post_fn, (B, 4), post_ins(o, proj), [(dy, hb(4))],
                                [(0, _sds((T, W4)), hb(0), False), (1, _sds((T, W4)), hb(0), False),
                                 (2, _sds((1, LANES)), small, True)], acc_axes=(0, 1))
    dqn, dkn, dvn, dbeta, dg = _delta_bwd(tag + "_delta_b", qn, kn, vn, beta, g, states, do, B, S)
    per_b = pl.BlockSpec((None, 1, LANES), lambda b, h: (b, 0, 0))
    dcw = pl.BlockSpec((None, 4, 1, LANES), lambda b, h: (b, 0, 0, h))
    dq, dk, dv, dbg, dcq, dck, dcv, dalog, ddt = _bwd_call(
        tag + "_prep_b", lambda pids, *v: _prep_fn((pids[1],), *v), (B, 4), prep_ins(proj),
        [(dqn, hb(0)), (dkn, hb(0)), (dvn, hb(0)), (dbeta, hb(0)), (dg, hb(0))],
        [(0, _sds((T, W4)), hb(0), False), (1, _sds((T, W4)), hb(0), False), (2, _sds((T, W4)), hb(0), False),
         (3, _sds((T, LANES)), pl.BlockSpec((S, LANES), lambda b, h: (b, 0)), True),
         (4, _sds((B, 4, 1, W4)), dcw, False), (5, _sds((B, 4, 1, W4)), dcw, False),
         (6, _sds((B, 4, 1, W4)), dcw, False),
         (7, _sds((B, 1, LANES)), per_b, True), (8, _sds((B, 1, LANES)), per_b, True)], acc_axes=(1,))
    d_conv = jnp.concatenate([jnp.sum(d, axis=0)[:, 0, :] for d in (dcq, dck, dcv)], axis=-1)
    da, dpool_w, dpool_scale = _bwd_call(
        tag + "_pool_b", _pool_fn, (B,), pool_ins(proj), [(dy, seq4)],
        [(0, _sds((T, W4)), seq4, False), (1, _sds((4, LANES, LANES)), _whole((4, LANES, LANES), 1), True),
         (2, _sds((1, W4)), _whole((1, W4), 1), True)], acc_axes=(0,))
    dproj = jnp.concatenate([da, dq, dk, dv, dz, dbg], axis=-1)
    dhn = _mm(tag + "_in_dx", dproj, P["w_in"], tb=True, tm=256)
    dw_in = _mm(tag + "_in_dw", hn, dproj, ta=True, tm=D, tk=512)
    dh_new, d_mix = _norm_bwd(tag + "_norm_b", h, P["mix_norm"], dhn, dh)
    grads = dict(w_in=dw_in, w_out=dw_out, mix_norm=d_mix[0], pool_w=dpool_w, pool_scale=dpool_scale[0],
                 conv_w=d_conv, a_log=jnp.sum(dalog, axis=0)[0, :4], dt_bias=jnp.sum(ddt, axis=0)[0, :4],
                 out_norm=d_onorm[0])
    return dh_new, grads


def _cd_specs(B, S, T, P):
    W4 = 4 * LANES
    R = _tile(T, SGU_ROWS, LANES)
    sgu_ins = lambda proj: [(proj, _rows(R, W4, 0)), (proj, _rows(R, W4, 1)), (P["sgu_g"], _whole((1, W4), 1)),
                            (P["sgu_b"], _whole((1, W4), 1)), (P["sgu_w"], _whole((4, LANES, LANES), 1)),
                            (P["bias_t"], _whole((LANES, LANES), 1))]
    jb = lambda off: pl.BlockSpec((S, LANES), lambda j, b: (b, off + j))
    sc_ins = lambda proj: [(proj, jb(8)), (proj, jb(12)), (proj, jb(16)),
                           (P["sc_w"], pl.BlockSpec((3, 1, LANES), lambda j, b: (0, 0, j)))]
    return R, sgu_ins, jb, sc_ins


def _cd_fwd(tag, h, P, B, S):
    T, D = h.shape
    W4 = 4 * LANES
    R, sgu_ins, jb, sc_ins = _cd_specs(B, S, T, P)
    hn = _norm_fwd(tag + "_norm", h, P["mix_norm"])
    proj = _mm(tag + "_in", hn, P["w_in"], tm=256)
    yc = _fwd_call(tag + "_sgu", _sgu_fn, (T // R,), sgu_ins(proj), [(_sds((T, W4), BF16), _rows(R, W4), False)])[0]
    yd = _fwd_call(tag + "_sconv", _sconv_fn, (4, B), sc_ins(proj), [(_sds((T, W4), BF16), jb(0), False)])[0]
    y = jnp.concatenate([yc, yd], axis=-1)
    h_new = _mm(tag + "_out", y, P["w_out"], res=h)
    return h_new, (h, hn, proj, y)


def _cd_bwd(tag, dh, saved, P, B, S):
    h, hn, proj, y = saved
    T, D = h.shape
    W4 = 4 * LANES
    R, sgu_ins, jb, sc_ins = _cd_specs(B, S, T, P)
    dy = _mm(tag + "_out_dx", dh, P["w_out"], tb=True)
    dw_out = _mm(tag + "_out_dw", y, dh, ta=True, tm=D, tk=1024)
    du, dv, dsg, dsb, dsw, dbias_t = _bwd_call(
        tag + "_sgu_b", _sgu_fn, (T // R,), sgu_ins(proj), [(dy, _rows(R, W4, 0))],
        [(0, _sds((T, W4)), _rows(R, W4), False), (1, _sds((T, W4)), _rows(R, W4), False),
         (2, _sds((1, W4)), _whole((1, W4), 1), True), (3, _sds((1, W4)), _whole((1, W4), 1), True),
         (4, _sds((4, LANES, LANES)), _whole((4, LANES, LANES), 1), True),
         (5, _sds((LANES, LANES)), _whole((LANES, LANES), 1), True)], acc_axes=(0,))
    dxd, dbgate, dcg, dsc = _bwd_call(
        tag + "_sconv_b", _sconv_fn, (4, B), sc_ins(proj), [(dy, jb(4))],
        [(0, _sds((T, W4)), jb(0), False), (1, _sds((T, W4)), jb(0), False), (2, _sds((T, W4)), jb(0), False),
         (3, _sds((3, 1, W4)), pl.BlockSpec((3, 1, LANES), lambda j, b: (0, 0, j)), True)], acc_axes=(1,))
    dproj = jnp.concatenate([du, dv, dxd, dbgate, dcg], axis=-1)
    dhn = _mm(tag + "_in_dx", dproj, P["w_in"], tb=True, tm=256)
    dw_in = _mm(tag + "_in_dw", hn, dproj, ta=True, tm=D, tk=512)
    dh_new, d_mix = _norm_bwd(tag + "_norm_b", h, P["mix_norm"], dhn, dh)
    grads = dict(w_in=dw_in, w_out=dw_out, mix_norm=d_mix[0], sgu_g=dsg[0], sgu_b=dsb[0], sgu_w=dsw,
                 sgu_bias=dbias_t[:, :4].T, sc_w=dsc[:, 0, :])
    return dh_new, grads


def _loss_fwd_bwd(h, g, tgt):
    T, D = h.shape
    tr = _tile(T, ROW_TILE, 8)
    ins = [(h, _rows(tr, D)), (g, _whole((1, D), 1)), (tgt, _rows(tr, D))]
    vec = _whole((1, LANES), 1)
    loss = _fwd_call("loss", _loss_fn, (T // tr,), ins, [(_sds((1, LANES)), vec, True)], acc_axes=(0,))[0]
    one = jnp.zeros((1, LANES), F32).at[0, 0].set(1.0)
    dh, dg = _bwd_call("loss_b", _loss_fn, (T // tr,), ins, [(one, vec)],
                       [(0, _sds((T, D)), _rows(tr, D), False), (1, _sds((1, D)), _whole((1, D), 1), True)],
                       acc_axes=(0,))
    return loss[0, 0], dh, dg[0]


def _local_step(x2, tgt2, W, B, S):
    L = W["ffn1_norm"].shape[0]
    tm = _tile(x2.shape[0], FFN_TILE, 8)
    tmx = _tile(x2.shape[0], FFN_TILE_BX, 8)
    h = x2
    saved = []
    for l in range(L):
        e = l // 2
        f1 = (W["ffn1_norm"][l][None], W["ffn1_g"], W["ffn1_u"], W["ffn1_d"])
        f2 = (W["ffn2_norm"][l][None], W["ffn2_g"], W["ffn2_u"], W["ffn2_d"])
        h0 = h
        h, gate1, up1, hn1 = _ffn_fwd(f"l{l}_ffn1", h0, *f1, l, tm)
        if l % 2 == 0:
            P = dict(mix_norm=W["mix_norm"][l][None], w_in=W["ab_in"][e], w_out=W["ab_out"][e], pool_w=W["pool_w"][e],
                     pool_scale=W["pool_scale"][e][None], conv_w=W["dn_conv_w"][e][:, None, :],
                     a_log=jnp.pad(W["dn_a_log"][e][None], ((0, 0), (0, LANES - 4))),
                     dt_bias=jnp.pad(W["dn_dt_bias"][e][None], ((0, 0), (0, LANES - 4))),
                     out_norm=W["dn_out_norm"][e][None])
            h1 = h
            h, msave = _ab_fwd(f"l{l}_ab", h1, P, B, S)
        else:
            P = dict(mix_norm=W["mix_norm"][l][None], w_in=W["cd_in"][e], w_out=W["cd_out"][e],
                     sgu_g=W["sgu_norm_g"][e][None], sgu_b=W["sgu_norm_b"][e][None], sgu_w=W["sgu_w"][e],
                     bias_t=jnp.pad(W["sgu_bias"][e].T, ((0, 0), (0, LANES - 4))),
                     sc_w=W["sc_conv_w"][e][:, None, :])
            h1 = h
            h, msave = _cd_fwd(f"l{l}_cd", h1, P, B, S)
        h2 = h
        h, gate2, up2, hn2 = _ffn_fwd(f"l{l}_ffn2", h2, *f2, l, tm)
        saved.append((f1, f2, P, h0, gate1, up1, hn1, msave, h2, gate2, up2, hn2))

    loss, dh, d_final = _loss_fwd_bwd(h, W["final_norm"][None], tgt2)

    G = {k: [None] * L for k in ("ffn1_norm", "ffn2_norm", "mix_norm")}
    bufs1 = bufs2 = None
    GA, GC = [None] * ((L + 1) // 2), [None] * (L // 2)
    for l in reversed(range(L)):
        f1, f2, P, h0, gate1, up1, hn1, msave, h2, gate2, up2, hn2 = saved[l]
        dh, dgate, dup, dn2, do = _ffn_bwd_x(f"l{l}_ffn2_bx", dh, h2, f2[0], gate2, up2, f2[1], f2[2], f2[3], l, tmx)
        bufs2 = _ffn_bwd_w(f"l{l}_ffn2_bw", hn2, do, gate2, up2, dgate, dup, bufs2, l, L, tm)
        G["ffn2_norm"][l] = dn2[0]
        if l % 2 == 0:
            dh, mg = _ab_bwd(f"l{l}_ab", dh, msave, P, B, S)
            GA[l // 2] = mg
        else:
            dh, mg = _cd_bwd(f"l{l}_cd", dh, msave, P, B, S)
            GC[l // 2] = mg
        G["mix_norm"][l] = mg["mix_norm"]
        dh, dgate, dup, dn1, do = _ffn_bwd_x(f"l{l}_ffn1_bx", dh, h0, f1[0], gate1, up1, f1[1], f1[2], f1[3], l, tmx)
        bufs1 = _ffn_bwd_w(f"l{l}_ffn1_bw", hn1, do, gate1, up1, dgate, dup, bufs1, l, L, tm)
        G["ffn1_norm"][l] = dn1[0]

    st = lambda xs: jnp.stack(xs, axis=0)
    big = dict(ffn1_w_gate=bufs1[0], ffn1_w_up=bufs1[1], ffn1_w_down=bufs1[2],
               ffn2_w_gate=bufs2[0], ffn2_w_up=bufs2[1], ffn2_w_down=bufs2[2],
               ab_w_in=st([m["w_in"] for m in GA]), ab_w_out=st([m["w_out"] for m in GA]),
               cd_w_in=st([m["w_in"] for m in GC]), cd_w_out=st([m["w_out"] for m in GC]))
    small = dict(ffn1_norm=st(G["ffn1_norm"]), mix_norm=st(G["mix_norm"]), ffn2_norm=st(G["ffn2_norm"]),
                 pool_w=st([m["pool_w"] for m in GA]), pool_scale=st([m["pool_scale"] for m in GA]),
                 dn_conv_w=st([m["conv_w"] for m in GA]), dn_a_log=st([m["a_log"] for m in GA]),
                 dn_dt_bias=st([m["dt_bias"] for m in GA]), dn_out_norm=st([m["out_norm"] for m in GA]),
                 sgu_norm_g=st([m["sgu_g"] for m in GC]), sgu_norm_b=st([m["sgu_b"] for m in GC]),
                 sgu_w=st([m["sgu_w"] for m in GC]), sgu_bias=st([m["sgu_bias"] for m in GC]),
                 sc_conv_w=st([m["sc_w"] for m in GC]), final_norm=d_final)
    return loss, dh, big, small


ANY = pl.BlockSpec(memory_space=pl.ANY)


def _place():
    return lax.axis_index("x"), lax.axis_index("y"), lax.axis_index("c")


def _exchange(name, srcs, out_shapes, plan, n_copies, bases=None):
    n, m = len(srcs), len(out_shapes)
    nb = m if bases is not None else 0

    def body(*refs):
        ins, outs = refs[:n], refs[n + nb:n + nb + m]
        send, recv = refs[n + nb + m:]
        remote = plan(_place(), ins, outs)
        assert len(remote) == n_copies
        sends = []
        for k, (s, d, peer, _) in enumerate(remote):
            cp = pltpu.make_async_remote_copy(src_ref=s, dst_ref=d, send_sem=send.at[k], recv_sem=recv.at[k],
                                              device_id=peer, device_id_type=MESH)
            cp.start()
            sends.append(cp)
        for k, (s, _, peer, land) in enumerate(remote):
            pltpu.make_async_remote_copy(src_ref=s, dst_ref=land, send_sem=send.at[k], recv_sem=recv.at[k],
                                         device_id=peer, device_id_type=MESH).wait_recv()
        for cp in sends:
            cp.wait_send()

    return _pcall(body, in_specs=[ANY] * (n + nb), out_specs=[ANY] * m, out_shape=out_shapes,
                  input_output_aliases={n + k: k for k in range(nb)},
                  scratch_shapes=[pltpu.SemaphoreType.DMA((n_copies,)), pltpu.SemaphoreType.DMA((n_copies,))],
                  name=name)(*srcs, *(bases or ()))


def _other_chips(x, y):
    return [(1 - x, y), (x, 1 - y), (1 - x, 1 - y)]


def _gather_chips(name, xs):
    n = len(xs)
    split = [a.ndim >= 3 and a.shape[0] % 2 == 0 for a in xs]
    n_fwd = sum(split)
    fwd_of = {t: j for j, t in enumerate(t for t in range(n) if split[t])}

    def body(*refs):
        ins, outs = refs[:n], refs[n:2 * n]
        send0, recv0, send1, recv1, send2, recv2 = refs[2 * n:]
        x, y, c = _place()
        me = 2 * x + y
        chips = _other_chips(x, y)

        def mine(t):
            return pltpu.make_async_remote_copy(src_ref=ins[t], dst_ref=outs[t].at[me], send_sem=send0.at[t],
                                                recv_sem=recv0.at[t], device_id=(x, y, 1 - c), device_id_type=MESH)

        def part(t, cc):
            half = xs[t].shape[0] // 2
            return pl.ds(cc * half, half) if split[t] else pl.ds(0, xs[t].shape[0])

        def first(r, t, started):
            px, py = chips[r]
            dst = outs[t].at[me, part(t, c)] if started else outs[t].at[2 * px + py, part(t, c)]
            return pltpu.make_async_remote_copy(src_ref=ins[t].at[part(t, c)], dst_ref=dst, send_sem=send1.at[r, t],
                                                recv_sem=recv1.at[r, t], device_id=(px, py, c), device_id_type=MESH)

        def second(r, t, started):
            px, py = chips[r]
            rows = part(t, c) if started else part(t, 1 - c)
            blk = outs[t].at[2 * px + py, rows]
            return pltpu.make_async_remote_copy(src_ref=blk, dst_ref=blk, send_sem=send2.at[r, fwd_of[t]],
                                                recv_sem=recv2.at[r, fwd_of[t]], device_id=(x, y, 1 - c),
                                                device_id_type=MESH)

        sends = [first(r, t, True) for r in range(3) for t in range(n)] + [mine(t) for t in range(n)]
        for cp in sends:
            cp.start()
        for r in range(3):
            for t in range(n):
                first(r, t, False).wait_recv()
                if split[t]:
                    cp = second(r, t, True)
                    cp.start()
                    sends.append(cp)
        for r in range(3):
            for t in range(n):
                if split[t]:
                    second(r, t, False).wait_recv()
        for t in range(n):
            mine(t).wait_recv()
        for cp in sends:
            cp.wait_send()

    return _pcall(body, in_specs=[ANY] * n, out_specs=[ANY] * n,
                  out_shape=[_sds((N_CHIPS,) + a.shape, a.dtype) for a in xs],
                  scratch_shapes=[pltpu.SemaphoreType.DMA((n,)), pltpu.SemaphoreType.DMA((n,)),
                                  pltpu.SemaphoreType.DMA((3, n)), pltpu.SemaphoreType.DMA((3, n)),
                                  pltpu.SemaphoreType.DMA((3, max(n_fwd, 1))), pltpu.SemaphoreType.DMA((3, max(n_fwd, 1)))],
                  name=name)(*xs)


def _pair_split(name, gs):
    n = len(gs)

    def plan(place, ins, outs):
        x, y, c = place
        return [(ins[t].at[:, :, 1 - c], outs[t], (x, y, 1 - c), outs[t]) for t in range(n)]

    return _exchange(name, gs, [_sds(a.shape[:2] + a.shape[3:], a.dtype) for a in gs], plan, n)


def _chip_scatter(name, ps, bases):
    n = len(ps)

    def plan(place, ins, outs):
        x, y, c = place
        me = 2 * x + y
        return [(ins[t].at[:, 2 * px + py], outs[t].at[:, me], (px, py, c), outs[t].at[:, 2 * px + py])
                for (px, py) in _other_chips(x, y) for t in range(n)]

    return _exchange(name, ps, [_sds(a.shape, a.dtype) for a in ps], plan, 3 * n, bases)


def _pair_share(name, ts):
    n = len(ts)

    def plan(place, ins, outs):
        x, y, c = place
        return [(ins[t], outs[t], (x, y, 1 - c), outs[t]) for t in range(n)]

    return _exchange(name, ts, [_sds(a.shape, a.dtype) for a in ts], plan, n)


def _gather_devices(name, v, base):
    flips = [(fx, fy, fc) for fx in (0, 1) for fy in (0, 1) for fc in (0, 1)][1:]

    def plan(place, ins, outs):
        x, y, c = place
        me = 4 * x + 2 * y + c
        remote = []
        for fx, fy, fc in flips:
            px, py, pc = (1 - x if fx else x), (1 - y if fy else y), (1 - c if fc else c)
            remote.append((ins[0], outs[0].at[me], (px, py, pc), outs[0].at[4 * px + 2 * py + pc]))
        return remote

    return _exchange(name, [v], [_sds((8,) + v.shape, v.dtype)], plan, len(flips), [base])[0]


def _sum_slots(name, a):
    k, rows, cols = a.shape
    tr = _tile(rows, ROW_TILE, 8)
    ins = [(a, pl.BlockSpec((None, tr, cols), lambda i, j=j: (j, i, 0))) for j in range(k)]
    return _fwd_call(name, lambda pids, *v: (functools.reduce(lambda p, q: p + q, v),), (rows // tr,), ins,
                     [(_sds((rows, cols)), _rows(tr, cols), False)])[0]


def _half_tile(ah):
    return _tile(ah, 512, 8)


def _pair_sum(name, g5, theirs, cf, out_dtype):
    L, P4, _, Ah, Bt = g5.shape
    ta = _half_tile(Ah)
    half = lambda hh: pl.BlockSpec((None, None, None, ta, Bt), lambda l, p, i: (l, p, hh, i, 0))
    blk = pl.BlockSpec((None, None, ta, Bt), lambda l, p, i: (l, p, i, 0))

    def fn(pids, g0, g1, r, c):
        tot = jnp.where(jnp.max(c) > 0.5, g1, g0) + r
        return tot, tot

    return _fwd_call(name, fn, (L, P4, Ah // ta), [(g5, half(0)), (g5, half(1)), (theirs, blk),
                                                   (cf, pl.BlockSpec((1, LANES), lambda l, p, i: (0, 0)))],
                     [(_sds((L, P4, Ah, Bt), out_dtype), blk, False)] * 2)


def _sum_chips(name, parts):
    L, P4, Ah, Bt = parts.shape
    ta = _half_tile(Ah)
    ins = [(parts, pl.BlockSpec((None, None, ta, Bt), lambda l, i, q=q: (l, q, i, 0))) for q in range(P4)]
    return _fwd_call(name, lambda pids, *v: (functools.reduce(lambda p, q: p + q, v),), (L, Ah // ta), ins,
                     [(_sds((L, Ah, Bt)), pl.BlockSpec((None, ta, Bt), lambda l, i: (l, i, 0)), False)])[0]


def _adam_terms(w, g, m, v):
    m2 = ADAM_B1 * m + (1.0 - ADAM_B1) * g
    v2 = ADAM_B2 * v + (1.0 - ADAM_B2) * (g * g)
    m_hat = m2 / (1.0 - ADAM_B1 ** ADAM_STEP)
    v_hat = v2 / (1.0 - ADAM_B2 ** ADAM_STEP)
    return -ADAM_LR * (m_hat / (jnp.sqrt(v_hat) + ADAM_EPS) + ADAM_WD * w), m2, v2


def _adam_halves(name, w, m, v, mine, theirs, cf):
    L, Aa, Bt = w.shape
    Ah = Aa // 2
    ta = _half_tile(Ah)
    full = pl.BlockSpec((None, None, ta, Bt), lambda l, hh, i: (l, hh, i, 0))
    part = pl.BlockSpec((None, ta, Bt), lambda l, hh, i: (l, i, 0))

    def fn(pids, w_, m_, v_, a, b, c):
        g = jnp.where(jnp.max(c) == pids[1].astype(F32), a, b)
        return (g,) + _adam_terms(w_, g, m_, v_)

    ins = [(a.reshape(L, 2, Ah, Bt), full) for a in (w, m, v)] + [(mine, part), (theirs, part),
                                                                 (cf, pl.BlockSpec((1, LANES), lambda l, hh, i: (0, 0)))]
    outs = _fwd_call(name, fn, (L, 2, Ah // ta), ins, [(_sds((L, 2, Ah, Bt)), full, False)] * 4)
    return [o.reshape(w.shape) for o in outs]


def _adam_rows(name, w, g, m, v):
    rows, cols = w.shape
    tr = _tile(rows, ROW_TILE, 8)
    ins = [(a, _rows(tr, cols)) for a in (w, g, m, v)]
    return _fwd_call(name, lambda pids, *a: _adam_terms(*a), (rows // tr,), ins,
                     [(_sds((rows, cols)), _rows(tr, cols), False)] * 3)


def _pack(xs):
    rows = []
    for a in xs:
        flat = a.reshape(-1).astype(F32)
        rows.append(jnp.pad(flat, (0, (-flat.size) % LANES)).reshape(-1, LANES))
    buf = jnp.concatenate(rows, axis=0)
    return jnp.pad(buf, ((0, (-buf.shape[0]) % 8), (0, 0)))


def _unpack(buf, shapes):
    out, row = [], 0
    for s in shapes:
        n = math.prod(s)
        nr = -(-n // LANES)
        out.append(buf[row:row + nr].reshape(-1)[:n].reshape(s))
        row += nr
    return out


_WEIGHTS = ("ffn1_norm", "ffn1_w_gate", "ffn1_w_up", "ffn1_w_down", "mix_norm", "ffn2_norm", "ffn2_w_gate", "ffn2_w_up",
            "ffn2_w_down", "ab_w_in", "pool_w", "pool_scale", "dn_conv_w", "dn_a_log", "dn_dt_bias", "dn_out_norm",
            "ab_w_out", "cd_w_in", "sgu_norm_g", "sgu_norm_b", "sgu_w", "sgu_bias", "sc_conv_w", "cd_w_out", "final_norm")
_BIG = ("ffn1_w_gate", "ffn1_w_up", "ffn1_w_down", "ffn2_w_gate", "ffn2_w_up", "ffn2_w_down", "ab_w_in", "ab_w_out",
        "cd_w_in", "cd_w_out")
_ROW_SHARDED = ("ffn1_w_down", "ffn2_w_down", "ab_w_out", "cd_w_out")
_SMALL_SHARDED = ("dn_conv_w", "sgu_norm_g", "sgu_norm_b", "sc_conv_w")
_SMALL = tuple(n for n in _WEIGHTS if n not in _BIG)


def _to_slots(name, g):
    L, A, Bt = g.shape
    if name in _ROW_SHARDED:
        return g.reshape(L, N_CHIPS, A // N_CHIPS, Bt)
    return g.reshape(L, A, N_CHIPS, Bt // N_CHIPS).transpose(0, 2, 1, 3)


def kernel(x, ffn1_norm, ffn1_w_gate, ffn1_w_up, ffn1_w_down, mix_norm, ffn2_norm, ffn2_w_gate, ffn2_w_up, ffn2_w_down,
           ab_w_in, pool_w, pool_scale, dn_conv_w, dn_a_log, dn_dt_bias, dn_out_norm, ab_w_out, cd_w_in, sgu_norm_g,
           sgu_norm_b, sgu_w, sgu_bias, sc_conv_w, cd_w_out, final_norm, loss_target,
           m_ffn1_norm, m_ffn1_w_gate, m_ffn1_w_up, m_ffn1_w_down, m_mix_norm, m_ffn2_norm, m_ffn2_w_gate, m_ffn2_w_up,
           m_ffn2_w_down, m_ab_w_in, m_pool_w, m_pool_scale, m_dn_conv_w, m_dn_a_log, m_dn_dt_bias, m_dn_out_norm,
           m_ab_w_out, m_cd_w_in, m_sgu_norm_g, m_sgu_norm_b, m_sgu_w, m_sgu_bias, m_sc_conv_w, m_cd_w_out, m_final_norm,
           v_ffn1_norm, v_ffn1_w_gate, v_ffn1_w_up, v_ffn1_w_down, v_mix_norm, v_ffn2_norm, v_ffn2_w_gate, v_ffn2_w_up,
           v_ffn2_w_down, v_ab_w_in, v_pool_w, v_pool_scale, v_dn_conv_w, v_dn_a_log, v_dn_dt_bias, v_dn_out_norm,
           v_ab_w_out, v_cd_w_in, v_sgu_norm_g, v_sgu_norm_b, v_sgu_w, v_sgu_bias, v_sc_conv_w, v_cd_w_out, v_final_norm):
    A = dict(locals())
    B, S, D = x.shape
    T = B * S
    xi, yi, ci = _place()
    chip = 2 * xi + yi
    cf = jnp.broadcast_to(ci.astype(F32), (1, LANES))
    own = lambda n, mine, slot: lax.dynamic_update_index_in_dim(lax.empty((n,) + mine.shape, mine.dtype), mine, slot, 0)

    sh_shapes = [A[n].shape for n in _SMALL_SHARDED]
    local = [A[n].astype(BF16) for n in _BIG] + [_pack([A[n] for n in _SMALL_SHARDED])]
    gathered = _gather_chips("gather_weights", local)
    gw = dict(zip(_BIG, gathered[:-1]))
    per_chip = [_unpack(gathered[-1][p], sh_shapes) for p in range(N_CHIPS)]
    W = {n: A[n] for n in _SMALL if n not in _SMALL_SHARDED}
    for j, n in enumerate(_SMALL_SHARDED):
        W[n] = jnp.concatenate([per_chip[p][j] for p in range(N_CHIPS)], axis=-1)
    for f in ("ffn1", "ffn2"):
        W[f + "_g"], W[f + "_u"], W[f + "_d"] = gw[f + "_w_gate"], gw[f + "_w_up"], gw[f + "_w_down"]
    cols = lambda g: jnp.concatenate([g[p] for p in range(N_CHIPS)], axis=-1)
    rows = lambda g: jnp.concatenate([g[p] for p in range(N_CHIPS)], axis=1)
    ab_in = cols(gw["ab_w_in"])
    ab_cols = ab_in.shape[-1]
    ab_pad = (-ab_cols) % LANES
    W["ab_in"] = jnp.pad(ab_in, ((0, 0), (0, 0), (0, ab_pad)))
    W["cd_in"], W["ab_out"], W["cd_out"] = cols(gw["cd_w_in"]), rows(gw["ab_w_out"]), rows(gw["cd_w_out"])

    loss, dx, big, small = _local_step(x.reshape(T, D), loss_target.reshape(T, D), W, B, S)
    loss = lax.psum(loss, ("x", "y", "c"))

    big["ab_w_in"] = big["ab_w_in"][:, :, :ab_cols]
    slots = []
    for n in _BIG:
        g = big[n] if big[n].ndim == 4 else _to_slots(n, big[n])
        L, P4, Aa, Bt = g.shape
        slots.append(g.reshape(L, P4, 2, Aa // 2, Bt))
    theirs = _pair_split("grads_pair_split", slots)
    pair = [_pair_sum(f"grads_pair_sum_{n}", g5, r, cf, BF16) for n, g5, r in zip(_BIG, slots, theirs)]
    parts = _chip_scatter("grads_chip_scatter", [p for p, _ in pair], [p for _, p in pair])
    tot = [_sum_chips(f"grads_chip_sum_{n}", p) for n, p in zip(_BIG, parts)]
    other = _pair_share("grads_pair_share", tot)

    sm_shapes = [small[n].shape for n in _SMALL]
    sm_local = _pack([small[n] for n in _SMALL])
    every = _gather_devices("gather_small_grads", sm_local, own(8, sm_local, 2 * chip + ci))
    grads = {}
    for n, g in zip(_SMALL, _unpack(_sum_slots("small_grads_sum", every), sm_shapes)):
        if n in _SMALL_SHARDED:
            w_loc = A[n].shape[-1]
            g = lax.dynamic_slice_in_dim(g, chip * w_loc, w_loc, axis=g.ndim - 1)
        grads[n] = g.reshape(A[n].shape)

    delta, new_m, new_v = {}, {}, {}
    for n, a, b in zip(_BIG, tot, other):
        grads[n], delta[n], new_m[n], new_v[n] = _adam_halves(f"adam_{n}", A[n], A["m_" + n], A["v_" + n], a, b, cf)
    packed = [_pack([d[n] for n in _SMALL]) for d in (A, grads, {n: A["m_" + n] for n in _SMALL},
                                                       {n: A["v_" + n] for n in _SMALL})]
    loc_shapes = [A[n].shape for n in _SMALL]
    for d, buf in zip((delta, new_m, new_v), _adam_rows("adam_small", *packed)):
        d.update(zip(_SMALL, _unpack(buf, loc_shapes)))

    return (loss, dx.reshape(B, S, D), *[grads[n] for n in _WEIGHTS], *[delta[n] for n in _WEIGHTS],
            *[new_m[n] for n in _WEIGHTS], *[new_v[n] for n in _WEIGHTS])
```

```python
import functools
import math

import jax
import jax.numpy as jnp
from jax import lax
from jax.experimental import pallas as pl
from jax.experimental.pallas import tpu as pltpu

F32, BF16 = jnp.float32, jnp.bfloat16
HIGHEST = lax.Precision.HIGHEST
MESH = pl.DeviceIdType.MESH

EPS = 1e-6
LANES = 128
CH = 64
PAIR = 2 * CH
POOL_WINDOWS = (2, 4, 8, 16)
N_CHIPS = 4
ADAM_LR, ADAM_B1, ADAM_B2, ADAM_EPS, ADAM_WD, ADAM_STEP = 0.001, 0.9, 0.999, 1e-08, 0.01, 10
VMEM_LIMIT = 63 * 1024 * 1024


def _pcall(body, **kw):
    return pl.pallas_call(body, **kw)


def _params(nd):
    return pltpu.CompilerParams(dimension_semantics=("arbitrary",) * nd, vmem_limit_bytes=VMEM_LIMIT)


def _sds(shape, dtype=F32):
    return jax.ShapeDtypeStruct(tuple(shape), dtype)


def _tile(n, cap, mult=LANES):
    if n <= cap:
        return n
    best = None
    for t in range(mult, cap + 1, mult):
        if n % t == 0:
            best = t
    assert best is not None, (n, cap)
    return best


def _dg(a, b, ca, cb):
    return lax.dot_general(a.astype(BF16), b.astype(BF16), (((ca,), (cb,)), ((), ())),
                           preferred_element_type=F32)


@jax.custom_vjp
def _bdot(a, b):
    return _dg(a, b, 1, 0)


def _bdot_fwd(a, b):
    return _dg(a, b, 1, 0), (a, b)


def _bdot_bwd(res, g):
    a, b = res
    return _dg(g, b, 1, 1), _dg(a, g, 0, 0)


_bdot.defvjp(_bdot_fwd, _bdot_bwd)


@jax.custom_vjp
def _bdot_nt(a, b):
    return _dg(a, b, 1, 1)


def _bdot_nt_fwd(a, b):
    return _dg(a, b, 1, 1), (a, b)


def _bdot_nt_bwd(res, g):
    a, b = res
    return _dg(g, b, 1, 0), _dg(g, a, 0, 0)


_bdot_nt.defvjp(_bdot_nt_fwd, _bdot_nt_bwd)


def _hdot(a, b):
    return jnp.dot(a, b, precision=HIGHEST, preferred_element_type=F32)


def _shift_raw(x, k):
    n = x.shape[0]
    rows = lax.broadcasted_iota(jnp.int32, x.shape, 0)
    r = pltpu.roll(x, k % n, 0)
    if k > 0:
        return jnp.where(rows >= k, r, 0.0)
    return jnp.where(rows < n + k, r, 0.0)


@functools.partial(jax.custom_vjp, nondiff_argnums=(1,))
def _shift(x, k):
    return _shift_raw(x, k)


def _shift_fwd(x, k):
    return _shift_raw(x, k), None


def _shift_bwd(k, _, g):
    return (_shift(g, -k),)


_shift.defvjp(_shift_fwd, _shift_bwd)


def _col(x, j):
    lanes = lax.broadcasted_iota(jnp.int32, x.shape, x.ndim - 1)
    return jnp.sum(jnp.where(lanes == j, x, 0.0), axis=-1, keepdims=True)


def _rms(x, g):
    return x * lax.rsqrt(jnp.mean(x * x, axis=-1, keepdims=True) + EPS) * g


def _sigmoid(x):
    return 1.0 / (1.0 + jnp.exp(-x))


def _silu(x):
    return x * _sigmoid(x)


def _sigmoid_fast(x):
    return pl.reciprocal(1.0 + jnp.exp(-x), approx=True)


def _softplus(x):
    return jnp.maximum(x, 0.0) + jnp.log(1.0 + jnp.exp(-jnp.abs(x)))


def _gelu(x):
    c = math.sqrt(2.0 / math.pi)
    return 0.5 * x * (1.0 + jnp.tanh(c * (x + 0.044715 * (x * x * x))))


def _first(axes):
    c = None
    for a in axes:
        t = pl.program_id(a) == 0
        c = t if c is None else jnp.logical_and(c, t)
    return c


def _fwd_call(name, fn, grid, ins, outs, acc_axes=()):
    nin = len(ins)

    def body(*refs):
        pids = tuple(pl.program_id(a) for a in range(len(grid)))
        first = _first(acc_axes) if acc_axes else None
        vals = [r[...].astype(F32) for r in refs[:nin]]
        res = fn(pids, *vals)
        for r, o, (_, _, acc) in zip(refs[nin:], res, outs):
            if acc:
                @pl.when(first)
                def _(r=r):
                    r[...] = jnp.zeros_like(r)
                r[...] += o.astype(r.dtype)
            else:
                r[...] = o.astype(r.dtype)

    return _pcall(body, grid=grid, in_specs=[s for _, s in ins], out_specs=[s for _, s, _ in outs],
                  out_shape=[s for s, _, _ in outs], name=name,
                  compiler_params=_params(len(grid)))(*[a for a, _ in ins])


def _bwd_call(name, fn, grid, ins, cts, gouts, acc_axes=(), addends=()):
    nin, nct, nadd = len(ins), len(cts), len(addends)

    def body(*refs):
        pids = tuple(pl.program_id(a) for a in range(len(grid)))
        first = _first(acc_axes) if acc_axes else None
        vals = [r[...].astype(F32) for r in refs[:nin]]
        ctv = tuple(r[...].astype(F32) for r in refs[nin:nin + nct])
        addv = {pos: refs[nin + nct + n][...].astype(F32) for n, (pos, _, _) in enumerate(addends)}
        _, vjp = jax.vjp(lambda *v: tuple(fn(pids, *v)), *vals)
        grads = vjp(ctv)
        for pos, (r, (idx, _, _, acc)) in enumerate(zip(refs[nin + nct + nadd:], gouts)):
            gval = grads[idx]
            if pos in addv:
                gval = gval + addv[pos]
            if acc:
                @pl.when(first)
                def _(r=r):
                    r[...] = jnp.zeros_like(r)
                r[...] += gval.astype(r.dtype)
            else:
                r[...] = gval.astype(r.dtype)

    args = [a for a, _ in ins] + [a for a, _ in cts] + [a for _, a, _ in addends]
    specs = [s for _, s in ins] + [s for _, s in cts] + [s for _, _, s in addends]
    return _pcall(body, grid=grid, in_specs=specs, out_specs=[s for _, _, s, _ in gouts],
                  out_shape=[s for _, s, _, _ in gouts], name=name,
                  compiler_params=_params(len(grid)))(*args)


def _mm(name, a, b, *, ta=False, tb=False, res=None, tm=512, tn=None, tk=None):
    if ta:
        K, M = a.shape
    else:
        M, K = a.shape
    N = b.shape[0] if tb else b.shape[1]
    tm, tn, tk = _tile(M, tm, 8), N if tn is None else tn, K if tk is None else min(tk, K)
    nk = K // tk
    assert res is None or nk == 1
    a_spec = pl.BlockSpec((tk, tm), lambda i, j, k: (k, i)) if ta else pl.BlockSpec((tm, tk), lambda i, j, k: (i, k))
    b_spec = pl.BlockSpec((tn, tk), lambda i, j, k: (j, k)) if tb else pl.BlockSpec((tk, tn), lambda i, j, k: (k, j))
    o_spec = pl.BlockSpec((tm, tn), lambda i, j, k: (i, j))
    ca, cb = (0 if ta else 1), (1 if tb else 0)

    def body(*refs):
        a_ref, b_ref, o_ref = refs[0], refs[1], refs[-1]
        part = _dg(a_ref[...], b_ref[...], ca, cb)
        if nk == 1:
            o_ref[...] = part if res is None else refs[2][...] + part
        else:
            @pl.when(pl.program_id(2) == 0)
            def _():
                o_ref[...] = jnp.zeros_like(o_ref)

            o_ref[...] += part

    args, specs = [a, b], [a_spec, b_spec]
    if res is not None:
        args.append(res)
        specs.append(o_spec)
    return _pcall(body, grid=(M // tm, N // tn, nk), in_specs=specs, out_specs=o_spec, out_shape=_sds((M, N)),
                  name=name, compiler_params=_params(3))(*args)


def _offsets(parts):
    widths = [p.shape[1] for p in parts]
    return widths, [sum(widths[:k]) for k in range(len(widths))]


def _mm_parts_dx(name, parts, w, tm=256):
    M, N = parts[0].shape[0], w.shape[0]
    widths, offs = _offsets(parts)
    tm = _tile(M, tm, 8)
    n = len(parts)

    def body(*refs):
        w_ref, o_ref = refs[n], refs[-1]
        terms = [_dg(p_ref[...], w_ref[:, off:off + wd], 1, 1) for p_ref, off, wd in zip(refs[:n], offs, widths)]
        o_ref[...] = functools.reduce(lambda p, q: p + q, terms)

    specs = [pl.BlockSpec((tm, wd), lambda i: (i, 0)) for wd in widths] + [pl.BlockSpec(w.shape, lambda i: (0, 0))]
    return _pcall(body, grid=(M // tm,), in_specs=specs, out_specs=pl.BlockSpec((tm, N), lambda i: (i, 0)),
                  out_shape=_sds((M, N)), name=name, compiler_params=_params(1))(*parts, w)


def _mm_parts_dw(name, a, parts, tk=512):
    K, M = a.shape
    widths, offs = _offsets(parts)
    tk = min(tk, K)

    def body(*refs):
        a_ref, o_ref = refs[0], refs[-1]

        @pl.when(pl.program_id(0) == 0)
        def _():
            o_ref[...] = jnp.zeros_like(o_ref)

        for p_ref, off, wd in zip(refs[1:-1], offs, widths):
            o_ref[:, off:off + wd] += _dg(a_ref[...], p_ref[...], 0, 0)

    specs = [pl.BlockSpec((tk, M), lambda k: (k, 0))] + [pl.BlockSpec((tk, wd), lambda k: (k, 0)) for wd in widths]
    return _pcall(body, grid=(K // tk,), in_specs=specs, out_specs=pl.BlockSpec((M, sum(widths)), lambda k: (0, 0)),
                  out_shape=_sds((M, sum(widths))), name=name, compiler_params=_params(1))(a, *parts)


FFN_PARTS = 2


def _ffn_fwd(name, h, g, wg, wu, wd, l, tm):
    T, D = h.shape
    ns, _, _, Fs = wg.shape

    def body(h_ref, g_ref, wg_ref, wu_ref, wd_ref, ho_ref, gate_ref, up_ref, hn_s, acc_s):
        s = pl.program_id(1)

        @pl.when(s == 0)
        def _():
            hn_s[...] = _rms(h_ref[...], g_ref[...]).astype(BF16)
            acc_s[...] = jnp.zeros_like(acc_s)

        parts = [pl.ds(k * (tm // FFN_PARTS), tm // FFN_PARTS) for k in range(FFN_PARTS)]
        gu = [(_dg(hn_s[r, :], wg_ref[...], 1, 0), _dg(hn_s[r, :], wu_ref[...], 1, 0)) for r in parts]
        for r, (gate, up) in zip(parts, gu):
            gate_ref[r, :] = gate.astype(BF16)
            up_ref[r, :] = up.astype(BF16)
            act = (gate * _sigmoid_fast(gate) * up).astype(BF16)
            acc_s[r, :] += _dg(act, wd_ref[...], 1, 0)

        @pl.when(s == ns - 1)
        def _():
            ho_ref[...] = h_ref[...] + 0.5 * acc_s[...]

    row = pl.BlockSpec((tm, D), lambda i, s: (i, 0))
    wcol = pl.BlockSpec((None, None, D, Fs), lambda i, s: (s, l, 0, 0))
    wrow = pl.BlockSpec((None, None, Fs, D), lambda i, s: (s, l, 0, 0))
    slot = pl.BlockSpec((None, tm, Fs), lambda i, s: (s, i, 0))
    return _pcall(body, grid=(T // tm, ns), in_specs=[row, pl.BlockSpec((1, D), lambda i, s: (0, 0)), wcol, wcol, wrow],
                  out_specs=[row, slot, slot, row],
                  out_shape=[_sds((T, D)), _sds((ns, T, Fs), BF16), _sds((ns, T, Fs), BF16), _sds((T, D), BF16)],
                  scratch_shapes=[pltpu.VMEM((tm, D), F32)], name=name,
                  compiler_params=_params(2))(h, g, wg, wu, wd)


def _ffn_bwd_x(name, dh, h, g, gate, up, wg, wu, wd, l, tm):
    T, D = h.shape
    ns, _, _, Fs = wg.shape

    def body(dh_ref, h_ref, g_ref, gate_ref, up_ref, wg_ref, wu_ref, wd_ref,
             dho_ref, dgate_ref, dup_ref, dgain_ref, do_s, acc_s):
        i, s = pl.program_id(0), pl.program_id(1)

        @pl.when(s == 0)
        def _():
            do_s[...] = (0.5 * dh_ref[...]).astype(BF16)
            acc_s[...] = jnp.zeros_like(acc_s)

        @pl.when(jnp.logical_and(i == 0, s == 0))
        def _():
            dgain_ref[...] = jnp.zeros_like(dgain_ref)

        parts = [pl.ds(k * (tm // FFN_PARTS), tm // FFN_PARTS) for k in range(FFN_PARTS)]
        dact = [_dg(do_s[r, :], wd_ref[...], 1, 1) for r in parts]
        for r, da in zip(parts, dact):
            gt, u = gate_ref[r, :].astype(F32), up_ref[r, :].astype(F32)
            sg = _sigmoid_fast(gt)
            dgt = (da * u * (sg * (1.0 + gt * (1.0 - sg)))).astype(BF16)
            du = (da * (gt * sg)).astype(BF16)
            dgate_ref[r, :] = dgt
            dup_ref[r, :] = du
            acc_s[r, :] += _dg(dgt, wg_ref[...], 1, 1) + _dg(du, wu_ref[...], 1, 1)

        @pl.when(s == ns - 1)
        def _():
            _, vjp = jax.vjp(_rms, h_ref[...], g_ref[...])
            dx, dgain = vjp(acc_s[...])
            dho_ref[...] = dh_ref[...] + dx
            dgain_ref[...] += dgain

    row = pl.BlockSpec((tm, D), lambda i, s: (i, 0))
    gain = pl.BlockSpec((1, D), lambda i, s: (0, 0))
    wcol = pl.BlockSpec((None, None, D, Fs), lambda i, s: (s, l, 0, 0))
    wrow = pl.BlockSpec((None, None, Fs, D), lambda i, s: (s, l, 0, 0))
    slot = pl.BlockSpec((None, tm, Fs), lambda i, s: (s, i, 0))
    return _pcall(body, grid=(T // tm, ns), in_specs=[row, row, gain, slot, slot, wcol, wcol, wrow],
                  out_specs=[row, slot, slot, gain, row],
                  out_shape=[_sds((T, D)), _sds((ns, T, Fs), BF16), _sds((ns, T, Fs), BF16), _sds((1, D)),
                             _sds((T, D), BF16)],
                  scratch_shapes=[pltpu.VMEM((tm, D), F32)], name=name,
                  compiler_params=_params(2))(dh, h, g, gate, up, wg, wu, wd)


def _ffn_bwd_w(name, hn, do, gate, up, dgate, dup, bufs, l, nl, tm):
    T, D = hn.shape
    ns, _, Fs = gate.shape

    def body(hn_ref, do_ref, gate_ref, up_ref, dgate_ref, dup_ref, dwg_ref, dwu_ref, dwd_ref):
        @pl.when(pl.program_id(1) == 0)
        def _():
            dwg_ref[...] = jnp.zeros_like(dwg_ref)
            dwu_ref[...] = jnp.zeros_like(dwu_ref)
            dwd_ref[...] = jnp.zeros_like(dwd_ref)

        def prep(r):
            gt = gate_ref[r, :].astype(F32)
            act = (gt * _sigmoid_fast(gt) * up_ref[r, :].astype(F32)).astype(BF16)
            return hn_ref[r, :], act, do_ref[r, :]

        parts = [pl.ds(k * (tm // FFN_PARTS), tm // FFN_PARTS) for k in range(FFN_PARTS)]
        ready = [prep(r) for r in parts]
        add = lambda xs: functools.reduce(lambda p, q: p + q, xs)
        dwg_ref[...] += add([_dg(hn, dgate_ref[r, :], 0, 0) for r, (hn, _, _) in zip(parts, ready)])
        dwu_ref[...] += add([_dg(hn, dup_ref[r, :], 0, 0) for r, (hn, _, _) in zip(parts, ready)])
        dwd_ref[...] += add([_dg(act, do, 0, 0) for _, act, do in ready])

    row = pl.BlockSpec((tm, D), lambda s, j: (j, 0))
    gain = pl.BlockSpec((1, D), lambda s, j: (0, 0))
    slot = pl.BlockSpec((None, tm, Fs), lambda s, j: (s, j, 0))
    wcol = pl.BlockSpec((None, None, D, Fs), lambda s, j: (l, s, 0, 0))
    wrow = pl.BlockSpec((None, None, Fs, D), lambda s, j: (l, s, 0, 0))
    args, specs, alias = [hn, do, gate, up, dgate, dup], [row, row, slot, slot, slot, slot], {}
    if bufs is not None:
        alias = {len(args) + k: k for k in range(3)}
        args, specs = args + list(bufs), specs + [pl.BlockSpec(memory_space=pl.ANY)] * 3
    return _pcall(lambda *refs: body(*refs[:6], *refs[-3:]), grid=(ns, T // tm), in_specs=specs,
                  out_specs=[wcol, wcol, wrow], input_output_aliases=alias,
                  out_shape=[_sds((nl, ns, D, Fs)), _sds((nl, ns, D, Fs)), _sds((nl, ns, Fs, D))], name=name,
                  compiler_params=_params(2))(*args)


def _rms_fn(pids, x, g):
    return (_rms(x, g),)


def _pool_fn(pids, a, w, scale):
    rows = lax.broadcasted_iota(jnp.int32, (a.shape[0], LANES), 0)
    outs = []
    for gi, win in enumerate(POOL_WINDOWS):
        ag = a[:, gi * LANES:(gi + 1) * LANES]
        s, k = ag, 1
        while k < win:
            s = s + _shift(s, k)
            k *= 2
        cnt = jnp.minimum(rows + 1, win).astype(F32)
        pooled = s / cnt - ag
        outs.append(_bdot(pooled, w[gi]) * scale[:, gi * LANES:(gi + 1) * LANES])
    return (jnp.concatenate(outs, axis=-1),)


def _conv_taps(x, cw):
    K = cw.shape[0]
    y = cw[K - 1] * x
    for j in range(K - 1):
        y = y + cw[j] * _shift(x, K - 1 - j)
    return y


def _prep_fn(pids, q, k, v, bg, cwq, cwk, cwv, alog, dtb):
    hd = pids[0]
    q, k, v = _silu(_conv_taps(q, cwq)), _silu(_conv_taps(k, cwk)), _silu(_conv_taps(v, cwv))
    q = q * lax.rsqrt(jnp.sum(q * q, axis=-1, keepdims=True) + EPS) * (LANES ** -0.5)
    k = k * lax.rsqrt(jnp.sum(k * k, axis=-1, keepdims=True) + EPS)
    beta = _sigmoid(_col(bg, hd))
    g = -jnp.exp(_col(alog, hd)) * _softplus(_col(bg, hd + 4) + _col(dtb, hd))
    return q, k, v, jnp.broadcast_to(beta, q.shape), jnp.broadcast_to(g, q.shape)


def _post_fn(pids, o, z, onorm):
    return (_rms(o, onorm) * _silu(z),)


def _sgu_fn(pids, u, v, g, b, ws, bias_t):
    u, v = _gelu(u), _gelu(v)
    mu = jnp.mean(v, axis=-1, keepdims=True)
    xc = v - mu
    vn = xc * lax.rsqrt(jnp.mean(xc * xc, axis=-1, keepdims=True) + EPS) * g + b
    r = lax.broadcasted_iota(jnp.int32, (LANES, LANES), 0)
    c = lax.broadcasted_iota(jnp.int32, (LANES, LANES), 1)
    rows = []
    for n in range(u.shape[0] // LANES):
        heads = []
        for hd in range(4):
            wm = jnp.where(r >= c, ws[hd], 0.0)
            blk = vn[n * LANES:(n + 1) * LANES, hd * LANES:(hd + 1) * LANES]
            heads.append(_bdot(wm, blk) + _col(bias_t, hd))
        rows.append(jnp.concatenate(heads, axis=-1))
    mixed = jnp.concatenate(rows, axis=0) if len(rows) > 1 else rows[0]
    return (u * mixed,)


def _sconv_fn(pids, xd, bgate, cg, cw):
    return (bgate * _conv_taps(cg * xd, cw),)


def _loss_fn(pids, h, g, tgt):
    err = _rms(h, g) - tgt
    tot = 0.5 * jnp.sum(jnp.mean(err * err, axis=-1, keepdims=True), axis=0, keepdims=True)
    return (jnp.broadcast_to(tot, (1, LANES)),)


DELTA_PAIRS = 4
DELTA_ROWS = DELTA_PAIRS * PAIR


def _each(fn, *lists):
    return [fn(*args) for args in zip(*lists)]


def _chunk_cumsum(g):
    pos = jnp.bitwise_and(lax.broadcasted_iota(jnp.int32, g.shape, 0), CH - 1)
    s, k = g, 1
    while k < CH:
        s = s + jnp.where(pos >= k, _shift(s, k), 0.0)
        k *= 2
    tot = [jnp.broadcast_to(jnp.sum(g[j * CH:(j + 1) * CH], axis=0, keepdims=True), (CH, g.shape[1])) for j in (0, 1)]
    return s, jnp.concatenate(tot, axis=0)


def _stage_a(blocks):
    q, k, v, beta, g = (list(x) for x in zip(*blocks))
    r = lax.broadcasted_iota(jnp.int32, (PAIR, PAIR), 0)
    c = lax.broadcasted_iota(jnp.int32, (PAIR, PAIR), 1)
    same = jnp.right_shift(r, 6) == jnp.right_shift(c, 6)
    tri = jnp.logical_and(same, r >= c)
    strict = jnp.logical_and(same, r > c)
    eye = (r == c).astype(F32)
    sums = _each(_chunk_cumsum, g)
    gc, gt = [a for a, _ in sums], [b for _, b in sums]
    gamma = _each(lambda x: jnp.exp(jnp.where(tri, x - x.T, -jnp.inf)), gc)
    kb = _each(lambda a, b: a * b, k, beta)
    kk = _each(_bdot_nt, kb, k)
    p = _each(lambda a, gm: -jnp.where(strict, a * gm, 0.0), kk, gamma)
    tinv = _each(lambda x: eye + x, p)
    for _ in range(5):
        p = _each(_bdot, p, p)
        tinv = _each(lambda t, x: t + x, tinv, _each(_bdot, tinv, p))
    egc = _each(jnp.exp, gc)
    u = _each(_bdot, tinv, _each(lambda a, b: a * b, v, beta))
    w = _each(_bdot, tinv, _each(lambda a, e: a * e, kb, egc))
    a = _each(lambda x, gm: x * gm, _each(_bdot_nt, q, k), gamma)
    qd = _each(lambda x, e: x * e, q, egc)
    kd = _each(lambda x, t, s: x * jnp.exp(t - s), k, gt, gc)
    return list(zip(u, w, qd, a, kd, _each(jnp.exp, gt)))


def _nn(a, b):
    return _dg(a, b, 1, 0)


def _nt(a, b):
    return _dg(a, b, 1, 1)


def _tn(a, b):
    return _dg(a, b, 0, 0)


def _scan_pair_fwd(s0, u, w, qd, a, kd, l0, l1):
    lo = lambda xs: [x[:CH] for x in xs]
    hi = lambda xs: [x[CH:] for x in xs]
    vn0 = _each(lambda x, y: x - y, lo(u), _each(_nn, lo(w), s0))
    s1 = _each(lambda s, l, x: s * l + x, s0, l0, _each(_tn, lo(kd), vn0))
    vn1 = _each(lambda x, y: x - y, hi(u), _each(_nn, hi(w), s1))
    s2 = _each(lambda s, l, x: s * l + x, s1, l1, _each(_tn, hi(kd), vn1))
    inter = _each(lambda x, y: jnp.concatenate([x, y], axis=0), _each(_nn, lo(qd), s0), _each(_nn, hi(qd), s1))
    intra = _each(_nn, a, _each(lambda x, y: jnp.concatenate([x, y], axis=0), vn0, vn1))
    return _each(lambda x, y: x + y, inter, intra), s1, s2


def _scan_chunk_bwd(s, ds_next, do, dvn_o, u, w, qd, kd, lrow):
    vn = _each(lambda x, y: x - y, u, _each(_nn, w, s))
    dvn = _each(lambda x, y: x + y, dvn_o, _each(_nn, kd, ds_next))
    dkd = _each(_nt, vn, ds_next)
    dl = _each(lambda a, b: jnp.sum(a * b, axis=0, keepdims=True), ds_next, s)
    dw = _each(lambda x: -x, _each(_nt, dvn, s))
    dqd = _each(_nt, do, s)
    ds = _each(lambda x, l, d, y: x + l * d - y, _each(_tn, qd, do), lrow, ds_next, _each(_tn, w, dvn))
    return ds, vn, dvn, dw, dqd, dkd, dl


def _delta_blocks(refs):
    return [tuple(r[p * PAIR:(p + 1) * PAIR, h * LANES:(h + 1) * LANES] for r in refs)
            for p in range(DELTA_PAIRS) for h in range(4)]


def _delta_specs(B, S, reverse):
    assert S % DELTA_ROWS == 0, (S, DELTA_ROWS)
    nstep = S // DELTA_ROWS
    at =(lambda i: nstep - 1 - i) if reverse else (lambda i: i)
    blk = pl.BlockSpec((DELTA_ROWS, 4 * LANES), lambda b, i: (b * nstep + at(i), 0))
    st = pl.BlockSpec((None, 4, 2 * DELTA_PAIRS, LANES, LANES), lambda b, i: (b, 0, at(i), 0, 0))
    scratch = [pltpu.VMEM((4, LANES, LANES), F32), pltpu.VMEM((4 * DELTA_PAIRS, PAIR, LANES), F32)]
    return nstep, blk, st, scratch


def _delta_fwd(name, qn, kn, vn, beta, g, B, S):
    T = qn.shape[0]
    nstep, blk, st, scratch = _delta_specs(B, S, False)

    def body(q_ref, k_ref, v_ref, b_ref, g_ref, o_ref, st_ref, s_s, l_s):
        @pl.when(pl.program_id(1) == 0)
        def _():
            s_s[...] = jnp.zeros_like(s_s)

        outs = _stage_a(_delta_blocks((q_ref, k_ref, v_ref, b_ref, g_ref)))
        for j, blk_out in enumerate(outs):
            l_s[j] = blk_out[5]
        state = [s_s[h] for h in range(4)]
        for p in range(DELTA_PAIRS):
            u, w, qd, a, kd, _ = (list(x) for x in zip(*outs[4 * p:4 * p + 4]))
            l0 = [l_s[4 * p + h, 0:1, :] for h in range(4)]
            l1 = [l_s[4 * p + h, CH:CH + 1, :] for h in range(4)]
            o, mid, end = _scan_pair_fwd(state, u, w, qd, a, kd, l0, l1)
            for h in range(4):
                o_ref[p * PAIR:(p + 1) * PAIR, h * LANES:(h + 1) * LANES] = o[h]
                st_ref[h, 2 * p] = state[h]
                st_ref[h, 2 * p + 1] = mid[h]
            state = end
        for h in range(4):
            s_s[h] = state[h]

    return _pcall(body, grid=(B, nstep), in_specs=[blk] * 5, out_specs=[blk, st],
                  out_shape=[_sds((T, 4 * LANES)), _sds((B, 4, S // CH, LANES, LANES))],
                  scratch_shapes=scratch, name=name, compiler_params=_params(2))(qn, kn, vn, beta, g)


def _delta_bwd(name, qn, kn, vn, beta, g, states, do, B, S):
    T = qn.shape[0]
    nstep, blk, st, scratch = _delta_specs(B, S, True)

    def body(q_ref, k_ref, v_ref, b_ref, g_ref, st_ref, do_ref, dq_ref, dk_ref, dv_ref, db_ref, dg_ref, ds_s, l_s):
        @pl.when(pl.program_id(1) == 0)
        def _():
            ds_s[...] = jnp.zeros_like(ds_s)

        rowid = lax.broadcasted_iota(jnp.int32, (PAIR, LANES), 0)
        lo = lambda xs: [x[:CH] for x in xs]
        hi = lambda xs: [x[CH:] for x in xs]
        cat = lambda xs, ys: _each(lambda x, y: jnp.concatenate([x, y], axis=0), xs, ys)
        outs, vjp = jax.vjp(_stage_a, _delta_blocks((q_ref, k_ref, v_ref, b_ref, g_ref)))
        for j, blk_out in enumerate(outs):
            l_s[j] = blk_out[5]
        ds = [ds_s[h] for h in range(4)]
        cts = [None] * (4 * DELTA_PAIRS)
        for p in reversed(range(DELTA_PAIRS)):
            u, w, qd, a, kd, _ = (list(x) for x in zip(*outs[4 * p:4 * p + 4]))
            l0 = [l_s[4 * p + h, 0:1, :] for h in range(4)]
            l1 = [l_s[4 * p + h, CH:CH + 1, :] for h in range(4)]
            s0 = [st_ref[h, 2 * p] for h in range(4)]
            s1 = [st_ref[h, 2 * p + 1] for h in range(4)]
            dout = [do_ref[p * PAIR:(p + 1) * PAIR, h * LANES:(h + 1) * LANES] for h in range(4)]
            dvn_o = _each(_tn, a, dout)
            ds1, vn1, dvn1, dw1, dqd1, dkd1, dl1 = _scan_chunk_bwd(s1, ds, hi(dout), hi(dvn_o), hi(u), hi(w), hi(qd),
                                                                   hi(kd), l1)
            ds, vn0, dvn0, dw0, dqd0, dkd0, dl0 = _scan_chunk_bwd(s0, ds1, lo(dout), lo(dvn_o), lo(u), lo(w), lo(qd),
                                                                  lo(kd), l0)
            da = _each(_nt, dout, cat(vn0, vn1))
            dl = _each(lambda x, y: jnp.where(rowid == 0, x, jnp.where(rowid == CH, y, 0.0)), dl0, dl1)
            for h, ct in enumerate(zip(cat(dvn0, dvn1), cat(dw0, dw1), cat(dqd0, dqd1), da, cat(dkd0, dkd1), dl)):
                cts[4 * p + h] = ct
        for h in range(4):
            ds_s[h] = ds[h]
        (grads,) = vjp(cts)
        for j, blk_grads in enumerate(grads):
            p, h = divmod(j, 4)
            for ref, val in zip((dq_ref, dk_ref, dv_ref, db_ref, dg_ref), blk_grads):
                ref[p * PAIR:(p + 1) * PAIR, h * LANES:(h + 1) * LANES] = val

    return _pcall(body, grid=(B, nstep), in_specs=[blk] * 5 + [st, blk], out_specs=[blk] * 5,
                  out_shape=[_sds((T, 4 * LANES))] * 5, scratch_shapes=scratch, name=name,
                  compiler_params=_params(2))(qn, kn, vn, beta, g, states, do)


ROW_TILE = 512
FFN_TILE = 1024
FFN_TILE_BX = 1024
SGU_ROWS = 256


def _rows(r, c, off=0):
    return pl.BlockSpec((r, c), lambda i: (i, off))


def _whole(shape, nd):
    zeros = (0,) * len(shape)
    if nd == 1:
        return pl.BlockSpec(shape, lambda i: zeros)
    return pl.BlockSpec(shape, lambda i, j: zeros)


def _norm_fwd(name, h, g):
    T, D = h.shape
    tr = _tile(T, ROW_TILE, 8)
    return _fwd_call(name, _rms_fn, (T // tr,), [(h, _rows(tr, D)), (g, _whole((1, D), 1))],
                     [(_sds((T, D), BF16), _rows(tr, D), False)])[0]


def _norm_bwd(name, h, g, dhn, dh_res):
    T, D = h.shape
    tr = _tile(T, ROW_TILE, 8)
    return _bwd_call(name, _rms_fn, (T // tr,), [(h, _rows(tr, D)), (g, _whole((1, D), 1))], [(dhn, _rows(tr, D))],
                     [(0, _sds((T, D)), _rows(tr, D), False), (1, _sds((1, D)), _whole((1, D), 1), True)],
                     acc_axes=(0,), addends=[(0, dh_res, _rows(tr, D))])


def _ab_specs(B, S, P):
    W4 = 4 * LANES
    seq4 = pl.BlockSpec((S, W4), lambda b: (b, 0))
    pool_ins = lambda proj: [(proj, seq4), (P["pool_w"], _whole((4, LANES, LANES), 1)), (P["pool_scale"], _whole((1, W4), 1))]
    hb = lambda off: pl.BlockSpec((S, LANES), lambda b, h: (b, off + h))
    cw = lambda off: pl.BlockSpec((4, 1, LANES), lambda b, h: (0, 0, off + h))
    small = _whole((1, LANES), 2)
    prep_ins = lambda proj: [(proj, hb(4)), (proj, hb(8)), (proj, hb(12)),
                             (proj, pl.BlockSpec((S, LANES), lambda b, h: (b, 20))),
                             (P["conv_w"], cw(0)), (P["conv_w"], cw(4)), (P["conv_w"], cw(8)),
                             (P["a_log"], small), (P["dt_bias"], small)]
    post_ins = lambda o, proj: [(o, hb(0)), (proj, hb(16)), (P["out_norm"], small)]
    return seq4, pool_ins, hb, prep_ins, post_ins, small


def _ab_fwd(tag, h, P, B, S):
    T, D = h.shape
    W4 = 4 * LANES
    seq4, pool_ins, hb, prep_ins, post_ins, small = _ab_specs(B, S, P)
    hn = _norm_fwd(tag + "_norm", h, P["mix_norm"])
    proj = _mm(tag + "_in", hn, P["w_in"], tm=256)
    ya = _fwd_call(tag + "_pool", _pool_fn, (B,), pool_ins(proj), [(_sds((T, W4), BF16), seq4, False)])[0]
    qn, kn, vn, beta, g = _fwd_call(tag + "_prep", lambda pids, *v: _prep_fn((pids[1],), *v), (B, 4), prep_ins(proj),
                                    [(_sds((T, W4)), hb(0), False)] * 5)
    o, states = _delta_fwd(tag + "_delta", qn, kn, vn, beta, g, B, S)
    yb = _fwd_call(tag + "_post", _post_fn, (B, 4), post_ins(o, proj), [(_sds((T, W4), BF16), hb(0), False)])[0]
    y = jnp.concatenate([ya, yb], axis=-1)
    h_new = _mm(tag + "_out", y, P["w_out"], res=h)
    return h_new, (h, hn, proj, qn, kn, vn, beta, g, states, o, y)


def _ab_bwd(tag, dh, saved, P, B, S):
    h, hn, proj, qn, kn, vn, beta, g, states, o, y = saved
    T, D = h.shape
    W4 = 4 * LANES
    seq4, pool_ins, hb, prep_ins, post_ins, small = _ab_specs(B, S, P)
    dy = _mm(tag + "_out_dx", dh, P["w_out"], tb=True)
    dw_out = _mm(tag + "_out_dw", y, dh, ta=True, tm=D, tk=1024)
    do, dz, d_onorm = _bwd_call(tag + "_post_b", _post_fn, (B, 4), post_ins(o, proj), [(dy, hb(4))],
                                [(0, _sds((T, W4)), hb(0), False), (1, _sds((T, W4)), hb(0), False),
                                 (2, _sds((1, LANES)), small, True)], acc_axes=(0, 1))
    dqn, dkn, dvn, dbeta, dg = _delta_bwd(tag + "_delta_b", qn, kn, vn, beta, g, states, do, B, S)
    per_b = pl.BlockSpec((None, 1, LANES), lambda b, h: (b, 0, 0))
    dcw = pl.BlockSpec((None, 4, 1, LANES), lambda b, h: (b, 0, 0, h))
    dq, dk, dv, dbg, dcq, dck, dcv, dalog, ddt = _bwd_call(
        tag + "_prep_b", lambda pids, *v: _prep_fn((pids[1],), *v), (B, 4), prep_ins(proj),
        [(dqn, hb(0)), (dkn, hb(0)), (dvn, hb(0)), (dbeta, hb(0)), (dg, hb(0))],
        [(0, _sds((T, W4)), hb(0), False), (1, _sds((T, W4)), hb(0), False), (2, _sds((T, W4)), hb(0), False),
         (3, _sds((T, LANES)), pl.BlockSpec((S, LANES), lambda b, h: (b, 0)), True),
         (4, _sds((B, 4, 1, W4)), dcw, False), (5, _sds((B, 4, 1, W4)), dcw, False),
         (6, _sds((B, 4, 1, W4)), dcw, False),
         (7, _sds((B, 1, LANES)), per_b, True), (8, _sds((B, 1, LANES)), per_b, True)], acc_axes=(1,))
    d_conv = jnp.concatenate([jnp.sum(d, axis=0)[:, 0, :] for d in (dcq, dck, dcv)], axis=-1)
    da, dpool_w, dpool_scale = _bwd_call(
        tag + "_pool_b", _pool_fn, (B,), pool_ins(proj), [(dy, seq4)],
        [(0, _sds((T, W4)), seq4, False), (1, _sds((4, LANES, LANES)), _whole((4, LANES, LANES), 1), True),
         (2, _sds((1, W4)), _whole((1, W4), 1), True)], acc_axes=(0,))
    dproj = [da, dq, dk, dv, dz, dbg]
    dhn = _mm_parts_dx(tag + "_in_dx", dproj, P["w_in"])
    dw_in = _mm_parts_dw(tag + "_in_dw", hn, dproj)
    dh_new, d_mix = _norm_bwd(tag + "_norm_b", h, P["mix_norm"], dhn, dh)
    grads = dict(w_in=dw_in, w_out=dw_out, mix_norm=d_mix[0], pool_w=dpool_w, pool_scale=dpool_scale[0],
                 conv_w=d_conv, a_log=jnp.sum(dalog, axis=0)[0, :4], dt_bias=jnp.sum(ddt, axis=0)[0, :4],
                 out_norm=d_onorm[0])
    return dh_new, grads


def _cd_specs(B, S, T, P):
    W4 = 4 * LANES
    R = _tile(T, SGU_ROWS, LANES)
    sgu_ins = lambda proj: [(proj, _rows(R, W4, 0)), (proj, _rows(R, W4, 1)), (P["sgu_g"], _whole((1, W4), 1)),
                            (P["sgu_b"], _whole((1, W4), 1)), (P["sgu_w"], _whole((4, LANES, LANES), 1)),
                            (P["bias_t"], _whole((LANES, LANES), 1))]
    jb = lambda off: pl.BlockSpec((S, LANES), lambda j, b: (b, off + j))
    sc_ins = lambda proj: [(proj, jb(8)), (proj, jb(12)), (proj, jb(16)),
                           (P["sc_w"], pl.BlockSpec((3, 1, LANES), lambda j, b: (0, 0, j)))]
    return R, sgu_ins, jb, sc_ins


def _cd_fwd(tag, h, P, B, S):
    T, D = h.shape
    W4 = 4 * LANES
    R, sgu_ins, jb, sc_ins = _cd_specs(B, S, T, P)
    hn = _norm_fwd(tag + "_norm", h, P["mix_norm"])
    proj = _mm(tag + "_in", hn, P["w_in"], tm=256)
    yc = _fwd_call(tag + "_sgu", _sgu_fn, (T // R,), sgu_ins(proj), [(_sds((T, W4), BF16), _rows(R, W4), False)])[0]
    yd = _fwd_call(tag + "_sconv", _sconv_fn, (4, B), sc_ins(proj), [(_sds((T, W4), BF16), jb(0), False)])[0]
    y = jnp.concatenate([yc, yd], axis=-1)
    h_new = _mm(tag + "_out", y, P["w_out"], res=h)
    return h_new, (h, hn, proj, y)


def _cd_bwd(tag, dh, saved, P, B, S):
    h, hn, proj, y = saved
    T, D = h.shape
    W4 = 4 * LANES
    R, sgu_ins, jb, sc_ins = _cd_specs(B, S, T, P)
    dy = _mm(tag + "_out_dx", dh, P["w_out"], tb=True)
    dw_out = _mm(tag + "_out_dw", y, dh, ta=True, tm=D, tk=1024)
    du, dv, dsg, dsb, dsw, dbias_t = _bwd_call(
        tag + "_sgu_b", _sgu_fn, (T // R,), sgu_ins(proj), [(dy, _rows(R, W4, 0))],
        [(0, _sds((T, W4)), _rows(R, W4), False), (1, _sds((T, W4)), _rows(R, W4), False),
         (2, _sds((1, W4)), _whole((1, W4), 1), True), (3, _sds((1, W4)), _whole((1, W4), 1), True),
         (4, _sds((4, LANES, LANES)), _whole((4, LANES, LANES), 1), True),
         (5, _sds((LANES, LANES)), _whole((LANES, LANES), 1), True)], acc_axes=(0,))
    dxd, dbgate, dcg, dsc = _bwd_call(
        tag + "_sconv_b", _sconv_fn, (4, B), sc_ins(proj), [(dy, jb(4))],
        [(0, _sds((T, W4)), jb(0), False), (1, _sds((T, W4)), jb(0), False), (2, _sds((T, W4)), jb(0), False),
         (3, _sds((3, 1, W4)), pl.BlockSpec((3, 1, LANES), lambda j, b: (0, 0, j)), True)], acc_axes=(1,))
    dproj = [du, dv, dxd, dbgate, dcg]
    dhn = _mm_parts_dx(tag + "_in_dx", dproj, P["w_in"])
    dw_in = _mm_parts_dw(tag + "_in_dw", hn, dproj)
    dh_new, d_mix = _norm_bwd(tag + "_norm_b", h, P["mix_norm"], dhn, dh)
    grads = dict(w_in=dw_in, w_out=dw_out, mix_norm=d_mix[0], sgu_g=dsg[0], sgu_b=dsb[0], sgu_w=dsw,
                 sgu_bias=dbias_t[:, :4].T, sc_w=dsc[:, 0, :])
    return dh_new, grads


def _loss_fwd_bwd(h, g, tgt):
    T, D = h.shape
    tr = _tile(T, ROW_TILE, 8)
    ins = [(h, _rows(tr, D)), (g, _whole((1, D), 1)), (tgt, _rows(tr, D))]
    vec = _whole((1, LANES), 1)
    loss = _fwd_call("loss", _loss_fn, (T // tr,), ins, [(_sds((1, LANES)), vec, True)], acc_axes=(0,))[0]
    one = jnp.zeros((1, LANES), F32).at[0, 0].set(1.0)
    dh, dg = _bwd_call("loss_b", _loss_fn, (T // tr,), ins, [(one, vec)],
                       [(0, _sds((T, D)), _rows(tr, D), False), (1, _sds((1, D)), _whole((1, D), 1), True)],
                       acc_axes=(0,))
    return loss[0, 0], dh, dg[0]


def _local_step(x2, tgt2, W, B, S):
    L = W["ffn1_norm"].shape[0]
    tm = _tile(x2.shape[0], FFN_TILE, 8)
    tmx = _tile(x2.shape[0], FFN_TILE_BX, 8)
    h = x2
    saved = []
    for l in range(L):
        e = l // 2
        f1 = (W["ffn1_norm"][l][None], W["ffn1_g"], W["ffn1_u"], W["ffn1_d"])
        f2 = (W["ffn2_norm"][l][None], W["ffn2_g"], W["ffn2_u"], W["ffn2_d"])
        h0 = h
        h, gate1, up1, hn1 = _ffn_fwd(f"l{l}_ffn1", h0, *f1, l, tm)
        if l % 2 == 0:
            P = dict(mix_norm=W["mix_norm"][l][None], w_in=W["ab_in"][e], w_out=W["ab_out"][e], pool_w=W["pool_w"][e],
                     pool_scale=W["pool_scale"][e][None], conv_w=W["dn_conv_w"][e][:, None, :],
                     a_log=jnp.pad(W["dn_a_log"][e][None], ((0, 0), (0, LANES - 4))),
                     dt_bias=jnp.pad(W["dn_dt_bias"][e][None], ((0, 0), (0, LANES - 4))),
                     out_norm=W["dn_out_norm"][e][None])
            h1 = h
            h, msave = _ab_fwd(f"l{l}_ab", h1, P, B, S)
        else:
            P = dict(mix_norm=W["mix_norm"][l][None], w_in=W["cd_in"][e], w_out=W["cd_out"][e],
                     sgu_g=W["sgu_norm_g"][e][None], sgu_b=W["sgu_norm_b"][e][None], sgu_w=W["sgu_w"][e],
                     bias_t=jnp.pad(W["sgu_bias"][e].T, ((0, 0), (0, LANES - 4))),
                     sc_w=W["sc_conv_w"][e][:, None, :])
            h1 = h
            h, msave = _cd_fwd(f"l{l}_cd", h1, P, B, S)
        h2 = h
        h, gate2, up2, hn2 = _ffn_fwd(f"l{l}_ffn2", h2, *f2, l, tm)
        saved.append((f1, f2, P, h0, gate1, up1, hn1, msave, h2, gate2, up2, hn2))

    loss, dh, d_final = _loss_fwd_bwd(h, W["final_norm"][None], tgt2)

    G = {k: [None] * L for k in ("ffn1_norm", "ffn2_norm", "mix_norm")}
    bufs1 = bufs2 = None
    GA, GC = [None] * ((L + 1) // 2), [None] * (L // 2)
    for l in reversed(range(L)):
        f1, f2, P, h0, gate1, up1, hn1, msave, h2, gate2, up2, hn2 = saved[l]
        dh, dgate, dup, dn2, do = _ffn_bwd_x(f"l{l}_ffn2_bx", dh, h2, f2[0], gate2, up2, f2[1], f2[2], f2[3], l, tmx)
        bufs2 = _ffn_bwd_w(f"l{l}_ffn2_bw", hn2, do, gate2, up2, dgate, dup, bufs2, l, L, tm)
        G["ffn2_norm"][l] = dn2[0]
        if l % 2 == 0:
            dh, mg = _ab_bwd(f"l{l}_ab", dh, msave, P, B, S)
            GA[l // 2] = mg
        else:
            dh, mg = _cd_bwd(f"l{l}_cd", dh, msave, P, B, S)
            GC[l // 2] = mg
        G["mix_norm"][l] = mg["mix_norm"]
        dh, dgate, dup, dn1, do = _ffn_bwd_x(f"l{l}_ffn1_bx", dh, h0, f1[0], gate1, up1, f1[1], f1[2], f1[3], l, tmx)
        bufs1 = _ffn_bwd_w(f"l{l}_ffn1_bw", hn1, do, gate1, up1, dgate, dup, bufs1, l, L, tm)
        G["ffn1_norm"][l] = dn1[0]

    st = lambda xs: jnp.stack(xs, axis=0)
    big = dict(ffn1_w_gate=bufs1[0], ffn1_w_up=bufs1[1], ffn1_w_down=bufs1[2],
               ffn2_w_gate=bufs2[0], ffn2_w_up=bufs2[1], ffn2_w_down=bufs2[2],
               ab_w_in=st([m["w_in"] for m in GA]), ab_w_out=st([m["w_out"] for m in GA]),
               cd_w_in=st([m["w_in"] for m in GC]), cd_w_out=st([m["w_out"] for m in GC]))
    small = dict(ffn1_norm=st(G["ffn1_norm"]), mix_norm=st(G["mix_norm"]), ffn2_norm=st(G["ffn2_norm"]),
                 pool_w=st([m["pool_w"] for m in GA]), pool_scale=st([m["pool_scale"] for m in GA]),
                 dn_conv_w=st([m["conv_w"] for m in GA]), dn_a_log=st([m["a_log"] for m in GA]),
                 dn_dt_bias=st([m["dt_bias"] for m in GA]), dn_out_norm=st([m["out_norm"] for m in GA]),
                 sgu_norm_g=st([m["sgu_g"] for m in GC]), sgu_norm_b=st([m["sgu_b"] for m in GC]),
                 sgu_w=st([m["sgu_w"] for m in GC]), sgu_bias=st([m["sgu_bias"] for m in GC]),
                 sc_conv_w=st([m["sc_w"] for m in GC]), final_norm=d_final)
    return loss, dh, big, small


ANY = pl.BlockSpec(memory_space=pl.ANY)


def _place():
    return lax.axis_index("x"), lax.axis_index("y"), lax.axis_index("c")


def _exchange(name, srcs, out_shapes, plan, n_copies, bases=None):
    n, m = len(srcs), len(out_shapes)
    nb = m if bases is not None else 0

    def body(*refs):
        ins, outs = refs[:n], refs[n + nb:n + nb + m]
        send, recv = refs[n + nb + m:]
        remote = plan(_place(), ins, outs)
        assert len(remote) == n_copies
        sends = []
        for k, (s, d, peer, _) in enumerate(remote):
            cp = pltpu.make_async_remote_copy(src_ref=s, dst_ref=d, send_sem=send.at[k], recv_sem=recv.at[k],
                                              device_id=peer, device_id_type=MESH)
            cp.start()
            sends.append(cp)
        for k, (s, _, peer, land) in enumerate(remote):
            pltpu.make_async_remote_copy(src_ref=s, dst_ref=land, send_sem=send.at[k], recv_sem=recv.at[k],
                                         device_id=peer, device_id_type=MESH).wait_recv()
        for cp in sends:
            cp.wait_send()

    return _pcall(body, in_specs=[ANY] * (n + nb), out_specs=[ANY] * m, out_shape=out_shapes,
                  input_output_aliases={n + k: k for k in range(nb)},
                  scratch_shapes=[pltpu.SemaphoreType.DMA((n_copies,)), pltpu.SemaphoreType.DMA((n_copies,))],
                  name=name)(*srcs, *(bases or ()))


def _other_chips(x, y):
    return [(1 - x, y), (x, 1 - y), (1 - x, 1 - y)]


def _gather_chips(name, xs):
    n = len(xs)
    split = [a.ndim >= 3 and a.shape[0] % 2 == 0 for a in xs]
    n_fwd = sum(split)
    fwd_of = {t: j for j, t in enumerate(t for t in range(n) if split[t])}

    def body(*refs):
        ins, outs = refs[:n], refs[n:2 * n]
        send0, recv0, send1, recv1, send2, recv2 = refs[2 * n:]
        x, y, c = _place()
        me = 2 * x + y
        chips = _other_chips(x, y)

        def mine(t):
            return pltpu.make_async_remote_copy(src_ref=ins[t], dst_ref=outs[t].at[me], send_sem=send0.at[t],
                                                recv_sem=recv0.at[t], device_id=(x, y, 1 - c), device_id_type=MESH)

        def part(t, cc):
            half = xs[t].shape[0] // 2
            return pl.ds(cc * half, half) if split[t] else pl.ds(0, xs[t].shape[0])

        def first(r, t, started):
            px, py = chips[r]
            dst = outs[t].at[me, part(t, c)] if started else outs[t].at[2 * px + py, part(t, c)]
            return pltpu.make_async_remote_copy(src_ref=ins[t].at[part(t, c)], dst_ref=dst, send_sem=send1.at[r, t],
                                                recv_sem=recv1.at[r, t], device_id=(px, py, c), device_id_type=MESH)

        def second(r, t, started):
            px, py = chips[r]
            rows = part(t, c) if started else part(t, 1 - c)
            blk = outs[t].at[2 * px + py, rows]
            return pltpu.make_async_remote_copy(src_ref=blk, dst_ref=blk, send_sem=send2.at[r, fwd_of[t]],
                                                recv_sem=recv2.at[r, fwd_of[t]], device_id=(x, y, 1 - c),
                                                device_id_type=MESH)

        sends = [first(r, t, True) for r in range(3) for t in range(n)] + [mine(t) for t in range(n)]
        for cp in sends:
            cp.start()
        for r in range(3):
            for t in range(n):
                first(r, t, False).wait_recv()
                if split[t]:
                    cp = second(r, t, True)
                    cp.start()
                    sends.append(cp)
        for r in range(3):
            for t in range(n):
                if split[t]:
                    second(r, t, False).wait_recv()
        for t in range(n):
            mine(t).wait_recv()
        for cp in sends:
            cp.wait_send()

    return _pcall(body, in_specs=[ANY] * n, out_specs=[ANY] * n,
                  out_shape=[_sds((N_CHIPS,) + a.shape, a.dtype) for a in xs],
                  scratch_shapes=[pltpu.SemaphoreType.DMA((n,)), pltpu.SemaphoreType.DMA((n,)),
                                  pltpu.SemaphoreType.DMA((3, n)), pltpu.SemaphoreType.DMA((3, n)),
                                  pltpu.SemaphoreType.DMA((3, max(n_fwd, 1))), pltpu.SemaphoreType.DMA((3, max(n_fwd, 1)))],
                  name=name)(*xs)


def _pair_split(name, gs):
    n = len(gs)

    def plan(place, ins, outs):
        x, y, c = place
        return [(ins[t].at[:, :, 1 - c], outs[t], (x, y, 1 - c), outs[t]) for t in range(n)]

    return _exchange(name, gs, [_sds(a.shape[:2] + a.shape[3:], a.dtype) for a in gs], plan, n)


def _chip_scatter(name, ps, bases, v, vbase):
    n = len(ps)
    flips = [(fx, fy, fc) for fx in (0, 1) for fy in (0, 1) for fc in (0, 1)][1:]

    def plan(place, ins, outs):
        x, y, c = place
        me = 2 * x + y
        remote = [(ins[t].at[:, 2 * px + py], outs[t].at[:, me], (px, py, c), outs[t].at[:, 2 * px + py])
                  for (px, py) in _other_chips(x, y) for t in range(n)]
        for fx, fy, fc in flips:
            px, py, pc = (1 - x if fx else x), (1 - y if fy else y), (1 - c if fc else c)
            remote.append((ins[n], outs[n].at[2 * me + c], (px, py, pc), outs[n].at[4 * px + 2 * py + pc]))
        return remote

    shapes = [_sds(a.shape, a.dtype) for a in ps] + [_sds((8,) + v.shape, v.dtype)]
    res = _exchange(name, list(ps) + [v], shapes, plan, 3 * n + len(flips), list(bases) + [vbase])
    return res[:n], res[n]


def _pair_share(name, ts):
    n = len(ts)

    def plan(place, ins, outs):
        x, y, c = place
        return [(ins[t], outs[t], (x, y, 1 - c), outs[t]) for t in range(n)]

    return _exchange(name, ts, [_sds(a.shape, a.dtype) for a in ts], plan, n)


def _gather_devices(name, v, base):
    flips = [(fx, fy, fc) for fx in (0, 1) for fy in (0, 1) for fc in (0, 1)][1:]

    def plan(place, ins, outs):
        x, y, c = place
        me = 4 * x + 2 * y + c
        remote = []
        for fx, fy, fc in flips:
            px, py, pc = (1 - x if fx else x), (1 - y if fy else y), (1 - c if fc else c)
            remote.append((ins[0], outs[0].at[me], (px, py, pc), outs[0].at[4 * px + 2 * py + pc]))
        return remote

    return _exchange(name, [v], [_sds((8,) + v.shape, v.dtype)], plan, len(flips), [base])[0]


def _sum_slots(name, a):
    k, rows, cols = a.shape
    tr = _tile(rows, ROW_TILE, 8)
    ins = [(a, pl.BlockSpec((None, tr, cols), lambda i, j=j: (j, i, 0))) for j in range(k)]
    return _fwd_call(name, lambda pids, *v: (functools.reduce(lambda p, q: p + q, v),), (rows // tr,), ins,
                     [(_sds((rows, cols)), _rows(tr, cols), False)])[0]


def _half_tile(ah):
    return _tile(ah, 512, 8)


def _pair_sum(name, g5, theirs, cf, out_dtype):
    L, P4, _, Ah, Bt = g5.shape
    ta = _half_tile(Ah)
    half = lambda hh: pl.BlockSpec((None, None, None, ta, Bt), lambda l, p, i: (l, p, hh, i, 0))
    blk = pl.BlockSpec((None, None, ta, Bt), lambda l, p, i: (l, p, i, 0))

    def fn(pids, g0, g1, r, c):
        tot = jnp.where(jnp.max(c) > 0.5, g1, g0) + r
        return tot, tot

    return _fwd_call(name, fn, (L, P4, Ah // ta), [(g5, half(0)), (g5, half(1)), (theirs, blk),
                                                   (cf, pl.BlockSpec((1, LANES), lambda l, p, i: (0, 0)))],
                     [(_sds((L, P4, Ah, Bt), out_dtype), blk, False)] * 2)


def _sum_chips(name, parts):
    L, P4, Ah, Bt = parts.shape
    ta = _half_tile(Ah)
    ins = [(parts, pl.BlockSpec((None, None, ta, Bt), lambda l, i, q=q: (l, q, i, 0))) for q in range(P4)]
    return _fwd_call(name, lambda pids, *v: (functools.reduce(lambda p, q: p + q, v),), (L, Ah // ta), ins,
                     [(_sds((L, Ah, Bt)), pl.BlockSpec((None, ta, Bt), lambda l, i: (l, i, 0)), False)])[0]


def _adam_terms(w, g, m, v):
    m2 = ADAM_B1 * m + (1.0 - ADAM_B1) * g
    v2 = ADAM_B2 * v + (1.0 - ADAM_B2) * (g * g)
    m_hat = m2 / (1.0 - ADAM_B1 ** ADAM_STEP)
    v_hat = v2 / (1.0 - ADAM_B2 ** ADAM_STEP)
    return -ADAM_LR * (m_hat / (jnp.sqrt(v_hat) + ADAM_EPS) + ADAM_WD * w), m2, v2


def _adam_halves(name, w, m, v, mine, theirs, cf):
    L, Aa, Bt = w.shape
    Ah = Aa // 2
    ta = _half_tile(Ah)
    full = pl.BlockSpec((None, None, ta, Bt), lambda l, hh, i: (l, hh, i, 0))
    part = pl.BlockSpec((None, ta, Bt), lambda l, hh, i: (l, i, 0))

    def fn(pids, w_, m_, v_, a, b, c):
        g = jnp.where(jnp.max(c) == pids[1].astype(F32), a, b)
        return (g,) + _adam_terms(w_, g, m_, v_)

    ins = [(a.reshape(L, 2, Ah, Bt), full) for a in (w, m, v)] + [(mine, part), (theirs, part),
                                                                 (cf, pl.BlockSpec((1, LANES), lambda l, hh, i: (0, 0)))]
    outs = _fwd_call(name, fn, (L, 2, Ah // ta), ins, [(_sds((L, 2, Ah, Bt)), full, False)] * 4)
    return [o.reshape(w.shape) for o in outs]


def _adam_rows(name, w, g, m, v):
    rows, cols = w.shape
    tr = _tile(rows, ROW_TILE, 8)
    ins = [(a, _rows(tr, cols)) for a in (w, g, m, v)]
    return _fwd_call(name, lambda pids, *a: _adam_terms(*a), (rows // tr,), ins,
                     [(_sds((rows, cols)), _rows(tr, cols), False)] * 3)


def _pack(xs):
    rows = []
    for a in xs:
        flat = a.reshape(-1).astype(F32)
        rows.append(jnp.pad(flat, (0, (-flat.size) % LANES)).reshape(-1, LANES))
    buf = jnp.concatenate(rows, axis=0)
    return jnp.pad(buf, ((0, (-buf.shape[0]) % 8), (0, 0)))


def _unpack(buf, shapes):
    out, row = [], 0
    for s in shapes:
        n = math.prod(s)
        nr = -(-n // LANES)
        out.append(buf[row:row + nr].reshape(-1)[:n].reshape(s))
        row += nr
    return out


_WEIGHTS = ("ffn1_norm", "ffn1_w_gate", "ffn1_w_up", "ffn1_w_down", "mix_norm", "ffn2_norm", "ffn2_w_gate", "ffn2_w_up",
            "ffn2_w_down", "ab_w_in", "pool_w", "pool_scale", "dn_conv_w", "dn_a_log", "dn_dt_bias", "dn_out_norm",
            "ab_w_out", "cd_w_in", "sgu_norm_g", "sgu_norm_b", "sgu_w", "sgu_bias", "sc_conv_w", "cd_w_out", "final_norm")
_BIG = ("ffn1_w_gate", "ffn1_w_up", "ffn1_w_down", "ffn2_w_gate", "ffn2_w_up", "ffn2_w_down", "ab_w_in", "ab_w_out",
        "cd_w_in", "cd_w_out")
_ROW_SHARDED = ("ffn1_w_down", "ffn2_w_down", "ab_w_out", "cd_w_out")
_SMALL_SHARDED = ("dn_conv_w", "sgu_norm_g", "sgu_norm_b", "sc_conv_w")
_SMALL = tuple(n for n in _WEIGHTS if n not in _BIG)


def _to_slots(name, g):
    L, A, Bt = g.shape
    if name in _ROW_SHARDED:
        return g.reshape(L, N_CHIPS, A // N_CHIPS, Bt)
    return g.reshape(L, A, N_CHIPS, Bt // N_CHIPS).transpose(0, 2, 1, 3)


def kernel(x, ffn1_norm, ffn1_w_gate, ffn1_w_up, ffn1_w_down, mix_norm, ffn2_norm, ffn2_w_gate, ffn2_w_up, ffn2_w_down,
           ab_w_in, pool_w, pool_scale, dn_conv_w, dn_a_log, dn_dt_bias, dn_out_norm, ab_w_out, cd_w_in, sgu_norm_g,
           sgu_norm_b, sgu_w, sgu_bias, sc_conv_w, cd_w_out, final_norm, loss_target,
           m_ffn1_norm, m_ffn1_w_gate, m_ffn1_w_up, m_ffn1_w_down, m_mix_norm, m_ffn2_norm, m_ffn2_w_gate, m_ffn2_w_up,
           m_ffn2_w_down, m_ab_w_in, m_pool_w, m_pool_scale, m_dn_conv_w, m_dn_a_log, m_dn_dt_bias, m_dn_out_norm,
           m_ab_w_out, m_cd_w_in, m_sgu_norm_g, m_sgu_norm_b, m_sgu_w, m_sgu_bias, m_sc_conv_w, m_cd_w_out, m_final_norm,
           v_ffn1_norm, v_ffn1_w_gate, v_ffn1_w_up, v_ffn1_w_down, v_mix_norm, v_ffn2_norm, v_ffn2_w_gate, v_ffn2_w_up,
           v_ffn2_w_down, v_ab_w_in, v_pool_w, v_pool_scale, v_dn_conv_w, v_dn_a_log, v_dn_dt_bias, v_dn_out_norm,
           v_ab_w_out, v_cd_w_in, v_sgu_norm_g, v_sgu_norm_b, v_sgu_w, v_sgu_bias, v_sc_conv_w, v_cd_w_out, v_final_norm):
    A = dict(locals())
    B, S, D = x.shape
    T = B * S
    xi, yi, ci = _place()
    chip = 2 * xi + yi
    cf = jnp.broadcast_to(ci.astype(F32), (1, LANES))
    own = lambda n, mine, slot: lax.dynamic_update_index_in_dim(lax.empty((n,) + mine.shape, mine.dtype), mine, slot, 0)

    sh_shapes = [A[n].shape for n in _SMALL_SHARDED]
    local = [A[n].astype(BF16) for n in _BIG] + [_pack([A[n] for n in _SMALL_SHARDED])]
    gathered = _gather_chips("gather_weights", local)
    gw = dict(zip(_BIG, gathered[:-1]))
    per_chip = [_unpack(gathered[-1][p], sh_shapes) for p in range(N_CHIPS)]
    W = {n: A[n] for n in _SMALL if n not in _SMALL_SHARDED}
    for j, n in enumerate(_SMALL_SHARDED):
        W[n] = jnp.concatenate([per_chip[p][j] for p in range(N_CHIPS)], axis=-1)
    for f in ("ffn1", "ffn2"):
        W[f + "_g"], W[f + "_u"], W[f + "_d"] = gw[f + "_w_gate"], gw[f + "_w_up"], gw[f + "_w_down"]
    cols = lambda g: jnp.concatenate([g[p] for p in range(N_CHIPS)], axis=-1)
    rows = lambda g: jnp.concatenate([g[p] for p in range(N_CHIPS)], axis=1)
    ab_in = cols(gw["ab_w_in"])
    ab_cols = ab_in.shape[-1]
    ab_pad = (-ab_cols) % LANES
    W["ab_in"] = jnp.pad(ab_in, ((0, 0), (0, 0), (0, ab_pad)))
    W["cd_in"], W["ab_out"], W["cd_out"] = cols(gw["cd_w_in"]), rows(gw["ab_w_out"]), rows(gw["cd_w_out"])

    loss, dx, big, small = _local_step(x.reshape(T, D), loss_target.reshape(T, D), W, B, S)
    loss = lax.psum(loss, ("x", "y", "c"))

    big["ab_w_in"] = big["ab_w_in"][:, :, :ab_cols]
    slots = []
    for n in _BIG:
        g = big[n] if big[n].ndim == 4 else _to_slots(n, big[n])
        L, P4, Aa, Bt = g.shape
        slots.append(g.reshape(L, P4, 2, Aa // 2, Bt))
    theirs = _pair_split("grads_pair_split", slots)
    pair = [_pair_sum(f"grads_pair_sum_{n}", g5, r, cf, BF16) for n, g5, r in zip(_BIG, slots, theirs)]
    sm_shapes = [small[n].shape for n in _SMALL]
    sm_local = _pack([small[n] for n in _SMALL])
    parts, every = _chip_scatter("grads_chip_scatter", [p for p, _ in pair], [p for _, p in pair],
                                 sm_local, own(8, sm_local, 2 * chip + ci))
    tot = [_sum_chips(f"grads_chip_sum_{n}", p) for n, p in zip(_BIG, parts)]
    other = _pair_share("grads_pair_share", tot)

    grads = {}
    for n, g in zip(_SMALL, _unpack(_sum_slots("small_grads_sum", every), sm_shapes)):
        if n in _SMALL_SHARDED:
            w_loc = A[n].shape[-1]
            g = lax.dynamic_slice_in_dim(g, chip * w_loc, w_loc, axis=g.ndim - 1)
        grads[n] = g.reshape(A[n].shape)

    delta, new_m, new_v = {}, {}, {}
    for n, a, b in zip(_BIG, tot, other):
        grads[n], delta[n], new_m[n], new_v[n] = _adam_halves(f"adam_{n}", A[n], A["m_" + n], A["v_" + n], a, b, cf)
    packed = [_pack([d[n] for n in _SMALL]) for d in (A, grads, {n: A["m_" + n] for n in _SMALL},
                                                       {n: A["v_" + n] for n in _SMALL})]
    loc_shapes = [A[n].shape for n in _SMALL]
    for d, buf in zip((delta, new_m, new_v), _adam_rows("adam_small", *packed)):
        d.update(zip(_SMALL, _unpack(buf, loc_shapes)))

    return (loss, dx.reshape(B, S, D), *[grads[n] for n in _WEIGHTS], *[delta[n] for n in _WEIGHTS],
            *[new_m[n] for n in _WEIGHTS], *[new_v[n] for n in _WEIGHTS])
```

```python
import functools
import math

import jax
import jax.numpy as jnp
from jax import lax
from jax.experimental import pallas as pl
from jax.experimental.pallas import tpu as pltpu

F32, BF16 = jnp.float32, jnp.bfloat16
HIGHEST = lax.Precision.HIGHEST
MESH = pl.DeviceIdType.MESH

EPS = 1e-6
LANES = 128
CH = 64
PAIR = 2 * CH
POOL_WINDOWS = (2, 4, 8, 16)
N_CHIPS = 4
ADAM_LR, ADAM_B1, ADAM_B2, ADAM_EPS, ADAM_WD, ADAM_STEP = 0.001, 0.9, 0.999, 1e-08, 0.01, 10
VMEM_LIMIT = 63 * 1024 * 1024


def _pcall(body, **kw):
    return pl.pallas_call(body, **kw)


def _params(nd):
    return pltpu.CompilerParams(dimension_semantics=("arbitrary",) * nd, vmem_limit_bytes=VMEM_LIMIT)


def _sds(shape, dtype=F32):
    return jax.ShapeDtypeStruct(tuple(shape), dtype)


def _tile(n, cap, mult=LANES):
    if n <= cap:
        return n
    best = None
    for t in range(mult, cap + 1, mult):
        if n % t == 0:
            best = t
    assert best is not None, (n, cap)
    return best


def _dg(a, b, ca, cb):
    return lax.dot_general(a.astype(BF16), b.astype(BF16), (((ca,), (cb,)), ((), ())),
                           preferred_element_type=F32)


@jax.custom_vjp
def _bdot(a, b):
    return _dg(a, b, 1, 0)


def _bdot_fwd(a, b):
    return _dg(a, b, 1, 0), (a, b)


def _bdot_bwd(res, g):
    a, b = res
    return _dg(g, b, 1, 1), _dg(a, g, 0, 0)


_bdot.defvjp(_bdot_fwd, _bdot_bwd)


@jax.custom_vjp
def _bdot_nt(a, b):
    return _dg(a, b, 1, 1)


def _bdot_nt_fwd(a, b):
    return _dg(a, b, 1, 1), (a, b)


def _bdot_nt_bwd(res, g):
    a, b = res
    return _dg(g, b, 1, 0), _dg(g, a, 0, 0)


_bdot_nt.defvjp(_bdot_nt_fwd, _bdot_nt_bwd)


def _hdot(a, b):
    return jnp.dot(a, b, precision=HIGHEST, preferred_element_type=F32)


def _shift_raw(x, k):
    n = x.shape[0]
    rows = lax.broadcasted_iota(jnp.int32, x.shape, 0)
    r = pltpu.roll(x, k % n, 0)
    if k > 0:
        return jnp.where(rows >= k, r, 0.0)
    return jnp.where(rows < n + k, r, 0.0)


@functools.partial(jax.custom_vjp, nondiff_argnums=(1,))
def _shift(x, k):
    return _shift_raw(x, k)


def _shift_fwd(x, k):
    return _shift_raw(x, k), None


def _shift_bwd(k, _, g):
    return (_shift(g, -k),)


_shift.defvjp(_shift_fwd, _shift_bwd)


def _col(x, j):
    lanes = lax.broadcasted_iota(jnp.int32, x.shape, x.ndim - 1)
    return jnp.sum(jnp.where(lanes == j, x, 0.0), axis=-1, keepdims=True)


def _rms(x, g):
    return x * lax.rsqrt(jnp.mean(x * x, axis=-1, keepdims=True) + EPS) * g


def _sigmoid(x):
    return 1.0 / (1.0 + jnp.exp(-x))


def _silu(x):
    return x * _sigmoid(x)


def _sigmoid_fast(x):
    return pl.reciprocal(1.0 + jnp.exp(-x), approx=True)


def _softplus(x):
    return jnp.maximum(x, 0.0) + jnp.log(1.0 + jnp.exp(-jnp.abs(x)))


def _gelu(x):
    c = math.sqrt(2.0 / math.pi)
    return 0.5 * x * (1.0 + jnp.tanh(c * (x + 0.044715 * (x * x * x))))


def _first(axes):
    c = None
    for a in axes:
        t = pl.program_id(a) == 0
        c = t if c is None else jnp.logical_and(c, t)
    return c


def _fwd_call(name, fn, grid, ins, outs, acc_axes=()):
    nin = len(ins)

    def body(*refs):
        pids = tuple(pl.program_id(a) for a in range(len(grid)))
        first = _first(acc_axes) if acc_axes else None
        vals = [r[...].astype(F32) for r in refs[:nin]]
        res = fn(pids, *vals)
        for r, o, (_, _, acc) in zip(refs[nin:], res, outs):
            if acc:
                @pl.when(first)
                def _(r=r):
                    r[...] = jnp.zeros_like(r)
                r[...] += o.astype(r.dtype)
            else:
                r[...] = o.astype(r.dtype)

    return _pcall(body, grid=grid, in_specs=[s for _, s in ins], out_specs=[s for _, s, _ in outs],
                  out_shape=[s for s, _, _ in outs], name=name,
                  compiler_params=_params(len(grid)))(*[a for a, _ in ins])


def _bwd_call(name, fn, grid, ins, cts, gouts, acc_axes=(), addends=()):
    nin, nct, nadd = len(ins), len(cts), len(addends)

    def body(*refs):
        pids = tuple(pl.program_id(a) for a in range(len(grid)))
        first = _first(acc_axes) if acc_axes else None
        vals = [r[...].astype(F32) for r in refs[:nin]]
        ctv = tuple(r[...].astype(F32) for r in refs[nin:nin + nct])
        addv = {pos: refs[nin + nct + n][...].astype(F32) for n, (pos, _, _) in enumerate(addends)}
        _, vjp = jax.vjp(lambda *v: tuple(fn(pids, *v)), *vals)
        grads = vjp(ctv)
        for pos, (r, (idx, _, _, acc)) in enumerate(zip(refs[nin + nct + nadd:], gouts)):
            gval = grads[idx]
            if pos in addv:
                gval = gval + addv[pos]
            if acc:
                @pl.when(first)
                def _(r=r):
                    r[...] = jnp.zeros_like(r)
                r[...] += gval.astype(r.dtype)
            else:
                r[...] = gval.astype(r.dtype)

    args = [a for a, _ in ins] + [a for a, _ in cts] + [a for _, a, _ in addends]
    specs = [s for _, s in ins] + [s for _, s in cts] + [s for _, _, s in addends]
    return _pcall(body, grid=grid, in_specs=specs, out_specs=[s for _, _, s, _ in gouts],
                  out_shape=[s for _, s, _, _ in gouts], name=name,
                  compiler_params=_params(len(grid)))(*args)


def _mm(name, a, b, *, ta=False, tb=False, res=None, tm=512, tn=None, tk=None):
    if ta:
        K, M = a.shape
    else:
        M, K = a.shape
    N = b.shape[0] if tb else b.shape[1]
    tm, tn, tk = _tile(M, tm, 8), N if tn is None else tn, K if tk is None else min(tk, K)
    nk = K // tk
    assert res is None or nk == 1
    a_spec = pl.BlockSpec((tk, tm), lambda i, j, k: (k, i)) if ta else pl.BlockSpec((tm, tk), lambda i, j, k: (i, k))
    b_spec = pl.BlockSpec((tn, tk), lambda i, j, k: (j, k)) if tb else pl.BlockSpec((tk, tn), lambda i, j, k: (k, j))
    o_spec = pl.BlockSpec((tm, tn), lambda i, j, k: (i, j))
    ca, cb = (0 if ta else 1), (1 if tb else 0)

    def body(*refs):
        a_ref, b_ref, o_ref = refs[0], refs[1], refs[-1]
        part = _dg(a_ref[...], b_ref[...], ca, cb)
        if nk == 1:
            o_ref[...] = part if res is None else refs[2][...] + part
        else:
            @pl.when(pl.program_id(2) == 0)
            def _():
                o_ref[...] = jnp.zeros_like(o_ref)

            o_ref[...] += part

    args, specs = [a, b], [a_spec, b_spec]
    if res is not None:
        args.append(res)
        specs.append(o_spec)
    return _pcall(body, grid=(M // tm, N // tn, nk), in_specs=specs, out_specs=o_spec, out_shape=_sds((M, N)),
                  name=name, compiler_params=_params(3))(*args)


def _mm_norm(name, h, g, w, tm=256):
    M, K = h.shape
    N = w.shape[1]
    tm = _tile(M, tm, 8)

    def body(h_ref, g_ref, w_ref, o_ref, hn_ref):
        hn = _rms(h_ref[...], g_ref[...]).astype(BF16)
        hn_ref[...] = hn
        o_ref[...] = _dg(hn, w_ref[...], 1, 0)

    row = pl.BlockSpec((tm, K), lambda i: (i, 0))
    return _pcall(body, grid=(M // tm,), in_specs=[row, pl.BlockSpec((1, K), lambda i: (0, 0)),
                                                   pl.BlockSpec(w.shape, lambda i: (0, 0))],
                  out_specs=[pl.BlockSpec((tm, N), lambda i: (i, 0)), row],
                  out_shape=[_sds((M, N)), _sds((M, K), BF16)], name=name, compiler_params=_params(1))(h, g, w)


def _offsets(parts):
    widths = [p.shape[1] for p in parts]
    return widths, [sum(widths[:k]) for k in range(len(widths))]


def _mm_parts_dx(name, parts, w, h, g, dh_res, tm=256):
    M, N = parts[0].shape[0], w.shape[0]
    widths, offs = _offsets(parts)
    tm = _tile(M, tm, 8)
    n = len(parts)

    def body(*refs):
        w_ref, h_ref, g_ref, r_ref, o_ref, dg_ref = refs[n:]

        @pl.when(pl.program_id(0) == 0)
        def _():
            dg_ref[...] = jnp.zeros_like(dg_ref)

        terms = [_dg(p_ref[...], w_ref[:, off:off + wd], 1, 1) for p_ref, off, wd in zip(refs[:n], offs, widths)]
        _, vjp = jax.vjp(_rms, h_ref[...], g_ref[...])
        dx, dgain = vjp(functools.reduce(lambda p, q: p + q, terms))
        o_ref[...] = r_ref[...] + dx
        dg_ref[...] += dgain

    row = pl.BlockSpec((tm, N), lambda i: (i, 0))
    gain = pl.BlockSpec((1, N), lambda i: (0, 0))
    specs = ([pl.BlockSpec((tm, wd), lambda i: (i, 0)) for wd in widths]
             + [pl.BlockSpec(w.shape, lambda i: (0, 0)), row, gain, row])
    return _pcall(body, grid=(M // tm,), in_specs=specs, out_specs=[row, gain],
                  out_shape=[_sds((M, N)), _sds((1, N))], name=name,
                  compiler_params=_params(1))(*parts, w, h, g, dh_res)


def _mm_parts_dw(name, a, parts, tk=512):
    K, M = a.shape
    widths, offs = _offsets(parts)
    tk = min(tk, K)

    def body(*refs):
        a_ref, o_ref = refs[0], refs[-1]

        @pl.when(pl.program_id(0) == 0)
        def _():
            o_ref[...] = jnp.zeros_like(o_ref)

        for p_ref, off, wd in zip(refs[1:-1], offs, widths):
            o_ref[:, off:off + wd] += _dg(a_ref[...], p_ref[...], 0, 0)

    specs = [pl.BlockSpec((tk, M), lambda k: (k, 0))] + [pl.BlockSpec((tk, wd), lambda k: (k, 0)) for wd in widths]
    return _pcall(body, grid=(K // tk,), in_specs=specs, out_specs=pl.BlockSpec((M, sum(widths)), lambda k: (0, 0)),
                  out_shape=_sds((M, sum(widths))), name=name, compiler_params=_params(1))(a, *parts)


FFN_PARTS = 2


def _ffn_fwd(name, h, g, wg, wu, wd, l, tm):
    T, D = h.shape
    ns, _, _, Fs = wg.shape

    def body(h_ref, g_ref, wg_ref, wu_ref, wd_ref, ho_ref, gate_ref, up_ref, hn_s, acc_s):
        s = pl.program_id(1)

        @pl.when(s == 0)
        def _():
            hn_s[...] = _rms(h_ref[...], g_ref[...]).astype(BF16)
            acc_s[...] = jnp.zeros_like(acc_s)

        parts = [pl.ds(k * (tm // FFN_PARTS), tm // FFN_PARTS) for k in range(FFN_PARTS)]
        gu = [(_dg(hn_s[r, :], wg_ref[...], 1, 0), _dg(hn_s[r, :], wu_ref[...], 1, 0)) for r in parts]
        for r, (gate, up) in zip(parts, gu):
            gate_ref[r, :] = gate.astype(BF16)
            up_ref[r, :] = up.astype(BF16)
            act = (gate * _sigmoid_fast(gate) * up).astype(BF16)
            acc_s[r, :] += _dg(act, wd_ref[...], 1, 0)

        @pl.when(s == ns - 1)
        def _():
            ho_ref[...] = h_ref[...] + 0.5 * acc_s[...]

    row = pl.BlockSpec((tm, D), lambda i, s: (i, 0))
    wcol = pl.BlockSpec((None, None, D, Fs), lambda i, s: (s, l, 0, 0))
    wrow = pl.BlockSpec((None, None, Fs, D), lambda i, s: (s, l, 0, 0))
    slot = pl.BlockSpec((None, tm, Fs), lambda i, s: (s, i, 0))
    return _pcall(body, grid=(T // tm, ns), in_specs=[row, pl.BlockSpec((1, D), lambda i, s: (0, 0)), wcol, wcol, wrow],
                  out_specs=[row, slot, slot, row],
                  out_shape=[_sds((T, D)), _sds((ns, T, Fs), BF16), _sds((ns, T, Fs), BF16), _sds((T, D), BF16)],
                  scratch_shapes=[pltpu.VMEM((tm, D), F32)], name=name,
                  compiler_params=_params(2))(h, g, wg, wu, wd)


def _ffn_bwd_x(name, dh, h, g, gate, up, wg, wu, wd, l, tm):
    T, D = h.shape
    ns, _, _, Fs = wg.shape

    def body(dh_ref, h_ref, g_ref, gate_ref, up_ref, wg_ref, wu_ref, wd_ref,
             dho_ref, dgate_ref, dup_ref, dgain_ref, do_s, acc_s):
        i, s = pl.program_id(0), pl.program_id(1)

        @pl.when(s == 0)
        def _():
            do_s[...] = (0.5 * dh_ref[...]).astype(BF16)
            acc_s[...] = jnp.zeros_like(acc_s)

        @pl.when(jnp.logical_and(i == 0, s == 0))
        def _():
            dgain_ref[...] = jnp.zeros_like(dgain_ref)

        parts = [pl.ds(k * (tm // FFN_PARTS), tm // FFN_PARTS) for k in range(FFN_PARTS)]
        dact = [_dg(do_s[r, :], wd_ref[...], 1, 1) for r in parts]
        for r, da in zip(parts, dact):
            gt, u = gate_ref[r, :].astype(F32), up_ref[r, :].astype(F32)
            sg = _sigmoid_fast(gt)
            dgt = (da * u * (sg * (1.0 + gt * (1.0 - sg)))).astype(BF16)
            du = (da * (gt * sg)).astype(BF16)
            dgate_ref[r, :] = dgt
            dup_ref[r, :] = du
            acc_s[r, :] += _dg(dgt, wg_ref[...], 1, 1) + _dg(du, wu_ref[...], 1, 1)

        @pl.when(s == ns - 1)
        def _():
            _, vjp = jax.vjp(_rms, h_ref[...], g_ref[...])
            dx, dgain = vjp(acc_s[...])
            dho_ref[...] = dh_ref[...] + dx
            dgain_ref[...] += dgain

    row = pl.BlockSpec((tm, D), lambda i, s: (i, 0))
    gain = pl.BlockSpec((1, D), lambda i, s: (0, 0))
    wcol = pl.BlockSpec((None, None, D, Fs), lambda i, s: (s, l, 0, 0))
    wrow = pl.BlockSpec((None, None, Fs, D), lambda i, s: (s, l, 0, 0))
    slot = pl.BlockSpec((None, tm, Fs), lambda i, s: (s, i, 0))
    return _pcall(body, grid=(T // tm, ns), in_specs=[row, row, gain, slot, slot, wcol, wcol, wrow],
                  out_specs=[row, slot, slot, gain, row],
                  out_shape=[_sds((T, D)), _sds((ns, T, Fs), BF16), _sds((ns, T, Fs), BF16), _sds((1, D)),
                             _sds((T, D), BF16)],
                  scratch_shapes=[pltpu.VMEM((tm, D), F32)], name=name,
                  compiler_params=_params(2))(dh, h, g, gate, up, wg, wu, wd)


def _ffn_bwd_w(name, hn, do, gate, up, dgate, dup, bufs, l, nl, tm):
    T, D = hn.shape
    ns, _, Fs = gate.shape

    def body(hn_ref, do_ref, gate_ref, up_ref, dgate_ref, dup_ref, dwg_ref, dwu_ref, dwd_ref):
        @pl.when(pl.program_id(1) == 0)
        def _():
            dwg_ref[...] = jnp.zeros_like(dwg_ref)
            dwu_ref[...] = jnp.zeros_like(dwu_ref)
            dwd_ref[...] = jnp.zeros_like(dwd_ref)

        def prep(r):
            gt = gate_ref[r, :].astype(F32)
            act = (gt * _sigmoid_fast(gt) * up_ref[r, :].astype(F32)).astype(BF16)
            return hn_ref[r, :], act, do_ref[r, :]

        parts = [pl.ds(k * (tm // FFN_PARTS), tm // FFN_PARTS) for k in range(FFN_PARTS)]
        ready = [prep(r) for r in parts]
        add = lambda xs: functools.reduce(lambda p, q: p + q, xs)
        dwg_ref[...] += add([_dg(hn, dgate_ref[r, :], 0, 0) for r, (hn, _, _) in zip(parts, ready)])
        dwu_ref[...] += add([_dg(hn, dup_ref[r, :], 0, 0) for r, (hn, _, _) in zip(parts, ready)])
        dwd_ref[...] += add([_dg(act, do, 0, 0) for _, act, do in ready])

    row = pl.BlockSpec((tm, D), lambda s, j: (j, 0))
    gain = pl.BlockSpec((1, D), lambda s, j: (0, 0))
    slot = pl.BlockSpec((None, tm, Fs), lambda s, j: (s, j, 0))
    wcol = pl.BlockSpec((None, None, D, Fs), lambda s, j: (l, s, 0, 0))
    wrow = pl.BlockSpec((None, None, Fs, D), lambda s, j: (l, s, 0, 0))
    args, specs, alias = [hn, do, gate, up, dgate, dup], [row, row, slot, slot, slot, slot], {}
    if bufs is not None:
        alias = {len(args) + k: k for k in range(3)}
        args, specs = args + list(bufs), specs + [pl.BlockSpec(memory_space=pl.ANY)] * 3
    return _pcall(lambda *refs: body(*refs[:6], *refs[-3:]), grid=(ns, T // tm), in_specs=specs,
                  out_specs=[wcol, wcol, wrow], input_output_aliases=alias,
                  out_shape=[_sds((nl, ns, D, Fs)), _sds((nl, ns, D, Fs)), _sds((nl, ns, Fs, D))], name=name,
                  compiler_params=_params(2))(*args)


def _rms_fn(pids, x, g):
    return (_rms(x, g),)


def _pool_fn(pids, a, w, scale):
    rows = lax.broadcasted_iota(jnp.int32, (a.shape[0], LANES), 0)
    outs = []
    for gi, win in enumerate(POOL_WINDOWS):
        ag = a[:, gi * LANES:(gi + 1) * LANES]
        s, k = ag, 1
        while k < win:
            s = s + _shift(s, k)
            k *= 2
        cnt = jnp.minimum(rows + 1, win).astype(F32)
        pooled = s / cnt - ag
        outs.append(_bdot(pooled, w[gi]) * scale[:, gi * LANES:(gi + 1) * LANES])
    return (jnp.concatenate(outs, axis=-1),)


def _conv_taps(x, cw):
    K = cw.shape[0]
    y = cw[K - 1] * x
    for j in range(K - 1):
        y = y + cw[j] * _shift(x, K - 1 - j)
    return y


def _prep_fn(pids, q, k, v, bg, cwq, cwk, cwv, alog, dtb):
    hd = pids[0]
    q, k, v = _silu(_conv_taps(q, cwq)), _silu(_conv_taps(k, cwk)), _silu(_conv_taps(v, cwv))
    q = q * lax.rsqrt(jnp.sum(q * q, axis=-1, keepdims=True) + EPS) * (LANES ** -0.5)
    k = k * lax.rsqrt(jnp.sum(k * k, axis=-1, keepdims=True) + EPS)
    beta = _sigmoid(_col(bg, hd))
    g = -jnp.exp(_col(alog, hd)) * _softplus(_col(bg, hd + 4) + _col(dtb, hd))
    return q, k, v, jnp.broadcast_to(beta, q.shape), jnp.broadcast_to(g, q.shape)


def _post_fn(pids, o, z, onorm):
    return (_rms(o, onorm) * _silu(z),)


def _sgu_fn(pids, u, v, g, b, ws, bias_t):
    u, v = _gelu(u), _gelu(v)
    mu = jnp.mean(v, axis=-1, keepdims=True)
    xc = v - mu
    vn = xc * lax.rsqrt(jnp.mean(xc * xc, axis=-1, keepdims=True) + EPS) * g + b
    r = lax.broadcasted_iota(jnp.int32, (LANES, LANES), 0)
    c = lax.broadcasted_iota(jnp.int32, (LANES, LANES), 1)
    rows = []
    for n in range(u.shape[0] // LANES):
        heads = []
        for hd in range(4):
            wm = jnp.where(r >= c, ws[hd], 0.0)
            blk = vn[n * LANES:(n + 1) * LANES, hd * LANES:(hd + 1) * LANES]
            heads.append(_bdot(wm, blk) + _col(bias_t, hd))
        rows.append(jnp.concatenate(heads, axis=-1))
    mixed = jnp.concatenate(rows, axis=0) if len(rows) > 1 else rows[0]
    return (u * mixed,)


def _sconv_fn(pids, xd, bgate, cg, cw):
    return (bgate * _conv_taps(cg * xd, cw),)


def _loss_fn(pids, h, g, tgt):
    err = _rms(h, g) - tgt
    tot = 0.5 * jnp.sum(jnp.mean(err * err, axis=-1, keepdims=True), axis=0, keepdims=True)
    return (jnp.broadcast_to(tot, (1, LANES)),)


DELTA_PAIRS = 4
DELTA_ROWS = DELTA_PAIRS * PAIR


def _each(fn, *lists):
    return [fn(*args) for args in zip(*lists)]


def _chunk_cumsum(g):
    pos = jnp.bitwise_and(lax.broadcasted_iota(jnp.int32, g.shape, 0), CH - 1)
    s, k = g, 1
    while k < CH:
        s = s + jnp.where(pos >= k, _shift(s, k), 0.0)
        k *= 2
    tot = [jnp.broadcast_to(jnp.sum(g[j * CH:(j + 1) * CH], axis=0, keepdims=True), (CH, g.shape[1])) for j in (0, 1)]
    return s, jnp.concatenate(tot, axis=0)


def _stage_a(blocks):
    q, k, v, beta, g = (list(x) for x in zip(*blocks))
    r = lax.broadcasted_iota(jnp.int32, (PAIR, PAIR), 0)
    c = lax.broadcasted_iota(jnp.int32, (PAIR, PAIR), 1)
    same = jnp.right_shift(r, 6) == jnp.right_shift(c, 6)
    tri = jnp.logical_and(same, r >= c)
    strict = jnp.logical_and(same, r > c)
    eye = (r == c).astype(F32)
    sums = _each(_chunk_cumsum, g)
    gc, gt = [a for a, _ in sums], [b for _, b in sums]
    gamma = _each(lambda x: jnp.exp(jnp.where(tri, x - x.T, -jnp.inf)), gc)
    kb = _each(lambda a, b: a * b, k, beta)
    kk = _each(_bdot_nt, kb, k)
    p = _each(lambda a, gm: -jnp.where(strict, a * gm, 0.0), kk, gamma)
    tinv = _each(lambda x: eye + x, p)
    for _ in range(5):
        p = _each(_bdot, p, p)
        tinv = _each(lambda t, x: t + x, tinv, _each(_bdot, tinv, p))
    egc = _each(jnp.exp, gc)
    u = _each(_bdot, tinv, _each(lambda a, b: a * b, v, beta))
    w = _each(_bdot, tinv, _each(lambda a, e: a * e, kb, egc))
    a = _each(lambda x, gm: x * gm, _each(_bdot_nt, q, k), gamma)
    qd = _each(lambda x, e: x * e, q, egc)
    kd = _each(lambda x, t, s: x * jnp.exp(t - s), k, gt, gc)
    return list(zip(u, w, qd, a, kd, _each(jnp.exp, gt)))


def _nn(a, b):
    return _dg(a, b, 1, 0)


def _nt(a, b):
    return _dg(a, b, 1, 1)


def _tn(a, b):
    return _dg(a, b, 0, 0)


def _scan_pair_fwd(s0, u, w, qd, a, kd, l0, l1):
    lo = lambda xs: [x[:CH] for x in xs]
    hi = lambda xs: [x[CH:] for x in xs]
    vn0 = _each(lambda x, y: x - y, lo(u), _each(_nn, lo(w), s0))
    s1 = _each(lambda s, l, x: s * l + x, s0, l0, _each(_tn, lo(kd), vn0))
    vn1 = _each(lambda x, y: x - y, hi(u), _each(_nn, hi(w), s1))
    s2 = _each(lambda s, l, x: s * l + x, s1, l1, _each(_tn, hi(kd), vn1))
    inter = _each(lambda x, y: jnp.concatenate([x, y], axis=0), _each(_nn, lo(qd), s0), _each(_nn, hi(qd), s1))
    intra = _each(_nn, a, _each(lambda x, y: jnp.concatenate([x, y], axis=0), vn0, vn1))
    return _each(lambda x, y: x + y, inter, intra), s1, s2


def _scan_chunk_bwd(s, ds_next, do, dvn_o, u, w, qd, kd, lrow):
    vn = _each(lambda x, y: x - y, u, _each(_nn, w, s))
    dvn = _each(lambda x, y: x + y, dvn_o, _each(_nn, kd, ds_next))
    dkd = _each(_nt, vn, ds_next)
    dl = _each(lambda a, b: jnp.sum(a * b, axis=0, keepdims=True), ds_next, s)
    dw = _each(lambda x: -x, _each(_nt, dvn, s))
    dqd = _each(_nt, do, s)
    ds = _each(lambda x, l, d, y: x + l * d - y, _each(_tn, qd, do), lrow, ds_next, _each(_tn, w, dvn))
    return ds, vn, dvn, dw, dqd, dkd, dl


def _delta_blocks(refs):
    return [tuple(r[p * PAIR:(p + 1) * PAIR, h * LANES:(h + 1) * LANES] for r in refs)
            for p in range(DELTA_PAIRS) for h in range(4)]


def _delta_specs(B, S, reverse):
    assert S % DELTA_ROWS == 0, (S, DELTA_ROWS)
    nstep = S // DELTA_ROWS
    at =(lambda i: nstep - 1 - i) if reverse else (lambda i: i)
    blk = pl.BlockSpec((DELTA_ROWS, 4 * LANES), lambda b, i: (b * nstep + at(i), 0))
    st = pl.BlockSpec((None, 4, 2 * DELTA_PAIRS, LANES, LANES), lambda b, i: (b, 0, at(i), 0, 0))
    scratch = [pltpu.VMEM((4, LANES, LANES), F32), pltpu.VMEM((4 * DELTA_PAIRS, PAIR, LANES), F32)]
    return nstep, blk, st, scratch


def _delta_fwd(name, qn, kn, vn, beta, g, B, S):
    T = qn.shape[0]
    nstep, blk, st, scratch = _delta_specs(B, S, False)

    def body(q_ref, k_ref, v_ref, b_ref, g_ref, o_ref, st_ref, s_s, l_s):
        @pl.when(pl.program_id(1) == 0)
        def _():
            s_s[...] = jnp.zeros_like(s_s)

        outs = _stage_a(_delta_blocks((q_ref, k_ref, v_ref, b_ref, g_ref)))
        for j, blk_out in enumerate(outs):
            l_s[j] = blk_out[5]
        state = [s_s[h] for h in range(4)]
        for p in range(DELTA_PAIRS):
            u, w, qd, a, kd, _ = (list(x) for x in zip(*outs[4 * p:4 * p + 4]))
            l0 = [l_s[4 * p + h, 0:1, :] for h in range(4)]
            l1 = [l_s[4 * p + h, CH:CH + 1, :] for h in range(4)]
            o, mid, end = _scan_pair_fwd(state, u, w, qd, a, kd, l0, l1)
            for h in range(4):
                o_ref[p * PAIR:(p + 1) * PAIR, h * LANES:(h + 1) * LANES] = o[h]
                st_ref[h, 2 * p] = state[h]
                st_ref[h, 2 * p + 1] = mid[h]
            state = end
        for h in range(4):
            s_s[h] = state[h]

    return _pcall(body, grid=(B, nstep), in_specs=[blk] * 5, out_specs=[blk, st],
                  out_shape=[_sds((T, 4 * LANES)), _sds((B, 4, S // CH, LANES, LANES))],
                  scratch_shapes=scratch, name=name, compiler_params=_params(2))(qn, kn, vn, beta, g)


def _delta_bwd(name, qn, kn, vn, beta, g, states, do, B, S):
    T = qn.shape[0]
    nstep, blk, st, scratch = _delta_specs(B, S, True)

    def body(q_ref, k_ref, v_ref, b_ref, g_ref, st_ref, do_ref, dq_ref, dk_ref, dv_ref, db_ref, dg_ref, ds_s, l_s):
        @pl.when(pl.program_id(1) == 0)
        def _():
            ds_s[...] = jnp.zeros_like(ds_s)

        rowid = lax.broadcasted_iota(jnp.int32, (PAIR, LANES), 0)
        lo = lambda xs: [x[:CH] for x in xs]
        hi = lambda xs: [x[CH:] for x in xs]
        cat = lambda xs, ys: _each(lambda x, y: jnp.concatenate([x, y], axis=0), xs, ys)
        outs, vjp = jax.vjp(_stage_a, _delta_blocks((q_ref, k_ref, v_ref, b_ref, g_ref)))
        for j, blk_out in enumerate(outs):
            l_s[j] = blk_out[5]
        ds = [ds_s[h] for h in range(4)]
        cts = [None] * (4 * DELTA_PAIRS)
        for p in reversed(range(DELTA_PAIRS)):
            u, w, qd, a, kd, _ = (list(x) for x in zip(*outs[4 * p:4 * p + 4]))
            l0 = [l_s[4 * p + h, 0:1, :] for h in range(4)]
            l1 = [l_s[4 * p + h, CH:CH + 1, :] for h in range(4)]
            s0 = [st_ref[h, 2 * p] for h in range(4)]
            s1 = [st_ref[h, 2 * p + 1] for h in range(4)]
            dout = [do_ref[p * PAIR:(p + 1) * PAIR, h * LANES:(h + 1) * LANES] for h in range(4)]
            dvn_o = _each(_tn, a, dout)
            ds1, vn1, dvn1, dw1, dqd1, dkd1, dl1 = _scan_chunk_bwd(s1, ds, hi(dout), hi(dvn_o), hi(u), hi(w), hi(qd),
                                                                   hi(kd), l1)
            ds, vn0, dvn0, dw0, dqd0, dkd0, dl0 = _scan_chunk_bwd(s0, ds1, lo(dout), lo(dvn_o), lo(u), lo(w), lo(qd),
                                                                  lo(kd), l0)
            da = _each(_nt, dout, cat(vn0, vn1))
            dl = _each(lambda x, y: jnp.where(rowid == 0, x, jnp.where(rowid == CH, y, 0.0)), dl0, dl1)
            for h, ct in enumerate(zip(cat(dvn0, dvn1), cat(dw0, dw1), cat(dqd0, dqd1), da, cat(dkd0, dkd1), dl)):
                cts[4 * p + h] = ct
        for h in range(4):
            ds_s[h] = ds[h]
        (grads,) = vjp(cts)
        for j, blk_grads in enumerate(grads):
            p, h = divmod(j, 4)
            for ref, val in zip((dq_ref, dk_ref, dv_ref, db_ref, dg_ref), blk_grads):
                ref[p * PAIR:(p + 1) * PAIR, h * LANES:(h + 1) * LANES] = val

    return _pcall(body, grid=(B, nstep), in_specs=[blk] * 5 + [st, blk], out_specs=[blk] * 5,
                  out_shape=[_sds((T, 4 * LANES))] * 5, scratch_shapes=scratch, name=name,
                  compiler_params=_params(2))(qn, kn, vn, beta, g, states, do)


ROW_TILE = 512
FFN_TILE = 1024
FFN_TILE_BX = 1024
SGU_ROWS = 256


def _rows(r, c, off=0):
    return pl.BlockSpec((r, c), lambda i: (i, off))


def _whole(shape, nd):
    zeros = (0,) * len(shape)
    if nd == 1:
        return pl.BlockSpec(shape, lambda i: zeros)
    return pl.BlockSpec(shape, lambda i, j: zeros)


def _norm_fwd(name, h, g):
    T, D = h.shape
    tr = _tile(T, ROW_TILE, 8)
    return _fwd_call(name, _rms_fn, (T // tr,), [(h, _rows(tr, D)), (g, _whole((1, D), 1))],
                     [(_sds((T, D), BF16), _rows(tr, D), False)])[0]


def _norm_bwd(name, h, g, dhn, dh_res):
    T, D = h.shape
    tr = _tile(T, ROW_TILE, 8)
    return _bwd_call(name, _rms_fn, (T // tr,), [(h, _rows(tr, D)), (g, _whole((1, D), 1))], [(dhn, _rows(tr, D))],
                     [(0, _sds((T, D)), _rows(tr, D), False), (1, _sds((1, D)), _whole((1, D), 1), True)],
                     acc_axes=(0,), addends=[(0, dh_res, _rows(tr, D))])


def _ab_specs(B, S, P):
    W4 = 4 * LANES
    seq4 = pl.BlockSpec((S, W4), lambda b: (b, 0))
    pool_ins = lambda proj: [(proj, seq4), (P["pool_w"], _whole((4, LANES, LANES), 1)), (P["pool_scale"], _whole((1, W4), 1))]
    hb = lambda off: pl.BlockSpec((S, LANES), lambda b, h: (b, off + h))
    cw = lambda off: pl.BlockSpec((4, 1, LANES), lambda b, h: (0, 0, off + h))
    small = _whole((1, LANES), 2)
    prep_ins = lambda proj: [(proj, hb(4)), (proj, hb(8)), (proj, hb(12)),
                             (proj, pl.BlockSpec((S, LANES), lambda b, h: (b, 20))),
                             (P["conv_w"], cw(0)), (P["conv_w"], cw(4)), (P["conv_w"], cw(8)),
                             (P["a_log"], small), (P["dt_bias"], small)]
    post_ins = lambda o, proj: [(o, hb(0)), (proj, hb(16)), (P["out_norm"], small)]
    return seq4, pool_ins, hb, prep_ins, post_ins, small


def _ab_fwd(tag, h, P, B, S):
    T, D = h.shape
    W4 = 4 * LANES
    seq4, pool_ins, hb, prep_ins, post_ins, small = _ab_specs(B, S, P)
    proj, hn = _mm_norm(tag + "_in", h, P["mix_norm"], P["w_in"])
    ya = _fwd_call(tag + "_pool", _pool_fn, (B,), pool_ins(proj), [(_sds((T, W4), BF16), seq4, False)])[0]
    qn, kn, vn, beta, g = _fwd_call(tag + "_prep", lambda pids, *v: _prep_fn((pids[1],), *v), (B, 4), prep_ins(proj),
                                    [(_sds((T, W4)), hb(0), False)] * 5)
    o, states = _delta_fwd(tag + "_delta", qn, kn, vn, beta, g, B, S)
    yb = _fwd_call(tag + "_post", _post_fn, (B, 4), post_ins(o, proj), [(_sds((T, W4), BF16), hb(0), False)])[0]
    y = jnp.concatenate([ya, yb], axis=-1)
    h_new = _mm(tag + "_out", y, P["w_out"], res=h)
    return h_new, (h, hn, proj, qn, kn, vn, beta, g, states, o, y)


def _ab_bwd(tag, dh, saved, P, B, S):
    h, hn, proj, qn, kn, vn, beta, g, states, o, y = saved
    T, D = h.shape
    W4 = 4 * LANES
    seq4, pool_ins, hb, prep_ins, post_ins, small = _ab_specs(B, S, P)
    dy = _mm(tag + "_out_dx", dh, P["w_out"], tb=True)
    dw_out = _mm(tag + "_out_dw", y, dh, ta=True, tm=D, tk=1024)
    do, dz, d_onorm = _bwd_call(tag + "_post_b", _post_fn, (B, 4), post_ins(o, proj), [(dy, hb(4))],
                                [(0, _sds((T, W4)), hb(0), False), (1, _sds((T, W4)), hb(0), False),
                                 (2, _sds((1, LANES)), small, True)], acc_axes=(0, 1))
    dqn, dkn, dvn, dbeta, dg = _delta_bwd(tag + "_delta_b", qn, kn, vn, beta, g, states, do, B, S)
    per_b = pl.BlockSpec((None, 1, LANES), lambda b, h: (b, 0, 0))
    dcw = pl.BlockSpec((None, 4, 1, LANES), lambda b, h: (b, 0, 0, h))
    dq, dk, dv, dbg, dcq, dck, dcv, dalog, ddt = _bwd_call(
        tag + "_prep_b", lambda pids, *v: _prep_fn((pids[1],), *v), (B, 4), prep_ins(proj),
        [(dqn, hb(0)), (dkn, hb(0)), (dvn, hb(0)), (dbeta, hb(0)), (dg, hb(0))],
        [(0, _sds((T, W4)), hb(0), False), (1, _sds((T, W4)), hb(0), False), (2, _sds((T, W4)), hb(0), False),
         (3, _sds((T, LANES)), pl.BlockSpec((S, LANES), lambda b, h: (b, 0)), True),
         (4, _sds((B, 4, 1, W4)), dcw, False), (5, _sds((B, 4, 1, W4)), dcw, False),
         (6, _sds((B, 4, 1, W4)), dcw, False),
         (7, _sds((B, 1, LANES)), per_b, True), (8, _sds((B, 1, LANES)), per_b, True)], acc_axes=(1,))
    d_conv = jnp.concatenate([jnp.sum(d, axis=0)[:, 0, :] for d in (dcq, dck, dcv)], axis=-1)
    da, dpool_w, dpool_scale = _bwd_call(
        tag + "_pool_b", _pool_fn, (B,), pool_ins(proj), [(dy, seq4)],
        [(0, _sds((T, W4)), seq4, False), (1, _sds((4, LANES, LANES)), _whole((4, LANES, LANES), 1), True),
         (2, _sds((1, W4)), _whole((1, W4), 1), True)], acc_axes=(0,))
    dproj = [da, dq, dk, dv, dz, dbg]
    dh_new, d_mix = _mm_parts_dx(tag + "_in_dx", dproj, P["w_in"], h, P["mix_norm"], dh)
    dw_in = _mm_parts_dw(tag + "_in_dw", hn, dproj)
    grads = dict(w_in=dw_in, w_out=dw_out, mix_norm=d_mix[0], pool_w=dpool_w, pool_scale=dpool_scale[0],
                 conv_w=d_conv, a_log=jnp.sum(dalog, axis=0)[0, :4], dt_bias=jnp.sum(ddt, axis=0)[0, :4],
                 out_norm=d_onorm[0])
    return dh_new, grads


def _cd_specs(B, S, T, P):
    W4 = 4 * LANES
    R = _tile(T, SGU_ROWS, LANES)
    sgu_ins = lambda proj: [(proj, _rows(R, W4, 0)), (proj, _rows(R, W4, 1)), (P["sgu_g"], _whole((1, W4), 1)),
                            (P["sgu_b"], _whole((1, W4), 1)), (P["sgu_w"], _whole((4, LANES, LANES), 1)),
                            (P["bias_t"], _whole((LANES, LANES), 1))]
    jb = lambda off: pl.BlockSpec((S, LANES), lambda j, b: (b, off + j))
    sc_ins = lambda proj: [(proj, jb(8)), (proj, jb(12)), (proj, jb(16)),
                           (P["sc_w"], pl.BlockSpec((3, 1, LANES), lambda j, b: (0, 0, j)))]
    return R, sgu_ins, jb, sc_ins


def _cd_fwd(tag, h, P, B, S):
    T, D = h.shape
    W4 = 4 * LANES
    R, sgu_ins, jb, sc_ins = _cd_specs(B, S, T, P)
    proj, hn = _mm_norm(tag + "_in", h, P["mix_norm"], P["w_in"])
    yc = _fwd_call(tag + "_sgu", _sgu_fn, (T // R,), sgu_ins(proj), [(_sds((T, W4), BF16), _rows(R, W4), False)])[0]
    yd = _fwd_call(tag + "_sconv", _sconv_fn, (4, B), sc_ins(proj), [(_sds((T, W4), BF16), jb(0), False)])[0]
    y = jnp.concatenate([yc, yd], axis=-1)
    h_new = _mm(tag + "_out", y, P["w_out"], res=h)
    return h_new, (h, hn, proj, y)


def _cd_bwd(tag, dh, saved, P, B, S):
    h, hn, proj, y = saved
    T, D = h.shape
    W4 = 4 * LANES
    R, sgu_ins, jb, sc_ins = _cd_specs(B, S, T, P)
    dy = _mm(tag + "_out_dx", dh, P["w_out"], tb=True)
    dw_out = _mm(tag + "_out_dw", y, dh, ta=True, tm=D, tk=1024)
    du, dv, dsg, dsb, dsw, dbias_t = _bwd_call(
        tag + "_sgu_b", _sgu_fn, (T // R,), sgu_ins(proj), [(dy, _rows(R, W4, 0))],
        [(0, _sds((T, W4)), _rows(R, W4), False), (1, _sds((T, W4)), _rows(R, W4), False),
         (2, _sds((1, W4)), _whole((1, W4), 1), True), (3, _sds((1, W4)), _whole((1, W4), 1), True),
         (4, _sds((4, LANES, LANES)), _whole((4, LANES, LANES), 1), True),
         (5, _sds((LANES, LANES)), _whole((LANES, LANES), 1), True)], acc_axes=(0,))
    dxd, dbgate, dcg, dsc = _bwd_call(
        tag + "_sconv_b", _sconv_fn, (4, B), sc_ins(proj), [(dy, jb(4))],
        [(0, _sds((T, W4)), jb(0), False), (1, _sds((T, W4)), jb(0), False), (2, _sds((T, W4)), jb(0), False),
         (3, _sds((3, 1, W4)), pl.BlockSpec((3, 1, LANES), lambda j, b: (0, 0, j)), True)], acc_axes=(1,))
    dproj = [du, dv, dxd, dbgate, dcg]
    dh_new, d_mix = _mm_parts_dx(tag + "_in_dx", dproj, P["w_in"], h, P["mix_norm"], dh)
    dw_in = _mm_parts_dw(tag + "_in_dw", hn, dproj)
    grads = dict(w_in=dw_in, w_out=dw_out, mix_norm=d_mix[0], sgu_g=dsg[0], sgu_b=dsb[0], sgu_w=dsw,
                 sgu_bias=dbias_t[:, :4].T, sc_w=dsc[:, 0, :])
    return dh_new, grads


def _loss_fwd_bwd(h, g, tgt):
    T, D = h.shape
    tr = _tile(T, ROW_TILE, 8)
    ins = [(h, _rows(tr, D)), (g, _whole((1, D), 1)), (tgt, _rows(tr, D))]
    vec = _whole((1, LANES), 1)
    loss = _fwd_call("loss", _loss_fn, (T // tr,), ins, [(_sds((1, LANES)), vec, True)], acc_axes=(0,))[0]
    one = jnp.zeros((1, LANES), F32).at[0, 0].set(1.0)
    dh, dg = _bwd_call("loss_b", _loss_fn, (T // tr,), ins, [(one, vec)],
                       [(0, _sds((T, D)), _rows(tr, D), False), (1, _sds((1, D)), _whole((1, D), 1), True)],
                       acc_axes=(0,))
    return loss[0, 0], dh, dg[0]


def _local_step(x2, tgt2, W, B, S):
    L = W["ffn1_norm"].shape[0]
    tm = _tile(x2.shape[0], FFN_TILE, 8)
    tmx = _tile(x2.shape[0], FFN_TILE_BX, 8)
    h = x2
    saved = []
    for l in range(L):
        e = l // 2
        f1 = (W["ffn1_norm"][l][None], W["ffn1_g"], W["ffn1_u"], W["ffn1_d"])
        f2 = (W["ffn2_norm"][l][None], W["ffn2_g"], W["ffn2_u"], W["ffn2_d"])
        h0 = h
        h, gate1, up1, hn1 = _ffn_fwd(f"l{l}_ffn1", h0, *f1, l, tm)
        if l % 2 == 0:
            P = dict(mix_norm=W["mix_norm"][l][None], w_in=W["ab_in"][e], w_out=W["ab_out"][e], pool_w=W["pool_w"][e],
                     pool_scale=W["pool_scale"][e][None], conv_w=W["dn_conv_w"][e][:, None, :],
                     a_log=jnp.pad(W["dn_a_log"][e][None], ((0, 0), (0, LANES - 4))),
                     dt_bias=jnp.pad(W["dn_dt_bias"][e][None], ((0, 0), (0, LANES - 4))),
                     out_norm=W["dn_out_norm"][e][None])
            h1 = h
            h, msave = _ab_fwd(f"l{l}_ab", h1, P, B, S)
        else:
            P = dict(mix_norm=W["mix_norm"][l][None], w_in=W["cd_in"][e], w_out=W["cd_out"][e],
                     sgu_g=W["sgu_norm_g"][e][None], sgu_b=W["sgu_norm_b"][e][None], sgu_w=W["sgu_w"][e],
                     bias_t=jnp.pad(W["sgu_bias"][e].T, ((0, 0), (0, LANES - 4))),
                     sc_w=W["sc_conv_w"][e][:, None, :])
            h1 = h
            h, msave = _cd_fwd(f"l{l}_cd", h1, P, B, S)
        h2 = h
        h, gate2, up2, hn2 = _ffn_fwd(f"l{l}_ffn2", h2, *f2, l, tm)
        saved.append((f1, f2, P, h0, gate1, up1, hn1, msave, h2, gate2, up2, hn2))

    loss, dh, d_final = _loss_fwd_bwd(h, W["final_norm"][None], tgt2)

    G = {k: [None] * L for k in ("ffn1_norm", "ffn2_norm", "mix_norm")}
    bufs1 = bufs2 = None
    GA, GC = [None] * ((L + 1) // 2), [None] * (L // 2)
    for l in reversed(range(L)):
        f1, f2, P, h0, gate1, up1, hn1, msave, h2, gate2, up2, hn2 = saved[l]
        dh, dgate, dup, dn2, do = _ffn_bwd_x(f"l{l}_ffn2_bx", dh, h2, f2[0], gate2, up2, f2[1], f2[2], f2[3], l, tmx)
        bufs2 = _ffn_bwd_w(f"l{l}_ffn2_bw", hn2, do, gate2, up2, dgate, dup, bufs2, l, L, tm)
        G["ffn2_norm"][l] = dn2[0]
        if l % 2 == 0:
            dh, mg = _ab_bwd(f"l{l}_ab", dh, msave, P, B, S)
            GA[l // 2] = mg
        else:
            dh, mg = _cd_bwd(f"l{l}_cd", dh, msave, P, B, S)
            GC[l // 2] = mg
        G["mix_norm"][l] = mg["mix_norm"]
        dh, dgate, dup, dn1, do = _ffn_bwd_x(f"l{l}_ffn1_bx", dh, h0, f1[0], gate1, up1, f1[1], f1[2], f1[3], l, tmx)
        bufs1 = _ffn_bwd_w(f"l{l}_ffn1_bw", hn1, do, gate1, up1, dgate, dup, bufs1, l, L, tm)
        G["ffn1_norm"][l] = dn1[0]

    st = lambda xs: jnp.stack(xs, axis=0)
    big = dict(ffn1_w_gate=bufs1[0], ffn1_w_up=bufs1[1], ffn1_w_down=bufs1[2],
               ffn2_w_gate=bufs2[0], ffn2_w_up=bufs2[1], ffn2_w_down=bufs2[2],
               ab_w_in=st([m["w_in"] for m in GA]), ab_w_out=st([m["w_out"] for m in GA]),
               cd_w_in=st([m["w_in"] for m in GC]), cd_w_out=st([m["w_out"] for m in GC]))
    small = dict(ffn1_norm=st(G["ffn1_norm"]), mix_norm=st(G["mix_norm"]), ffn2_norm=st(G["ffn2_norm"]),
                 pool_w=st([m["pool_w"] for m in GA]), pool_scale=st([m["pool_scale"] for m in GA]),
                 dn_conv_w=st([m["conv_w"] for m in GA]), dn_a_log=st([m["a_log"] for m in GA]),
                 dn_dt_bias=st([m["dt_bias"] for m in GA]), dn_out_norm=st([m["out_norm"] for m in GA]),
                 sgu_norm_g=st([m["sgu_g"] for m in GC]), sgu_norm_b=st([m["sgu_b"] for m in GC]),
                 sgu_w=st([m["sgu_w"] for m in GC]), sgu_bias=st([m["sgu_bias"] for m in GC]),
                 sc_conv_w=st([m["sc_w"] for m in GC]), final_norm=d_final)
    return loss, dh, big, small


ANY = pl.BlockSpec(memory_space=pl.ANY)


def _place():
    return lax.axis_index("x"), lax.axis_index("y"), lax.axis_index("c")


def _exchange(name, srcs, out_shapes, plan, n_copies, bases=None):
    n, m = len(srcs), len(out_shapes)
    nb = m if bases is not None else 0

    def body(*refs):
        ins, outs = refs[:n], refs[n + nb:n + nb + m]
        send, recv = refs[n + nb + m:]
        remote = plan(_place(), ins, outs)
        assert len(remote) == n_copies
        sends = []
        for k, (s, d, peer, _) in enumerate(remote):
            cp = pltpu.make_async_remote_copy(src_ref=s, dst_ref=d, send_sem=send.at[k], recv_sem=recv.at[k],
                                              device_id=peer, device_id_type=MESH)
            cp.start()
            sends.append(cp)
        for k, (s, _, peer, land) in enumerate(remote):
            pltpu.make_async_remote_copy(src_ref=s, dst_ref=land, send_sem=send.at[k], recv_sem=recv.at[k],
                                         device_id=peer, device_id_type=MESH).wait_recv()
        for cp in sends:
            cp.wait_send()

    return _pcall(body, in_specs=[ANY] * (n + nb), out_specs=[ANY] * m, out_shape=out_shapes,
                  input_output_aliases={n + k: k for k in range(nb)},
                  scratch_shapes=[pltpu.SemaphoreType.DMA((n_copies,)), pltpu.SemaphoreType.DMA((n_copies,))],
                  name=name)(*srcs, *(bases or ()))


def _other_chips(x, y):
    return [(1 - x, y), (x, 1 - y), (1 - x, 1 - y)]


def _gather_chips(name, xs):
    n = len(xs)
    split = [a.ndim >= 3 and a.shape[0] % 2 == 0 for a in xs]
    n_fwd = sum(split)
    fwd_of = {t: j for j, t in enumerate(t for t in range(n) if split[t])}

    def body(*refs):
        ins, outs = refs[:n], refs[n:2 * n]
        send0, recv0, send1, recv1, send2, recv2 = refs[2 * n:]
        x, y, c = _place()
        me = 2 * x + y
        chips = _other_chips(x, y)

        def mine(t):
            return pltpu.make_async_remote_copy(src_ref=ins[t], dst_ref=outs[t].at[me], send_sem=send0.at[t],
                                                recv_sem=recv0.at[t], device_id=(x, y, 1 - c), device_id_type=MESH)

        def part(t, cc):
            half = xs[t].shape[0] // 2
            return pl.ds(cc * half, half) if split[t] else pl.ds(0, xs[t].shape[0])

        def first(r, t, started):
            px, py = chips[r]
            dst = outs[t].at[me, part(t, c)] if started else outs[t].at[2 * px + py, part(t, c)]
            return pltpu.make_async_remote_copy(src_ref=ins[t].at[part(t, c)], dst_ref=dst, send_sem=send1.at[r, t],
                                                recv_sem=recv1.at[r, t], device_id=(px, py, c), device_id_type=MESH)

        def second(r, t, started):
            px, py = chips[r]
            rows = part(t, c) if started else part(t, 1 - c)
            blk = outs[t].at[2 * px + py, rows]
            return pltpu.make_async_remote_copy(src_ref=blk, dst_ref=blk, send_sem=send2.at[r, fwd_of[t]],
                                                recv_sem=recv2.at[r, fwd_of[t]], device_id=(x, y, 1 - c),
                                                device_id_type=MESH)

        sends = [first(r, t, True) for r in range(3) for t in range(n)] + [mine(t) for t in range(n)]
        for cp in sends:
            cp.start()
        for r in range(3):
            for t in range(n):
                first(r, t, False).wait_recv()
                if split[t]:
                    cp = second(r, t, True)
                    cp.start()
                    sends.append(cp)
        for r in range(3):
            for t in range(n):
                if split[t]:
                    second(r, t, False).wait_recv()
        for t in range(n):
            mine(t).wait_recv()
        for cp in sends:
            cp.wait_send()

    return _pcall(body, in_specs=[ANY] * n, out_specs=[ANY] * n,
                  out_shape=[_sds((N_CHIPS,) + a.shape, a.dtype) for a in xs],
                  scratch_shapes=[pltpu.SemaphoreType.DMA((n,)), pltpu.SemaphoreType.DMA((n,)),
                                  pltpu.SemaphoreType.DMA((3, n)), pltpu.SemaphoreType.DMA((3, n)),
                                  pltpu.SemaphoreType.DMA((3, max(n_fwd, 1))), pltpu.SemaphoreType.DMA((3, max(n_fwd, 1)))],
                  name=name)(*xs)


def _pair_split(name, gs):
    n = len(gs)

    def plan(place, ins, outs):
        x, y, c = place
        return [(ins[t].at[:, :, 1 - c], outs[t], (x, y, 1 - c), outs[t]) for t in range(n)]

    return _exchange(name, gs, [_sds(a.shape[:2] + a.shape[3:], a.dtype) for a in gs], plan, n)


def _chip_scatter(name, ps, bases, v, vbase):
    n = len(ps)
    flips = [(fx, fy, fc) for fx in (0, 1) for fy in (0, 1) for fc in (0, 1)][1:]

    def plan(place, ins, outs):
        x, y, c = place
        me = 2 * x + y
        remote = [(ins[t].at[:, 2 * px + py], outs[t].at[:, me], (px, py, c), outs[t].at[:, 2 * px + py])
                  for (px, py) in _other_chips(x, y) for t in range(n)]
        for fx, fy, fc in flips:
            px, py, pc = (1 - x if fx else x), (1 - y if fy else y), (1 - c if fc else c)
            remote.append((ins[n], outs[n].at[2 * me + c], (px, py, pc), outs[n].at[4 * px + 2 * py + pc]))
        return remote

    shapes = [_sds(a.shape, a.dtype) for a in ps] + [_sds((8,) + v.shape, v.dtype)]
    res = _exchange(name, list(ps) + [v], shapes, plan, 3 * n + len(flips), list(bases) + [vbase])
    return res[:n], res[n]


def _pair_share(name, ts):
    n = len(ts)

    def plan(place, ins, outs):
        x, y, c = place
        return [(ins[t], outs[t], (x, y, 1 - c), outs[t]) for t in range(n)]

    return _exchange(name, ts, [_sds(a.shape, a.dtype) for a in ts], plan, n)


def _gather_devices(name, v, base):
    flips = [(fx, fy, fc) for fx in (0, 1) for fy in (0, 1) for fc in (0, 1)][1:]

    def plan(place, ins, outs):
        x, y, c = place
        me = 4 * x + 2 * y + c
        remote = []
        for fx, fy, fc in flips:
            px, py, pc = (1 - x if fx else x), (1 - y if fy else y), (1 - c if fc else c)
            remote.append((ins[0], outs[0].at[me], (px, py, pc), outs[0].at[4 * px + 2 * py + pc]))
        return remote

    return _exchange(name, [v], [_sds((8,) + v.shape, v.dtype)], plan, len(flips), [base])[0]


def _sum_slots(name, a):
    k, rows, cols = a.shape
    tr = _tile(rows, ROW_TILE, 8)
    ins = [(a, pl.BlockSpec((None, tr, cols), lambda i, j=j: (j, i, 0))) for j in range(k)]
    return _fwd_call(name, lambda pids, *v: (functools.reduce(lambda p, q: p + q, v),), (rows // tr,), ins,
                     [(_sds((rows, cols)), _rows(tr, cols), False)])[0]


def _half_tile(ah):
    return _tile(ah, 512, 8)


def _pair_sum(name, g5, theirs, cf, out_dtype):
    L, P4, _, Ah, Bt = g5.shape
    ta = _half_tile(Ah)
    half = lambda hh: pl.BlockSpec((None, None, None, ta, Bt), lambda l, p, i: (l, p, hh, i, 0))
    blk = pl.BlockSpec((None, None, ta, Bt), lambda l, p, i: (l, p, i, 0))

    def fn(pids, g0, g1, r, c):
        tot = jnp.where(jnp.max(c) > 0.5, g1, g0) + r
        return tot, tot

    return _fwd_call(name, fn, (L, P4, Ah // ta), [(g5, half(0)), (g5, half(1)), (theirs, blk),
                                                   (cf, pl.BlockSpec((1, LANES), lambda l, p, i: (0, 0)))],
                     [(_sds((L, P4, Ah, Bt), out_dtype), blk, False)] * 2)


def _sum_chips(name, parts):
    L, P4, Ah, Bt = parts.shape
    ta = _half_tile(Ah)
    ins = [(parts, pl.BlockSpec((None, None, ta, Bt), lambda l, i, q=q: (l, q, i, 0))) for q in range(P4)]
    return _fwd_call(name, lambda pids, *v: (functools.reduce(lambda p, q: p + q, v),), (L, Ah // ta), ins,
                     [(_sds((L, Ah, Bt)), pl.BlockSpec((None, ta, Bt), lambda l, i: (l, i, 0)), False)])[0]


def _adam_terms(w, g, m, v):
    m2 = ADAM_B1 * m + (1.0 - ADAM_B1) * g
    v2 = ADAM_B2 * v + (1.0 - ADAM_B2) * (g * g)
    m_hat = m2 / (1.0 - ADAM_B1 ** ADAM_STEP)
    v_hat = v2 / (1.0 - ADAM_B2 ** ADAM_STEP)
    return -ADAM_LR * (m_hat / (jnp.sqrt(v_hat) + ADAM_EPS) + ADAM_WD * w), m2, v2


def _adam_halves(name, w, m, v, mine, theirs, cf):
    L, Aa, Bt = w.shape
    Ah = Aa // 2
    ta = _half_tile(Ah)
    full = pl.BlockSpec((None, None, ta, Bt), lambda l, hh, i: (l, hh, i, 0))
    part = pl.BlockSpec((None, ta, Bt), lambda l, hh, i: (l, i, 0))

    def fn(pids, w_, m_, v_, a, b, c):
        g = jnp.where(jnp.max(c) == pids[1].astype(F32), a, b)
        return (g,) + _adam_terms(w_, g, m_, v_)

    ins = [(a.reshape(L, 2, Ah, Bt), full) for a in (w, m, v)] + [(mine, part), (theirs, part),
                                                                 (cf, pl.BlockSpec((1, LANES), lambda l, hh, i: (0, 0)))]
    outs = _fwd_call(name, fn, (L, 2, Ah // ta), ins, [(_sds((L, 2, Ah, Bt)), full, False)] * 4)
    return [o.reshape(w.shape) for o in outs]


def _adam_rows(name, w, g, m, v):
    rows, cols = w.shape
    tr = _tile(rows, ROW_TILE, 8)
    ins = [(a, _rows(tr, cols)) for a in (w, g, m, v)]
    return _fwd_call(name, lambda pids, *a: _adam_terms(*a), (rows // tr,), ins,
                     [(_sds((rows, cols)), _rows(tr, cols), False)] * 3)


def _pack(xs):
    rows = []
    for a in xs:
        flat = a.reshape(-1).astype(F32)
        rows.append(jnp.pad(flat, (0, (-flat.size) % LANES)).reshape(-1, LANES))
    buf = jnp.concatenate(rows, axis=0)
    return jnp.pad(buf, ((0, (-buf.shape[0]) % 8), (0, 0)))


def _unpack(buf, shapes):
    out, row = [], 0
    for s in shapes:
        n = math.prod(s)
        nr = -(-n // LANES)
        out.append(buf[row:row + nr].reshape(-1)[:n].reshape(s))
        row += nr
    return out


_WEIGHTS = ("ffn1_norm", "ffn1_w_gate", "ffn1_w_up", "ffn1_w_down", "mix_norm", "ffn2_norm", "ffn2_w_gate", "ffn2_w_up",
            "ffn2_w_down", "ab_w_in", "pool_w", "pool_scale", "dn_conv_w", "dn_a_log", "dn_dt_bias", "dn_out_norm",
            "ab_w_out", "cd_w_in", "sgu_norm_g", "sgu_norm_b", "sgu_w", "sgu_bias", "sc_conv_w", "cd_w_out", "final_norm")
_BIG = ("ffn1_w_gate", "ffn1_w_up", "ffn1_w_down", "ffn2_w_gate", "ffn2_w_up", "ffn2_w_down", "ab_w_in", "ab_w_out",
        "cd_w_in", "cd_w_out")
_ROW_SHARDED = ("ffn1_w_down", "ffn2_w_down", "ab_w_out", "cd_w_out")
_SMALL_SHARDED = ("dn_conv_w", "sgu_norm_g", "sgu_norm_b", "sc_conv_w")
_SMALL = tuple(n for n in _WEIGHTS if n not in _BIG)


def _to_slots(name, g):
    L, A, Bt = g.shape
    if name in _ROW_SHARDED:
        return g.reshape(L, N_CHIPS, A // N_CHIPS, Bt)
    return g.reshape(L, A, N_CHIPS, Bt // N_CHIPS).transpose(0, 2, 1, 3)


def kernel(x, ffn1_norm, ffn1_w_gate, ffn1_w_up, ffn1_w_down, mix_norm, ffn2_norm, ffn2_w_gate, ffn2_w_up, ffn2_w_down,
           ab_w_in, pool_w, pool_scale, dn_conv_w, dn_a_log, dn_dt_bias, dn_out_norm, ab_w_out, cd_w_in, sgu_norm_g,
           sgu_norm_b, sgu_w, sgu_bias, sc_conv_w, cd_w_out, final_norm, loss_target,
           m_ffn1_norm, m_ffn1_w_gate, m_ffn1_w_up, m_ffn1_w_down, m_mix_norm, m_ffn2_norm, m_ffn2_w_gate, m_ffn2_w_up,
           m_ffn2_w_down, m_ab_w_in, m_pool_w, m_pool_scale, m_dn_conv_w, m_dn_a_log, m_dn_dt_bias, m_dn_out_norm,
           m_ab_w_out, m_cd_w_in, m_sgu_norm_g, m_sgu_norm_b, m_sgu_w, m_sgu_bias, m_sc_conv_w, m_cd_w_out, m_final_norm,
           v_ffn1_norm, v_ffn1_w_gate, v_ffn1_w_up, v_ffn1_w_down, v_mix_norm, v_ffn2_norm, v_ffn2_w_gate, v_ffn2_w_up,
           v_ffn2_w_down, v_ab_w_in, v_pool_w, v_pool_scale, v_dn_conv_w, v_dn_a_log, v_dn_dt_bias, v_dn_out_norm,
           v_ab_w_out, v_cd_w_in, v_sgu_norm_g, v_sgu_norm_b, v_sgu_w, v_sgu_bias, v_sc_conv_w, v_cd_w_out, v_final_norm):
    A = dict(locals())
    B, S, D = x.shape
    T = B * S
    xi, yi, ci = _place()
    chip = 2 * xi + yi
    cf = jnp.broadcast_to(ci.astype(F32), (1, LANES))
    own = lambda n, mine, slot: lax.dynamic_update_index_in_dim(lax.empty((n,) + mine.shape, mine.dtype), mine, slot, 0)

    sh_shapes = [A[n].shape for n in _SMALL_SHARDED]
    local = [A[n].astype(BF16) for n in _BIG] + [_pack([A[n] for n in _SMALL_SHARDED])]
    gathered = _gather_chips("gather_weights", local)
    gw = dict(zip(_BIG, gathered[:-1]))
    per_chip = [_unpack(gathered[-1][p], sh_shapes) for p in range(N_CHIPS)]
    W = {n: A[n] for n in _SMALL if n not in _SMALL_SHARDED}
    for j, n in enumerate(_SMALL_SHARDED):
        W[n] = jnp.concatenate([per_chip[p][j] for p in range(N_CHIPS)], axis=-1)
    for f in ("ffn1", "ffn2"):
        W[f + "_g"], W[f + "_u"], W[f + "_d"] = gw[f + "_w_gate"], gw[f + "_w_up"], gw[f + "_w_down"]
    cols = lambda g: jnp.concatenate([g[p] for p in range(N_CHIPS)], axis=-1)
    rows = lambda g: jnp.concatenate([g[p] for p in range(N_CHIPS)], axis=1)
    ab_in = cols(gw["ab_w_in"])
    ab_cols = ab_in.shape[-1]
    ab_pad = (-ab_cols) % LANES
    W["ab_in"] = jnp.pad(ab_in, ((0, 0), (0, 0), (0, ab_pad)))
    W["cd_in"], W["ab_out"], W["cd_out"] = cols(gw["cd_w_in"]), rows(gw["ab_w_out"]), rows(gw["cd_w_out"])

    loss, dx, big, small = _local_step(x.reshape(T, D), loss_target.reshape(T, D), W, B, S)
    loss = lax.psum(loss, ("x", "y", "c"))

    big["ab_w_in"] = big["ab_w_in"][:, :, :ab_cols]
    slots = []
    for n in _BIG:
        g = big[n] if big[n].ndim == 4 else _to_slots(n, big[n])
        L, P4, Aa, Bt = g.shape
        slots.append(g.reshape(L, P4, 2, Aa // 2, Bt))
    theirs = _pair_split("grads_pair_split", slots)
    pair = [_pair_sum(f"grads_pair_sum_{n}", g5, r, cf, BF16) for n, g5, r in zip(_BIG, slots, theirs)]
    sm_shapes = [small[n].shape for n in _SMALL]
    sm_local = _pack([small[n] for n in _SMALL])
    parts, every = _chip_scatter("grads_chip_scatter", [p for p, _ in pair], [p for _, p in pair],
                                 sm_local, own(8, sm_local, 2 * chip + ci))
    tot = [_sum_chips(f"grads_chip_sum_{n}", p) for n, p in zip(_BIG, parts)]
    other = _pair_share("grads_pair_share", tot)

    grads = {}
    for n, g in zip(_SMALL, _unpack(_sum_slots("small_grads_sum", every), sm_shapes)):
        if n in _SMALL_SHARDED:
            w_loc = A[n].shape[-1]
            g = lax.dynamic_slice_in_dim(g, chip * w_loc, w_loc, axis=g.ndim - 1)
        grads[n] = g.reshape(A[n].shape)

    delta, new_m, new_v = {}, {}, {}
    for n, a, b in zip(_BIG, tot, other):
        grads[n], delta[n], new_m[n], new_v[n] = _adam_halves(f"adam_{n}", A[n], A["m_" + n], A["v_" + n], a, b, cf)
    packed = [_pack([d[n] for n in _SMALL]) for d in (A, grads, {n: A["m_" + n] for n in _SMALL},
                                                       {n: A["v_" + n] for n in _SMALL})]
    loc_shapes = [A[n].shape for n in _SMALL]
    for d, buf in zip((delta, new_m, new_v), _adam_rows("adam_small", *packed)):
        d.update(zip(_SMALL, _unpack(buf, loc_shapes)))

    return (loss, dx.reshape(B, S, D), *[grads[n] for n in _WEIGHTS], *[delta[n] for n in _WEIGHTS],
            *[new_m[n] for n in _WEIGHTS], *[new_v[n] for n in _WEIGHTS])
```
